```python
import math
import jax, jax.numpy as jnp
from jax import lax
import numpy as np

D_MODEL = 1024
BATCH = 8
SEQ = 8192
DEPTH = 4

CHUNK = 64
N_PAST_CHUNKS = 8
N_MIXERS = 2
N_HEADS = 16
HEAD_DIM = 64
MIX_WIDTH = N_HEADS * HEAD_DIM
MAX_REL = 128
Q_BLOCK = 128
N_A_LAYERS = (DEPTH + N_MIXERS - 1) // N_MIXERS
N_B_LAYERS = DEPTH // N_MIXERS
LN_EPS = 1e-5
NEG_INF = -1e30
DEEPNORM_ALPHA = (2.0 * DEPTH) ** 0.25
DEEPNORM_BETA = (8.0 * DEPTH) ** -0.25

kernel_name = "hybrid_chunk_relpos_fox_deepnorm"


def layer_norm(x, g, b):
    xf = x.astype(jnp.float32)
    mu = jnp.mean(xf, axis=-1, keepdims=True)
    var = jnp.mean(jnp.square(xf - mu), axis=-1, keepdims=True)
    y = (xf - mu) * lax.rsqrt(var + LN_EPS) * g.astype(jnp.float32) + b.astype(jnp.float32)
    return y.astype(x.dtype)


def split_heads(t):
    b, s, _ = t.shape
    return t.reshape(b, s, N_HEADS, HEAD_DIM).transpose(0, 2, 1, 3)


def merge_heads(t):
    b, h, s, d = t.shape
    return t.transpose(0, 2, 1, 3).reshape(b, s, h * d)


def chunk_relpos_attention(q, k, v, rel_bias):
    b, h, s, d = q.shape
    n_chunks = s // CHUNK
    pad = N_PAST_CHUNKS * CHUNK
    band = (N_PAST_CHUNKS + 1) * CHUNK
    kp = jnp.pad(k, ((0, 0), (0, 0), (pad, 0), (0, 0)))
    vp = jnp.pad(v, ((0, 0), (0, 0), (pad, 0), (0, 0)))
    qi = jnp.arange(CHUNK)[:, None]
    kj = jnp.arange(band)[None, :]
    rel_idx = jnp.clip(qi - kj + pad, -MAX_REL, MAX_REL) + MAX_REL
    bias = rel_bias.astype(jnp.float32)[:, rel_idx]
    scale = 1.0 / math.sqrt(d)

    def one_chunk(c):
        start = c * CHUNK
        qc = lax.dynamic_slice_in_dim(q, start, CHUNK, axis=2)
        kc = lax.dynamic_slice_in_dim(kp, start, band, axis=2)
        vc = lax.dynamic_slice_in_dim(vp, start, band, axis=2)
        logits = jnp.einsum('bhqd,bhkd->bhqk', qc, kc).astype(jnp.float32) * scale + bias
        valid = (start - pad + jnp.arange(band)) >= 0
        logits = jnp.where(valid[None, None, None, :], logits, NEG_INF)
        p = jax.nn.softmax(logits, axis=-1).astype(v.dtype)
        return jnp.einsum('bhqk,bhkd->bhqd', p, vc)

    out = lax.map(one_chunk, jnp.arange(n_chunks))
    return out.transpose(1, 2, 0, 3, 4).reshape(b, h, s, d)


def forgetting_attention(q, k, v, log_f):
    b, h, s, d = q.shape
    n_blocks = s // Q_BLOCK
    cum_f = jnp.cumsum(log_f, axis=-1)
    q_blocks = q.reshape(b, h, n_blocks, Q_BLOCK, d).transpose(2, 0, 1, 3, 4)
    f_blocks = cum_f.reshape(b, h, n_blocks, Q_BLOCK).transpose(2, 0, 1, 3)
    kpos = jnp.arange(s)
    scale = 1.0 / math.sqrt(d)

    def one_block(args):
        qb, fb, blk = args
        qpos = blk * Q_BLOCK + jnp.arange(Q_BLOCK)
        logits = jnp.einsum('bhqd,bhkd->bhqk', qb, k).astype(jnp.float32) * scale
        logits = logits + fb[..., :, None] - cum_f[..., None, :]
        mask = kpos[None, :] <= qpos[:, None]
        logits = jnp.where(mask[None, None], logits, NEG_INF)
        p = jax.nn.softmax(logits, axis=-1).astype(v.dtype)
        return jnp.einsum('bhqk,bhkd->bhqd', p, v)

    out = lax.map(one_block, (q_blocks, f_blocks, jnp.arange(n_blocks)))
    return out.transpose(1, 2, 0, 3, 4).reshape(b, h, s, d)


def mixer_a(x, w_in, rel_bias, w_out):
    h = x @ w_in
    q, k, v, z = jnp.split(h, 4, axis=-1)
    o = merge_heads(chunk_relpos_attention(split_heads(q), split_heads(k), split_heads(v), rel_bias))
    return (o * jax.nn.silu(z)) @ w_out


def mixer_b(x, w_in, w_f, b_f, w_out):
    h = x @ w_in
    q, k, v, z = jnp.split(h, 4, axis=-1)
    log_f = jax.nn.log_sigmoid((x @ w_f + b_f).astype(jnp.float32))
    log_f = log_f.transpose(0, 2, 1)
    o = merge_heads(forgetting_attention(split_heads(q), split_heads(k), split_heads(v), log_f))
    return (o * jax.nn.silu(z)) @ w_out


def _in_proj(key, n):
    w = jax.random.normal(key, (n, D_MODEL, 4 * MIX_WIDTH), jnp.float32) * D_MODEL ** -0.5
    col_scale = jnp.concatenate([
        jnp.ones((2 * MIX_WIDTH,), jnp.float32),
        jnp.full((MIX_WIDTH,), DEEPNORM_BETA, jnp.float32),
        jnp.ones((MIX_WIDTH,), jnp.float32)])
    return w * col_scale


def _out_proj(key, n):
    return jax.random.normal(key, (n, MIX_WIDTH, D_MODEL), jnp.float32) * (MIX_WIDTH ** -0.5) * DEEPNORM_BETA


def setup_inputs(seed: int = 0) -> dict:
    key = jax.random.key(seed)
    ks = jax.random.split(key, 10)
    x = jax.random.normal(ks[0], (BATCH, SEQ, D_MODEL), jnp.float32)
    w_in_a = _in_proj(ks[1], N_A_LAYERS)
    rel_bias_a = jax.random.normal(ks[2], (N_A_LAYERS, N_HEADS, 2 * MAX_REL + 1), jnp.float32) * 0.1
    w_out_a = _out_proj(ks[3], N_A_LAYERS)
    w_in_b = _in_proj(ks[4], N_B_LAYERS)
    w_f_b = jax.random.normal(ks[5], (N_B_LAYERS, D_MODEL, N_HEADS), jnp.float32) * D_MODEL ** -0.5
    b_f_b = 2.0 + 0.5 * jax.random.normal(ks[6], (N_B_LAYERS, N_HEADS), jnp.float32)
    w_out_b = _out_proj(ks[7], N_B_LAYERS)
    ln_g = 1.0 + 0.05 * jax.random.normal(ks[8], (DEPTH, D_MODEL), jnp.float32)
    ln_b = 0.02 * jax.random.normal(ks[9], (DEPTH, D_MODEL), jnp.float32)
    return {"x": x, "w_in_a": w_in_a, "rel_bias_a": rel_bias_a, "w_out_a": w_out_a,
            "w_in_b": w_in_b, "w_f_b": w_f_b, "b_f_b": b_f_b, "w_out_b": w_out_b,
            "ln_g": ln_g, "ln_b": ln_b}


def reference(x, w_in_a, rel_bias_a, w_out_a, w_in_b, w_f_b, b_f_b, w_out_b, ln_g, ln_b):
    for i in range(DEPTH):
        j = i // N_MIXERS
        if i % N_MIXERS == 0:
            y = mixer_a(x, w_in_a[j], rel_bias_a[j], w_out_a[j])
        else:
            y = mixer_b(x, w_in_b[j], w_f_b[j], b_f_b[j], w_out_b[j])
        x = layer_norm(DEEPNORM_ALPHA * x + y, ln_g[i], ln_b[i])
    return x
```

```python
import functools
import math

import jax
import jax.numpy as jnp
from jax import lax
from jax.experimental import pallas as pl
from jax.experimental.pallas import tpu as pltpu

D_MODEL = 1024
N_HEADS = 16
HEAD_DIM = 64
MIX_WIDTH = N_HEADS * HEAD_DIM
CHUNK = 64
N_PAST_CHUNKS = 8
MAX_REL = 128
DEPTH = 4
N_MIXERS = 2
LN_EPS = 1e-5
DEEPNORM_ALPHA = (2.0 * DEPTH) ** 0.25
LOG2E = math.log2(math.e)
NEG = -1e30

LANES = 128
HEADS_PER_TILE = LANES // HEAD_DIM
N_HEAD_PAIRS = N_HEADS // HEADS_PER_TILE
TQ = 256
TKV = 256
TM_PROJ = 512
TB_PREP = 512
VMEM_LIMIT = 56 * 1024 * 1024


def _params(*semantics):
    return pltpu.CompilerParams(dimension_semantics=semantics, vmem_limit_bytes=VMEM_LIMIT)


def _in_proj_kernel(x_ref, w_ref, *out_refs):
    xb = x_ref[...].astype(jnp.bfloat16)
    for g in range(4):
        out_refs[g][...] = jnp.dot(
            xb, w_ref[:, g * MIX_WIDTH:(g + 1) * MIX_WIDTH],
            preferred_element_type=jnp.float32).astype(jnp.bfloat16)
    if len(out_refs) > 4:
        out_refs[4][...] = jnp.dot(xb, w_ref[:, 4 * MIX_WIDTH:], preferred_element_type=jnp.float32)


def _in_proj(xf, w):
    m = xf.shape[0]
    n = w.shape[1]
    has_gate = n > 4 * MIX_WIDTH
    out_shape = [jax.ShapeDtypeStruct((m, MIX_WIDTH), jnp.bfloat16)] * 4
    out_specs = [pl.BlockSpec((TM_PROJ, MIX_WIDTH), lambda i: (i, 0))] * 4
    if has_gate:
        out_shape = out_shape + [jax.ShapeDtypeStruct((m, LANES), jnp.float32)]
        out_specs = out_specs + [pl.BlockSpec((TM_PROJ, LANES), lambda i: (i, 0))]
    return pl.pallas_call(
        _in_proj_kernel,
        grid=(m // TM_PROJ,),
        in_specs=[pl.BlockSpec((TM_PROJ, D_MODEL), lambda i: (i, 0)),
                  pl.BlockSpec((D_MODEL, n), lambda i: (0, 0))],
        out_specs=out_specs,
        out_shape=out_shape,
        compiler_params=_params("arbitrary"),
        name="in_proj_gate" if has_gate else "in_proj",
    )(xf, w)


def _out_proj_ln_kernel(g_ref, x_ref, w_ref, gam_ref, bet_ref, o_ref):
    y = jnp.dot(g_ref[...], w_ref[...], preferred_element_type=jnp.float32)
    r = DEEPNORM_ALPHA * x_ref[...] + y
    mu = jnp.mean(r, axis=-1, keepdims=True)
    c = r - mu
    var = jnp.mean(c * c, axis=-1, keepdims=True)
    o_ref[...] = c * lax.rsqrt(var + LN_EPS) * gam_ref[...] + bet_ref[...]


def _out_proj_ln(g, xf, w, gamma, beta):
    m = xf.shape[0]
    return pl.pallas_call(
        _out_proj_ln_kernel,
        grid=(m // TM_PROJ,),
        in_specs=[pl.BlockSpec((TM_PROJ, MIX_WIDTH), lambda i: (i, 0)),
                  pl.BlockSpec((TM_PROJ, D_MODEL), lambda i: (i, 0)),
                  pl.BlockSpec((MIX_WIDTH, D_MODEL), lambda i: (0, 0)),
                  pl.BlockSpec((1, D_MODEL), lambda i: (0, 0)),
                  pl.BlockSpec((1, D_MODEL), lambda i: (0, 0))],
        out_specs=pl.BlockSpec((TM_PROJ, D_MODEL), lambda i: (i, 0)),
        out_shape=jax.ShapeDtypeStruct((m, D_MODEL), jnp.float32),
        compiler_params=_params("arbitrary"),
        name="out_proj_ln",
    )(g, xf, w, gamma.reshape(1, D_MODEL), beta.reshape(1, D_MODEL))


def _head_masked_queries(q2):
    lane = lax.broadcasted_iota(jnp.int32, q2.shape, 1)
    zero = jnp.zeros_like(q2)
    return [jnp.where((lane >= HEAD_DIM * h) & (lane < HEAD_DIM * (h + 1)), q2, zero)
            for h in range(HEADS_PER_TILE)]


def _online_update(state, a, v_blk):
    u, l, acc = state
    u_new = jnp.maximum(u, jnp.max(a, axis=-1, keepdims=True))
    p = jnp.exp2(a - u_new)
    alpha = jnp.exp2(u - u_new)
    l = alpha * l + jnp.sum(p, axis=-1, keepdims=True)
    acc = alpha * acc + jnp.dot(p.astype(jnp.bfloat16), v_blk, preferred_element_type=jnp.float32)
    return u_new, l, acc


def _init_state():
    return (jnp.full((TQ, 1), NEG, jnp.float32), jnp.zeros((TQ, 1), jnp.float32),
            jnp.zeros((TQ, LANES), jnp.float32))


def _finish(states, z_ref, o_ref):
    lane = lax.broadcasted_iota(jnp.int32, (TQ, LANES), 1)
    o = jnp.where(lane < HEAD_DIM, states[0][2] / states[0][1], states[1][2] / states[1][1])
    z = z_ref[...].astype(jnp.float32)
    o_ref[...] = (o * (z / (1.0 + jnp.exp(-z)))).astype(o_ref.dtype)


def _qk(qm, k_blk):
    return lax.dot_general(qm, k_blk, (((1,), (1,)), ((), ())), preferred_element_type=jnp.float32)


N_BAND_BLOCKS = N_PAST_CHUNKS * CHUNK // TKV + 1


def _mixer_a_kernel(q_ref, k_ref, v_ref, z_ref, bias_ref, o_ref):
    i = pl.program_id(2)
    qms = _head_masked_queries(q_ref[...])
    states = [_init_state() for _ in range(HEADS_PER_TILE)]
    for rel in range(N_BAND_BLOCKS):
        j = i - (N_BAND_BLOCKS - 1) + rel
        row0 = pl.multiple_of(jnp.maximum(j, 0) * TKV, TKV)
        k_blk = k_ref[pl.ds(row0, TKV), :]
        v_blk = v_ref[pl.ds(row0, TKV), :]
        off = jnp.where(j < 0, NEG, 0.0).astype(jnp.float32)
        for h in range(HEADS_PER_TILE):
            a = _qk(qms[h], k_blk) + (bias_ref[h, rel] + off)
            states[h] = _online_update(states[h], a, v_blk)
    _finish(states, z_ref, o_ref)


def _mixer_a_bias(rel_bias):
    r = jnp.arange(TQ)[:, None]
    c = jnp.arange(TKV)[None, :]
    tiles = []
    for rel in range(N_BAND_BLOCKS):
        blk_off = (N_BAND_BLOCKS - 1 - rel) * TKV
        dist = r - c + blk_off
        idx = jnp.clip(dist, -MAX_REL, MAX_REL) + MAX_REL
        dchunk = r // CHUNK - c // CHUNK + blk_off // CHUNK
        visible = (dchunk >= 0) & (dchunk <= N_PAST_CHUNKS)
        b = rel_bias.astype(jnp.float32)[:, idx] * LOG2E
        tiles.append(jnp.where(visible[None], b, NEG))
    return jnp.stack(tiles, axis=1)


def _mixer_a(q, k, v, z, bias, batch, seq):
    nq = seq // TQ
    qspec = pl.BlockSpec((TQ, LANES), lambda b, hp, i: (b * nq + i, hp))
    kvspec = pl.BlockSpec((seq, LANES), lambda b, hp, i: (b, hp))
    return pl.pallas_call(
        _mixer_a_kernel,
        grid=(batch, N_HEAD_PAIRS, nq),
        in_specs=[qspec, kvspec, kvspec, qspec,
                  pl.BlockSpec((HEADS_PER_TILE, N_BAND_BLOCKS, TQ, TKV), lambda b, hp, i: (hp, 0, 0, 0))],
        out_specs=qspec,
        out_shape=jax.ShapeDtypeStruct(q.shape, jnp.bfloat16),
        compiler_params=_params("arbitrary", "arbitrary", "arbitrary"),
        name="mixer_a",
    )(q, k, v, z, bias)


def _fox_prep_kernel(lf_ref, bf_ref, tri_ref, f_ref, carry_ref):
    @pl.when(pl.program_id(1) == 0)
    def _():
        carry_ref[...] = jnp.zeros_like(carry_ref)

    pre = lf_ref[...] + bf_ref[...]
    logf = (jnp.minimum(pre, 0.0) - jnp.log(1.0 + jnp.exp(-jnp.abs(pre)))) * LOG2E
    hi = logf.astype(jnp.bfloat16)
    r1 = logf - hi.astype(jnp.float32)
    mid = r1.astype(jnp.bfloat16)
    lo = (r1 - mid.astype(jnp.float32)).astype(jnp.bfloat16)
    tri = tri_ref[...]
    csum = (jnp.dot(tri, hi, preferred_element_type=jnp.float32)
            + jnp.dot(tri, mid, preferred_element_type=jnp.float32)
            + jnp.dot(tri, lo, preferred_element_type=jnp.float32))
    f = csum + carry_ref[...]
    carry_ref[...] = f[TB_PREP - 1:TB_PREP, :]
    f_ref[...] = f.T[:N_HEADS, :]


def _fox_prep(lf, b_f, batch, seq):
    nb = seq // TB_PREP
    bf = jnp.zeros((1, LANES), jnp.float32).at[0, :N_HEADS].set(b_f.astype(jnp.float32))
    tri = (jnp.arange(TB_PREP)[:, None] >= jnp.arange(TB_PREP)[None, :]).astype(jnp.bfloat16)
    return pl.pallas_call(
        _fox_prep_kernel,
        grid=(batch, nb),
        in_specs=[pl.BlockSpec((TB_PREP, LANES), lambda b, i: (b * nb + i, 0)),
                  pl.BlockSpec((1, LANES), lambda b, i: (0, 0)),
                  pl.BlockSpec((TB_PREP, TB_PREP), lambda b, i: (0, 0))],
        out_specs=pl.BlockSpec((None, N_HEADS, TB_PREP), lambda b, i: (b, 0, i)),
        out_shape=jax.ShapeDtypeStruct((batch, N_HEADS, seq), jnp.float32),
        scratch_shapes=[pltpu.VMEM((1, LANES), jnp.float32)],
        compiler_params=_params("arbitrary", "arbitrary"),
        name="fox_prep",
    )(lf, bf, tri)


def _fox_kernel(q_ref, k_ref, v_ref, z_ref, f_ref, o_ref):
    i = pl.program_id(2)
    qms = _head_masked_queries(q_ref[...])

    def tile(j, states, masked):
        row0 = pl.multiple_of(j * TKV, TKV)
        k_blk = k_ref[pl.ds(row0, TKV), :]
        v_blk = v_ref[pl.ds(row0, TKV), :]
        new = []
        for h in range(HEADS_PER_TILE):
            frow = f_ref[pl.ds(h, 1), pl.ds(row0, TKV)]
            f0 = frow[:, 0:1]
            a = _qk(qms[h], k_blk) + (f0 - frow)
            if masked:
                r = lax.broadcasted_iota(jnp.int32, a.shape, 0)
                c = lax.broadcasted_iota(jnp.int32, a.shape, 1)
                a = jnp.where(c <= r, a, NEG)
            u, l, acc = states[h]
            u_new, l, acc = _online_update((u + f0, l, acc), a, v_blk)
            new.append((u_new - f0, l, acc))
        return new

    def body(j, flat):
        states = [tuple(flat[3 * h:3 * h + 3]) for h in range(HEADS_PER_TILE)]
        states = tile(j, states, masked=False)
        return tuple(x for s in states for x in s)

    flat = tuple(x for _ in range(HEADS_PER_TILE) for x in _init_state())
    flat = lax.fori_loop(0, i, body, flat)
    states = [tuple(flat[3 * h:3 * h + 3]) for h in range(HEADS_PER_TILE)]
    states = tile(i, states, masked=True)
    _finish(states, z_ref, o_ref)


def _fox(q, k, v, z, f, batch, seq):
    nq = seq // TQ
    qspec = pl.BlockSpec((TQ, LANES), lambda b, hp, i: (b * nq + i, hp))
    kvspec = pl.BlockSpec((seq, LANES), lambda b, hp, i: (b, hp))
    f4 = f.reshape(batch, N_HEAD_PAIRS, HEADS_PER_TILE, seq)
    return pl.pallas_call(
        _fox_kernel,
        grid=(batch, N_HEAD_PAIRS, nq),
        in_specs=[qspec, kvspec, kvspec, qspec,
                  pl.BlockSpec((None, None, HEADS_PER_TILE, seq), lambda b, hp, i: (b, hp, 0, 0))],
        out_specs=qspec,
        out_shape=jax.ShapeDtypeStruct(q.shape, jnp.bfloat16),
        compiler_params=_params("arbitrary", "arbitrary", "arbitrary"),
        name="fox",
    )(q, k, v, z, f4)


def _scaled_in_weights(w_in):
    qscale = LOG2E / math.sqrt(HEAD_DIM)
    col = jnp.concatenate([jnp.full((MIX_WIDTH,), qscale, jnp.float32),
                           jnp.ones((3 * MIX_WIDTH,), jnp.float32)])
    return w_in.astype(jnp.float32) * col


def kernel(x, w_in_a, rel_bias_a, w_out_a, w_in_b, w_f_b, b_f_b, w_out_b, ln_g, ln_b):
    batch, seq, d = x.shape
    xf = x.reshape(batch * seq, d).astype(jnp.float32)
    for i in range(DEPTH):
        j = i // N_MIXERS
        if i % N_MIXERS == 0:
            w = _scaled_in_weights(w_in_a[j]).astype(jnp.bfloat16)
            q, k, v, z = _in_proj(xf, w)
            g = _mixer_a(q, k, v, z, _mixer_a_bias(rel_bias_a[j]), batch, seq)
            w_out = w_out_a[j]
        else:
            wf = jnp.zeros((d, LANES), jnp.float32).at[:, :N_HEADS].set(w_f_b[j].astype(jnp.float32))
            w = jnp.concatenate([_scaled_in_weights(w_in_b[j]), wf], axis=1).astype(jnp.bfloat16)
            q, k, v, z, lf = _in_proj(xf, w)
            f = _fox_prep(lf, b_f_b[j], batch, seq)
            g = _fox(q, k, v, z, f, batch, seq)
            w_out = w_out_b[j]
        xf = _out_proj_ln(g, xf, w_out.astype(jnp.bfloat16), ln_g[i], ln_b[i])
    return xf.reshape(batch, seq, d).astype(x.dtype)
```

```python
import math

import jax
import jax.numpy as jnp
from jax import lax
from jax.experimental import pallas as pl
from jax.experimental.pallas import tpu as pltpu

D_MODEL = 1024
N_HEADS = 16
HEAD_DIM = 64
MIX_WIDTH = N_HEADS * HEAD_DIM
CHUNK = 64
N_PAST_CHUNKS = 8
MAX_REL = 128
DEPTH = 4
N_MIXERS = 2
LN_EPS = 1e-5
DEEPNORM_ALPHA = (2.0 * DEPTH) ** 0.25
LOG2E = math.log2(math.e)
NEG = -1e30

LANES = 128
HEADS_PER_TILE = LANES // HEAD_DIM
N_HEAD_PAIRS = N_HEADS // HEADS_PER_TILE
BF16_ROWS = 16
N_SPLIT = 3
TQ = 256
TKV = 1024
TM_PROJ = 512
TB_PREP = 512
VMEM_LIMIT = 56 * 1024 * 1024


def _params(*semantics):
    return pltpu.CompilerParams(dimension_semantics=semantics, vmem_limit_bytes=VMEM_LIMIT)


def _dot_nt(a, b):
    return lax.dot_general(a, b, (((1,), (1,)), ((), ())), preferred_element_type=jnp.float32)


def _in_proj_kernel(x_ref, w_ref, wvt_ref, q_ref, k_ref, vt_ref, z_ref, *gate_ref):
    xb = x_ref[...].astype(jnp.bfloat16)
    for g, o_ref in enumerate((q_ref, k_ref, z_ref)):
        o_ref[...] = jnp.dot(xb, w_ref[:, g * MIX_WIDTH:(g + 1) * MIX_WIDTH],
                             preferred_element_type=jnp.float32).astype(jnp.bfloat16)
    vt_ref[...] = _dot_nt(wvt_ref[...], xb).astype(jnp.bfloat16)
    if gate_ref:
        gate_ref[0][...] = jnp.dot(xb, w_ref[:, 3 * MIX_WIDTH:], preferred_element_type=jnp.float32)


def _in_proj(xf, w, wvt):
    m = xf.shape[0]
    n = w.shape[1]
    has_gate = n > 3 * MIX_WIDTH
    row = pl.BlockSpec((TM_PROJ, MIX_WIDTH), lambda i: (i, 0))
    col = pl.BlockSpec((MIX_WIDTH, TM_PROJ), lambda i: (0, i))
    tok = jax.ShapeDtypeStruct((m, MIX_WIDTH), jnp.bfloat16)
    out_shape = [tok, tok, jax.ShapeDtypeStruct((MIX_WIDTH, m), jnp.bfloat16), tok]
    out_specs = [row, row, col, row]
    if has_gate:
        out_shape = out_shape + [jax.ShapeDtypeStruct((m, LANES), jnp.float32)]
        out_specs = out_specs + [pl.BlockSpec((TM_PROJ, LANES), lambda i: (i, 0))]
    return pl.pallas_call(
        _in_proj_kernel,
        grid=(m // TM_PROJ,),
        in_specs=[pl.BlockSpec((TM_PROJ, D_MODEL), lambda i: (i, 0)),
                  pl.BlockSpec((D_MODEL, n), lambda i: (0, 0)),
                  pl.BlockSpec((MIX_WIDTH, D_MODEL), lambda i: (0, 0))],
        out_specs=out_specs,
        out_shape=out_shape,
        compiler_params=_params("arbitrary"),
        name="in_proj_gate" if has_gate else "in_proj",
    )(xf, w, wvt)


def _out_proj_ln_kernel(g_ref, x_ref, w_ref, gam_ref, bet_ref, o_ref):
    y = jnp.dot(g_ref[...], w_ref[...], preferred_element_type=jnp.float32)
    r = DEEPNORM_ALPHA * x_ref[...] + y
    mu = jnp.mean(r, axis=-1, keepdims=True)
    c = r - mu
    var = jnp.mean(c * c, axis=-1, keepdims=True)
    o_ref[...] = c * lax.rsqrt(var + LN_EPS) * gam_ref[...] + bet_ref[...]


def _out_proj_ln(g, xf, w, gamma, beta):
    m = xf.shape[0]
    return pl.pallas_call(
        _out_proj_ln_kernel,
        grid=(m // TM_PROJ,),
        in_specs=[pl.BlockSpec((TM_PROJ, MIX_WIDTH), lambda i: (i, 0)),
                  pl.BlockSpec((TM_PROJ, D_MODEL), lambda i: (i, 0)),
                  pl.BlockSpec((MIX_WIDTH, D_MODEL), lambda i: (0, 0)),
                  pl.BlockSpec((1, D_MODEL), lambda i: (0, 0)),
                  pl.BlockSpec((1, D_MODEL), lambda i: (0, 0))],
        out_specs=pl.BlockSpec((TM_PROJ, D_MODEL), lambda i: (i, 0)),
        out_shape=jax.ShapeDtypeStruct((m, D_MODEL), jnp.float32),
        compiler_params=_params("arbitrary"),
        name="out_proj_ln",
    )(g, xf, w, gamma.reshape(1, D_MODEL), beta.reshape(1, D_MODEL))


ACC_ROWS = HEAD_DIM + BF16_ROWS


def _head_masked_queries(q2):
    lane = lax.broadcasted_iota(jnp.int32, q2.shape, 1)
    zero = jnp.zeros_like(q2)
    return [jnp.where((lane >= HEAD_DIM * h) & (lane < HEAD_DIM * (h + 1)), q2, zero)
            for h in range(HEADS_PER_TILE)]


def _head_values(vt_blk, h):
    ones = jnp.ones((BF16_ROWS, vt_blk.shape[1]), vt_blk.dtype)
    return jnp.concatenate([vt_blk[HEAD_DIM * h:HEAD_DIM * (h + 1), :], ones], axis=0)


def _online_update(state, a, vt_h):
    u, acc = state
    u_new = jnp.maximum(u, jnp.max(a, axis=0, keepdims=True))
    p = jnp.exp2(a - u_new).astype(jnp.bfloat16)
    alpha = jnp.exp2(u - u_new)
    acc = alpha * acc + jnp.dot(vt_h, p, preferred_element_type=jnp.float32)
    return u_new, acc


def _init_state():
    return (jnp.full((1, TQ), NEG, jnp.float32), jnp.zeros((ACC_ROWS, TQ), jnp.float32))


def _finish(states, z_ref, o_ref):
    ot = jnp.concatenate([acc[:HEAD_DIM] / acc[HEAD_DIM:HEAD_DIM + 1] for _, acc in states], axis=0)
    z = z_ref[...].astype(jnp.float32)
    o_ref[...] = (ot.T * (z / (1.0 + jnp.exp(-z)))).astype(o_ref.dtype)


A_PAST = N_PAST_CHUNKS * CHUNK
A_WINDOW = A_PAST + TQ
A_VARIANTS = A_PAST // TQ + 1


def _mixer_a_kernel(q_ref, k_ref, vt_ref, z_ref, bias_ref, o_ref):
    i = pl.program_id(2)
    row0 = pl.multiple_of(jnp.maximum(i * TQ - A_PAST, 0), TQ)
    k_win = k_ref[pl.ds(row0, A_WINDOW), :]
    vt_win = vt_ref[:, pl.ds(row0, A_WINDOW)]
    qms = _head_masked_queries(q_ref[...])
    logits = [_dot_nt(k_win, qms[h]) + bias_ref[h] for h in range(HEADS_PER_TILE)]
    states = [_online_update(_init_state(), logits[h], _head_values(vt_win, h))
              for h in range(HEADS_PER_TILE)]
    _finish(states, z_ref, o_ref)


def _mixer_a_bias(rel_bias):
    c = jnp.arange(A_WINDOW)[:, None]
    r = jnp.arange(TQ)[None, :]
    tiles = []
    for v in range(A_VARIANTS):
        kpos = c - (A_PAST - v * TQ)
        idx = jnp.clip(r - kpos, -MAX_REL, MAX_REL) + MAX_REL
        dchunk = r // CHUNK - kpos // CHUNK
        visible = (dchunk >= 0) & (dchunk <= N_PAST_CHUNKS)
        b = rel_bias.astype(jnp.float32)[:, idx] * LOG2E
        tiles.append(jnp.where(visible[None], b, NEG))
    return jnp.stack(tiles, axis=1)


def _attn_specs(seq):
    nq = seq // TQ
    tok = pl.BlockSpec((TQ, LANES), lambda b, hp, i: (b * nq + i, hp))
    seq_rows = pl.BlockSpec((seq, LANES), lambda b, hp, i: (b, hp))
    seq_cols = pl.BlockSpec((LANES, seq), lambda b, hp, i: (hp, b))
    return nq, tok, seq_rows, seq_cols


def _mixer_a(q, k, vt, z, bias, batch, seq):
    nq, tok, seq_rows, seq_cols = _attn_specs(seq)
    return pl.pallas_call(
        _mixer_a_kernel,
        grid=(batch, N_HEAD_PAIRS, nq),
        in_specs=[tok, seq_rows, seq_cols, tok,
                  pl.BlockSpec((HEADS_PER_TILE, None, A_WINDOW, TQ),
                               lambda b, hp, i: (hp, jnp.maximum(A_VARIANTS - 1 - i, 0), 0, 0))],
        out_specs=tok,
        out_shape=jax.ShapeDtypeStruct(q.shape, jnp.bfloat16),
        compiler_params=_params("arbitrary", "arbitrary", "arbitrary"),
        name="mixer_a",
    )(q, k, vt, z, bias)


def _split3(x):
    hi = x.astype(jnp.bfloat16)
    r1 = x - hi.astype(jnp.float32)
    mid = r1.astype(jnp.bfloat16)
    lo = (r1 - mid.astype(jnp.float32)).astype(jnp.bfloat16)
    return hi, mid, lo


def _fox_prep_kernel(lf_ref, bf_ref, tri_ref, sel_ref, g_ref, carry_ref):
    @pl.when(pl.program_id(1) == 0)
    def _():
        carry_ref[...] = jnp.zeros_like(carry_ref)

    pre = lf_ref[...] + bf_ref[...]
    logf = (jnp.minimum(pre, 0.0) - jnp.log(1.0 + jnp.exp(-jnp.abs(pre)))) * LOG2E
    tri = tri_ref[...]
    csum = sum(jnp.dot(tri, piece, preferred_element_type=jnp.float32) for piece in _split3(logf))
    f = csum + carry_ref[...]
    carry_ref[...] = f[TB_PREP - 1:TB_PREP, :]
    g_ref[...] = sum(jnp.dot(piece, sel_ref[n], preferred_element_type=jnp.float32)
                     for n, piece in enumerate(_split3(-f))).astype(jnp.bfloat16)


def _fox_prep(lf, b_f, batch, seq):
    nb = seq // TB_PREP
    bf = jnp.zeros((1, LANES), jnp.float32).at[0, :N_HEADS].set(b_f.astype(jnp.float32))
    tri = (jnp.arange(TB_PREP)[:, None] >= jnp.arange(TB_PREP)[None, :]).astype(jnp.bfloat16)
    head = jnp.arange(N_HEADS)
    dst = (head // HEADS_PER_TILE) * LANES + (head % HEADS_PER_TILE) * N_SPLIT
    sel = jnp.stack([jnp.zeros((LANES, N_HEAD_PAIRS * LANES), jnp.bfloat16).at[head, dst + n].set(1.0)
                     for n in range(N_SPLIT)])
    return pl.pallas_call(
        _fox_prep_kernel,
        grid=(batch, nb),
        in_specs=[pl.BlockSpec((TB_PREP, LANES), lambda b, i: (b * nb + i, 0)),
                  pl.BlockSpec((1, LANES), lambda b, i: (0, 0)),
                  pl.BlockSpec((TB_PREP, TB_PREP), lambda b, i: (0, 0)),
                  pl.BlockSpec((N_SPLIT, LANES, N_HEAD_PAIRS * LANES), lambda b, i: (0, 0, 0))],
        out_specs=pl.BlockSpec((TB_PREP, N_HEAD_PAIRS * LANES), lambda b, i: (b * nb + i, 0)),
        out_shape=jax.ShapeDtypeStruct((batch * seq, N_HEAD_PAIRS * LANES), jnp.bfloat16),
        scratch_shapes=[pltpu.VMEM((1, LANES), jnp.float32)],
        compiler_params=_params("arbitrary", "arbitrary"),
        name="fox_prep",
    )(lf, bf, tri, sel)


def _fox_kernel(q_ref, k_ref, g_ref, vt_ref, z_ref, o_ref):
    i = pl.program_id(2)
    lane = lax.broadcasted_iota(jnp.int32, (TQ, LANES), 1)
    qaug = []
    for h, qm in enumerate(_head_masked_queries(q_ref[...])):
        pick = ((lane >= N_SPLIT * h) & (lane < N_SPLIT * (h + 1))).astype(jnp.bfloat16)
        qaug.append(jnp.concatenate([qm, pick], axis=1))

    def tile(j, states, masked):
        row0 = pl.multiple_of(j * TKV, TKV)
        kaug = jnp.concatenate([k_ref[pl.ds(row0, TKV), :], g_ref[pl.ds(row0, TKV), :]], axis=1)
        vt_blk = vt_ref[:, pl.ds(row0, TKV)]
        logits = [_dot_nt(kaug, qaug[h]) for h in range(HEADS_PER_TILE)]
        if masked:
            s = lax.broadcasted_iota(jnp.int32, (TKV, TQ), 0) + (j * TKV - i * TQ)
            t = lax.broadcasted_iota(jnp.int32, (TKV, TQ), 1)
            logits = [jnp.where(s <= t, a, NEG) for a in logits]
        return [_online_update(states[h], logits[h], _head_values(vt_blk, h))
                for h in range(HEADS_PER_TILE)]

    def body(j, flat):
        states = tile(j, [flat[0:2], flat[2:4]], masked=False)
        return tuple(x for s in states for x in s)

    j_diag = (i * TQ) // TKV
    flat = lax.fori_loop(0, j_diag, body, tuple(x for _ in range(HEADS_PER_TILE) for x in _init_state()))
    states = tile(j_diag, [flat[0:2], flat[2:4]], masked=True)
    _finish(states, z_ref, o_ref)


def _fox(q, k, gaug, vt, z, batch, seq):
    nq, tok, seq_rows, seq_cols = _attn_specs(seq)
    return pl.pallas_call(
        _fox_kernel,
        grid=(batch, N_HEAD_PAIRS, nq),
        in_specs=[tok, seq_rows, seq_rows, seq_cols, tok],
        out_specs=tok,
        out_shape=jax.ShapeDtypeStruct(q.shape, jnp.bfloat16),
        compiler_params=_params("arbitrary", "arbitrary", "arbitrary"),
        name="fox",
    )(q, k, gaug, vt, z)


def _in_weights(w_in, w_gate=None):
    w_in = w_in.astype(jnp.float32)
    e = MIX_WIDTH
    cols = [w_in[:, :e] * (LOG2E / math.sqrt(HEAD_DIM)), w_in[:, e:2 * e], w_in[:, 3 * e:]]
    if w_gate is not None:
        cols.append(jnp.zeros((w_in.shape[0], LANES), jnp.float32).at[:, :N_HEADS].set(w_gate.astype(jnp.float32)))
    return jnp.concatenate(cols, axis=1).astype(jnp.bfloat16), w_in[:, 2 * e:3 * e].T.astype(jnp.bfloat16)


def kernel(x, w_in_a, rel_bias_a, w_out_a, w_in_b, w_f_b, b_f_b, w_out_b, ln_g, ln_b):
    batch, seq, d = x.shape
    xf = x.reshape(batch * seq, d).astype(jnp.float32)
    for i in range(DEPTH):
        j = i // N_MIXERS
        if i % N_MIXERS == 0:
            q, k, vt, z = _in_proj(xf, *_in_weights(w_in_a[j]))
            g = _mixer_a(q, k, vt, z, _mixer_a_bias(rel_bias_a[j]), batch, seq)
            w_out = w_out_a[j]
        else:
            q, k, vt, z, lf = _in_proj(xf, *_in_weights(w_in_b[j], w_f_b[j]))
            gaug = _fox_prep(lf, b_f_b[j], batch, seq)
            g = _fox(q, k, gaug, vt, z, batch, seq)
            w_out = w_out_b[j]
        xf = _out_proj_ln(g, xf, w_out.astype(jnp.bfloat16), ln_g[i], ln_b[i])
    return xf.reshape(batch, seq, d).astype(x.dtype)
```

```python
import math

import jax
import jax.numpy as jnp
from jax import lax
from jax.experimental import pallas as pl
from jax.experimental.pallas import tpu as pltpu

D_MODEL = 1024
N_HEADS = 16
HEAD_DIM = 64
MIX_WIDTH = N_HEADS * HEAD_DIM
CHUNK = 64
N_PAST_CHUNKS = 8
MAX_REL = 128
DEPTH = 4
N_MIXERS = 2
LN_EPS = 1e-5
DEEPNORM_ALPHA = (2.0 * DEPTH) ** 0.25
LOG2E = math.log2(math.e)
NEG = -1e30

LANES = 128
HEADS_PER_TILE = LANES // HEAD_DIM
N_HEAD_PAIRS = N_HEADS // HEADS_PER_TILE
BF16_ROWS = 16
N_SPLIT = 3
TQ = 256
Q_SUB = 4
TG = TQ * Q_SUB
TKV = 1024
TM_PROJ = 512
VMEM_LIMIT = 56 * 1024 * 1024
PRUNE_LOG2 = 160.0
NORM_SLACK = 1.0 + 2.0 ** -6


def _params(*semantics):
    return pltpu.CompilerParams(dimension_semantics=semantics, vmem_limit_bytes=VMEM_LIMIT)


def _dot_nt(a, b):
    return lax.dot_general(a, b, (((1,), (1,)), ((), ())), preferred_element_type=jnp.float32)


def _split(x, n):
    pieces = []
    for _ in range(n):
        p = x.astype(jnp.bfloat16)
        pieces.append(p)
        x = x - p.astype(jnp.float32)
    return pieces


def _in_proj_kernel(x_ref, w_ref, wvt_ref, q_ref, k_ref, vt_ref, z_ref, *gate_ref):
    xb = x_ref[...].astype(jnp.bfloat16)
    for g, o_ref in enumerate((q_ref, k_ref, z_ref)):
        o_ref[...] = jnp.dot(xb, w_ref[:, g * MIX_WIDTH:(g + 1) * MIX_WIDTH],
                             preferred_element_type=jnp.float32).astype(jnp.bfloat16)
    vt_ref[...] = _dot_nt(wvt_ref[...], xb).astype(jnp.bfloat16)
    if gate_ref:
        gate_ref[0][...] = jnp.dot(xb, w_ref[:, 3 * MIX_WIDTH:], preferred_element_type=jnp.float32)


def _in_proj(xf, w, wvt):
    m = xf.shape[0]
    n = w.shape[1]
    has_gate = n > 3 * MIX_WIDTH
    row = pl.BlockSpec((TM_PROJ, MIX_WIDTH), lambda i: (i, 0))
    col = pl.BlockSpec((MIX_WIDTH, TM_PROJ), lambda i: (0, i))
    tok = jax.ShapeDtypeStruct((m, MIX_WIDTH), jnp.bfloat16)
    out_shape = [tok, tok, jax.ShapeDtypeStruct((MIX_WIDTH, m), jnp.bfloat16), tok]
    out_specs = [row, row, col, row]
    if has_gate:
        out_shape = out_shape + [jax.ShapeDtypeStruct((m, LANES), jnp.float32)]
        out_specs = out_specs + [pl.BlockSpec((TM_PROJ, LANES), lambda i: (i, 0))]
    return pl.pallas_call(
        _in_proj_kernel,
        grid=(m // TM_PROJ,),
        in_specs=[pl.BlockSpec((TM_PROJ, D_MODEL), lambda i: (i, 0)),
                  pl.BlockSpec((D_MODEL, n), lambda i: (0, 0)),
                  pl.BlockSpec((MIX_WIDTH, D_MODEL), lambda i: (0, 0))],
        out_specs=out_specs,
        out_shape=out_shape,
        compiler_params=_params("arbitrary"),
        name="in_proj_gate" if has_gate else "in_proj",
    )(xf, w, wvt)


def _out_proj_ln_kernel(g_ref, x_ref, w_ref, gam_ref, bet_ref, o_ref):
    y = jnp.dot(g_ref[...], w_ref[...], preferred_element_type=jnp.float32)
    r = DEEPNORM_ALPHA * x_ref[...] + y
    mu = jnp.mean(r, axis=-1, keepdims=True)
    c = r - mu
    var = jnp.mean(c * c, axis=-1, keepdims=True)
    o_ref[...] = c * lax.rsqrt(var + LN_EPS) * gam_ref[...] + bet_ref[...]


def _out_proj_ln(g, xf, w, gamma, beta):
    m = xf.shape[0]
    return pl.pallas_call(
        _out_proj_ln_kernel,
        grid=(m // TM_PROJ,),
        in_specs=[pl.BlockSpec((TM_PROJ, MIX_WIDTH), lambda i: (i, 0)),
                  pl.BlockSpec((TM_PROJ, D_MODEL), lambda i: (i, 0)),
                  pl.BlockSpec((MIX_WIDTH, D_MODEL), lambda i: (0, 0)),
                  pl.BlockSpec((1, D_MODEL), lambda i: (0, 0)),
                  pl.BlockSpec((1, D_MODEL), lambda i: (0, 0))],
        out_specs=pl.BlockSpec((TM_PROJ, D_MODEL), lambda i: (i, 0)),
        out_shape=jax.ShapeDtypeStruct((m, D_MODEL), jnp.float32),
        compiler_params=_params("arbitrary"),
        name="out_proj_ln",
    )(g, xf, w, gamma.reshape(1, D_MODEL), beta.reshape(1, D_MODEL))


ACC_ROWS = HEAD_DIM + BF16_ROWS


def _head_masked_queries(q2):
    lane = lax.broadcasted_iota(jnp.int32, q2.shape, 1)
    zero = jnp.zeros_like(q2)
    return [jnp.where((lane >= HEAD_DIM * h) & (lane < HEAD_DIM * (h + 1)), q2, zero)
            for h in range(HEADS_PER_TILE)]


def _head_values(vt_blk, h):
    ones = jnp.ones((BF16_ROWS, vt_blk.shape[1]), vt_blk.dtype)
    return jnp.concatenate([vt_blk[HEAD_DIM * h:HEAD_DIM * (h + 1), :], ones], axis=0)


def _online_update(state, a, vt_h):
    u, acc = state
    u_new = jnp.maximum(u, jnp.max(a, axis=0, keepdims=True))
    p = jnp.exp2(a - u_new).astype(jnp.bfloat16)
    alpha = jnp.exp2(u - u_new)
    acc = alpha * acc + jnp.dot(vt_h, p, preferred_element_type=jnp.float32)
    return u_new, acc


def _init_state():
    return (jnp.full((1, TQ), NEG, jnp.float32), jnp.zeros((ACC_ROWS, TQ), jnp.float32))


def _finish(states, z_ref, o_ref, r):
    ot = jnp.concatenate([acc[:HEAD_DIM] / acc[HEAD_DIM:HEAD_DIM + 1] for _, acc in states], axis=0)
    rows = pl.ds(r * TQ, TQ)
    z = z_ref[rows, :].astype(jnp.float32)
    o_ref[rows, :] = (ot.T * (z / (1.0 + jnp.exp(-z)))).astype(o_ref.dtype)


def _attn_specs(seq):
    ng = seq // TG
    tok = pl.BlockSpec((TG, LANES), lambda b, hp, i, *_: (b * ng + i, hp))
    seq_rows = pl.BlockSpec((seq, LANES), lambda b, hp, i, *_: (b, hp))
    seq_cols = pl.BlockSpec((LANES, seq), lambda b, hp, i, *_: (hp, b))
    return ng, tok, seq_rows, seq_cols


A_PAST = N_PAST_CHUNKS * CHUNK
A_WINDOW = A_PAST + TQ
A_BIAS_ROWS = 2 * A_PAST + TQ


def _mixer_a_kernel(q_ref, k_ref, vt_ref, z_ref, bias_ref, o_ref):
    gi = pl.program_id(2)
    chains = []
    for r in range(Q_SUB):
        start = (gi * Q_SUB + r) * TQ - A_PAST
        row0 = pl.multiple_of(jnp.maximum(start, 0), TQ)
        brow = pl.multiple_of(row0 - start, TQ)
        k_win = k_ref[pl.ds(row0, A_WINDOW), :]
        qms = _head_masked_queries(q_ref[pl.ds(r * TQ, TQ), :])
        logits = [_dot_nt(k_win, qms[h]) + bias_ref[h, pl.ds(brow, A_WINDOW), :]
                  for h in range(HEADS_PER_TILE)]
        chains.append((row0, logits))
    for r, (row0, logits) in enumerate(chains):
        vt_win = vt_ref[:, pl.ds(row0, A_WINDOW)]
        states = [_online_update(_init_state(), logits[h], _head_values(vt_win, h))
                  for h in range(HEADS_PER_TILE)]
        _finish(states, z_ref, o_ref, r)


def _mixer_a_bias(rel_bias):
    n = A_BIAS_ROWS
    L = n + TQ + 1
    m = jnp.arange(L)
    d = jnp.where(m < L - n, m, m - L) + A_PAST
    strip = rel_bias.astype(jnp.float32)[:, jnp.clip(d, -MAX_REL, MAX_REL) + MAX_REL] * LOG2E
    skew = jnp.broadcast_to(strip[:, None, :], (N_HEADS, n, L)).reshape(N_HEADS, n * L)
    toep = skew[:, :n * (L - 1)].reshape(N_HEADS, n, L - 1)[:, :, :TQ]
    c = jnp.arange(n)[:, None]
    r = jnp.arange(TQ)[None, :]
    dchunk = r // CHUNK - (c - A_PAST) // CHUNK
    visible = (dchunk >= 0) & (dchunk <= N_PAST_CHUNKS)
    return jnp.where(visible[None], toep, NEG)


def _mixer_a(q, k, vt, z, bias, batch, seq):
    ng, tok, seq_rows, seq_cols = _attn_specs(seq)
    return pl.pallas_call(
        _mixer_a_kernel,
        grid=(batch, N_HEAD_PAIRS, ng),
        in_specs=[tok, seq_rows, seq_cols, tok,
                  pl.BlockSpec((HEADS_PER_TILE, A_BIAS_ROWS, TQ), lambda b, hp, i: (hp, 0, 0))],
        out_specs=tok,
        out_shape=jax.ShapeDtypeStruct(q.shape, jnp.bfloat16),
        compiler_params=_params("arbitrary", "arbitrary", "arbitrary"),
        name="mixer_a",
    )(q, k, vt, z, bias)


N_STATS = 8


def _fox_prep_kernel(lf_ref, bf_ref, q_ref, k_ref, tri_ref, sel_ref, hsel_ref, g_ref, st_ref, carry_ref):
    @pl.when(pl.program_id(1) == 0)
    def _():
        carry_ref[...] = jnp.zeros_like(carry_ref)

    pre = lf_ref[...] + bf_ref[...]
    logf = (jnp.minimum(pre, 0.0) - jnp.log(1.0 + jnp.exp(-jnp.abs(pre)))) * LOG2E
    tri = tri_ref[...]
    csum = sum(jnp.dot(tri, piece, preferred_element_type=jnp.float32) for piece in _split(logf, N_SPLIT))
    f = csum + carry_ref[...]
    carry_ref[...] = f[TKV - 1:TKV, :]
    g_ref[...] = sum(jnp.dot(piece, sel_ref[n], preferred_element_type=jnp.float32)
                     for n, piece in enumerate(_split(-f, N_SPLIT))).astype(jnp.bfloat16)

    def max_norm(x_ref):
        x = x_ref[...].astype(jnp.float32)
        sq = sum(jnp.dot(piece, hsel_ref[...], preferred_element_type=jnp.float32)
                 for piece in _split(x * x, 2))
        return jnp.sqrt(jnp.max(sq, axis=0, keepdims=True))

    st_ref[...] = jnp.concatenate(
        [max_norm(q_ref), max_norm(k_ref), f[0:1, :], f[TKV - 1:TKV, :],
         jnp.zeros((N_STATS - 4, LANES), jnp.float32)], axis=0)


def _fox_prep(lf, b_f, q, k, batch, seq):
    nb = seq // TKV
    bf = jnp.zeros((1, LANES), jnp.float32).at[0, :N_HEADS].set(b_f.astype(jnp.float32))
    tri = (jnp.arange(TKV)[:, None] >= jnp.arange(TKV)[None, :]).astype(jnp.bfloat16)
    head = jnp.arange(N_HEADS)
    dst = (head // HEADS_PER_TILE) * LANES + (head % HEADS_PER_TILE) * N_SPLIT
    sel = jnp.stack([jnp.zeros((LANES, N_HEAD_PAIRS * LANES), jnp.bfloat16).at[head, dst + n].set(1.0)
                     for n in range(N_SPLIT)])
    hsel = (jnp.arange(MIX_WIDTH)[:, None] // HEAD_DIM == jnp.arange(LANES)[None, :]).astype(jnp.bfloat16)
    tokspec = pl.BlockSpec((TKV, MIX_WIDTH), lambda b, i: (b * nb + i, 0))
    return pl.pallas_call(
        _fox_prep_kernel,
        grid=(batch, nb),
        in_specs=[pl.BlockSpec((TKV, LANES), lambda b, i: (b * nb + i, 0)),
                  pl.BlockSpec((1, LANES), lambda b, i: (0, 0)),
                  tokspec, tokspec,
                  pl.BlockSpec((TKV, TKV), lambda b, i: (0, 0)),
                  pl.BlockSpec((N_SPLIT, LANES, N_HEAD_PAIRS * LANES), lambda b, i: (0, 0, 0)),
                  pl.BlockSpec((MIX_WIDTH, LANES), lambda b, i: (0, 0))],
        out_specs=[tokspec, pl.BlockSpec((None, None, N_STATS, LANES), lambda b, i: (b, i, 0, 0))],
        out_shape=[jax.ShapeDtypeStruct((batch * seq, N_HEAD_PAIRS * LANES), jnp.bfloat16),
                   jax.ShapeDtypeStruct((batch, nb, N_STATS, LANES), jnp.float32)],
        scratch_shapes=[pltpu.VMEM((1, LANES), jnp.float32)],
        compiler_params=_params("arbitrary", "arbitrary"),
        name="fox_prep",
    )(lf, bf, q, k, tri, sel, hsel)


def _first_needed_step(stats):
    qn = stats[:, :, 0, :N_HEADS] * NORM_SLACK
    kn = stats[:, :, 1, :N_HEADS] * NORM_SLACK
    f_first, f_last = stats[:, :, 2, :N_HEADS], stats[:, :, 3, :N_HEADS]
    bound = (qn[:, :, None] * kn[:, None, :] + f_first[:, :, None] - f_last[:, None, :]
             + (qn * kn)[:, :, None])
    nb = stats.shape[1]
    below = jnp.arange(nb)[None, :, None, None] > jnp.arange(nb)[None, None, :, None]
    skip = (bound < -PRUNE_LOG2) & below
    first = jnp.sum(jnp.cumprod(skip.astype(jnp.int32), axis=2), axis=2)
    first = jnp.min(first.reshape(first.shape[0], nb, N_HEAD_PAIRS, HEADS_PER_TILE), axis=-1)
    return jnp.transpose(first, (0, 2, 1)).reshape(-1).astype(jnp.int32)


def _fox_kernel(first_ref, q_ref, k_ref, g_ref, vt_ref, z_ref, o_ref):
    gi = pl.program_id(2)
    n_groups = pl.num_programs(2)
    j_first = first_ref[(pl.program_id(0) * N_HEAD_PAIRS + pl.program_id(1)) * n_groups + gi]
    lane = lax.broadcasted_iota(jnp.int32, (TQ, LANES), 1)
    picks = [((lane >= N_SPLIT * h) & (lane < N_SPLIT * (h + 1))).astype(jnp.bfloat16)
             for h in range(HEADS_PER_TILE)]
    chains = [(r, h) for r in range(Q_SUB) for h in range(HEADS_PER_TILE)]

    def qaug(r):
        qms = _head_masked_queries(q_ref[pl.ds(r * TQ, TQ), :])
        return [jnp.concatenate([qms[h], picks[h]], axis=1) for h in range(HEADS_PER_TILE)]

    def keys_at(j):
        rows = pl.ds(pl.multiple_of(j * TKV, TKV), TKV)
        return jnp.concatenate([k_ref[rows, :], g_ref[rows, :]], axis=1), vt_ref[:, rows]

    def body(j, flat):
        kaug, vt_blk = keys_at(j)
        qa = [qaug(r) for r in range(Q_SUB)]
        logits = [_dot_nt(kaug, qa[r][h]) for r, h in chains]
        new = [_online_update(flat[2 * c:2 * c + 2], logits[c], _head_values(vt_blk, h))
               for c, (r, h) in enumerate(chains)]
        return tuple(x for s in new for x in s)

    flat = lax.fori_loop(j_first, gi, body, tuple(x for _ in chains for x in _init_state()))

    kaug, vt_blk = keys_at(gi)
    s_loc = lax.broadcasted_iota(jnp.int32, (TQ, TQ), 0)
    t_loc = lax.broadcasted_iota(jnp.int32, (TQ, TQ), 1)
    logits = []
    for r in range(Q_SUB):
        qa = qaug(r)
        for h in range(HEADS_PER_TILE):
            a = _dot_nt(kaug[:(r + 1) * TQ], qa[h])
            diag = jnp.where(s_loc <= t_loc, a[r * TQ:], NEG)
            logits.append(diag if r == 0 else jnp.concatenate([a[:r * TQ], diag], axis=0))
    for r in range(Q_SUB):
        states = [_online_update(flat[2 * c:2 * c + 2], logits[c], _head_values(vt_blk[:, :(r + 1) * TQ], h))
                  for c, (rr, h) in enumerate(chains) if rr == r]
        _finish(states, z_ref, o_ref, r)


def _fox(first, q, k, gaug, vt, z, batch, seq):
    ng, tok, seq_rows, seq_cols = _attn_specs(seq)
    return pl.pallas_call(
        _fox_kernel,
        grid_spec=pltpu.PrefetchScalarGridSpec(
            num_scalar_prefetch=1,
            grid=(batch, N_HEAD_PAIRS, ng),
            in_specs=[tok, seq_rows, seq_rows, seq_cols, tok],
            out_specs=tok),
        out_shape=jax.ShapeDtypeStruct(q.shape, jnp.bfloat16),
        compiler_params=_params("arbitrary", "arbitrary", "arbitrary"),
        name="fox",
    )(first, q, k, gaug, vt, z)


def _in_weights(w_in, w_gate=None):
    w_in = w_in.astype(jnp.float32)
    e = MIX_WIDTH
    cols = [w_in[:, :e] * (LOG2E / math.sqrt(HEAD_DIM)), w_in[:, e:2 * e], w_in[:, 3 * e:]]
    if w_gate is not None:
        cols.append(jnp.zeros((w_in.shape[0], LANES), jnp.float32).at[:, :N_HEADS].set(w_gate.astype(jnp.float32)))
    return jnp.concatenate(cols, axis=1).astype(jnp.bfloat16), w_in[:, 2 * e:3 * e].T.astype(jnp.bfloat16)


def kernel(x, w_in_a, rel_bias_a, w_out_a, w_in_b, w_f_b, b_f_b, w_out_b, ln_g, ln_b):
    batch, seq, d = x.shape
    assert seq % TG == 0 and TG == TKV and seq >= A_WINDOW
    xf = x.reshape(batch * seq, d).astype(jnp.float32)
    for i in range(DEPTH):
        j = i // N_MIXERS
        if i % N_MIXERS == 0:
            q, k, vt, z = _in_proj(xf, *_in_weights(w_in_a[j]))
            g = _mixer_a(q, k, vt, z, _mixer_a_bias(rel_bias_a[j]), batch, seq)
            w_out = w_out_a[j]
        else:
            q, k, vt, z, lf = _in_proj(xf, *_in_weights(w_in_b[j], w_f_b[j]))
            gaug, stats = _fox_prep(lf, b_f_b[j], q, k, batch, seq)
            g = _fox(_first_needed_step(stats), q, k, gaug, vt, z, batch, seq)
            w_out = w_out_b[j]
        xf = _out_proj_ln(g, xf, w_out.astype(jnp.bfloat16), ln_g[i], ln_b[i])
    return xf.reshape(batch, seq, d).astype(x.dtype)
```

```python
import math

import jax
import jax.numpy as jnp
from jax import lax
from jax.experimental import pallas as pl
from jax.experimental.pallas import tpu as pltpu

D_MODEL = 1024
N_HEADS = 16
HEAD_DIM = 64
MIX_WIDTH = N_HEADS * HEAD_DIM
CHUNK = 64
N_PAST_CHUNKS = 8
MAX_REL = 128
DEPTH = 4
N_MIXERS = 2
LN_EPS = 1e-5
DEEPNORM_ALPHA = (2.0 * DEPTH) ** 0.25
LOG2E = math.log2(math.e)
NEG = -1e30

LANES = 128
HEADS_PER_TILE = LANES // HEAD_DIM
N_HEAD_PAIRS = N_HEADS // HEADS_PER_TILE
BF16_ROWS = 16
N_SPLIT = 3
TQ = 256
Q_SUB = 4
TG = TQ * Q_SUB
TKV = 512
KV_PER_G = TG // TKV
TM_PROJ = 512
VMEM_LIMIT = 56 * 1024 * 1024
PRUNE_LOG2 = 160.0
NORM_SLACK = 1.0 + 2.0 ** -6
PLAIN_EXP_LOG2 = 60.0


def _params(*semantics):
    return pltpu.CompilerParams(dimension_semantics=semantics, vmem_limit_bytes=VMEM_LIMIT)


def _dot_nt(a, b):
    return lax.dot_general(a, b, (((1,), (1,)), ((), ())), preferred_element_type=jnp.float32)


def _split(x, n):
    pieces = []
    for _ in range(n):
        p = x.astype(jnp.bfloat16)
        pieces.append(p)
        x = x - p.astype(jnp.float32)
    return pieces


def _in_proj_kernel(x_ref, w_ref, wvt_ref, q_ref, k_ref, vt_ref, z_ref, *gate_ref):
    xb = x_ref[...].astype(jnp.bfloat16)
    for g, o_ref in enumerate((q_ref, k_ref, z_ref)):
        o_ref[...] = jnp.dot(xb, w_ref[:, g * MIX_WIDTH:(g + 1) * MIX_WIDTH],
                             preferred_element_type=jnp.float32).astype(jnp.bfloat16)
    vt_ref[...] = _dot_nt(wvt_ref[...], xb).astype(jnp.bfloat16)
    if gate_ref:
        gate_ref[0][...] = jnp.dot(xb, w_ref[:, 3 * MIX_WIDTH:], preferred_element_type=jnp.float32)


def _in_proj(xf, w, wvt):
    m = xf.shape[0]
    n = w.shape[1]
    has_gate = n > 3 * MIX_WIDTH
    row = pl.BlockSpec((TM_PROJ, MIX_WIDTH), lambda i: (i, 0))
    col = pl.BlockSpec((MIX_WIDTH, TM_PROJ), lambda i: (0, i))
    tok = jax.ShapeDtypeStruct((m, MIX_WIDTH), jnp.bfloat16)
    out_shape = [tok, tok, jax.ShapeDtypeStruct((MIX_WIDTH, m), jnp.bfloat16), tok]
    out_specs = [row, row, col, row]
    if has_gate:
        out_shape = out_shape + [jax.ShapeDtypeStruct((m, LANES), jnp.float32)]
        out_specs = out_specs + [pl.BlockSpec((TM_PROJ, LANES), lambda i: (i, 0))]
    return pl.pallas_call(
        _in_proj_kernel,
        grid=(m // TM_PROJ,),
        in_specs=[pl.BlockSpec((TM_PROJ, D_MODEL), lambda i: (i, 0)),
                  pl.BlockSpec((D_MODEL, n), lambda i: (0, 0)),
                  pl.BlockSpec((MIX_WIDTH, D_MODEL), lambda i: (0, 0))],
        out_specs=out_specs,
        out_shape=out_shape,
        compiler_params=_params("arbitrary"),
        name="in_proj_gate" if has_gate else "in_proj",
    )(xf, w, wvt)


def _out_proj_ln_kernel(g_ref, x_ref, w_ref, gam_ref, bet_ref, o_ref):
    y = jnp.dot(g_ref[...], w_ref[...], preferred_element_type=jnp.float32)
    r = DEEPNORM_ALPHA * x_ref[...] + y
    mu = jnp.mean(r, axis=-1, keepdims=True)
    c = r - mu
    var = jnp.mean(c * c, axis=-1, keepdims=True)
    o_ref[...] = c * lax.rsqrt(var + LN_EPS) * gam_ref[...] + bet_ref[...]


def _out_proj_ln(g, xf, w, gamma, beta):
    m = xf.shape[0]
    return pl.pallas_call(
        _out_proj_ln_kernel,
        grid=(m // TM_PROJ,),
        in_specs=[pl.BlockSpec((TM_PROJ, MIX_WIDTH), lambda i: (i, 0)),
                  pl.BlockSpec((TM_PROJ, D_MODEL), lambda i: (i, 0)),
                  pl.BlockSpec((MIX_WIDTH, D_MODEL), lambda i: (0, 0)),
                  pl.BlockSpec((1, D_MODEL), lambda i: (0, 0)),
                  pl.BlockSpec((1, D_MODEL), lambda i: (0, 0))],
        out_specs=pl.BlockSpec((TM_PROJ, D_MODEL), lambda i: (i, 0)),
        out_shape=jax.ShapeDtypeStruct((m, D_MODEL), jnp.float32),
        compiler_params=_params("arbitrary"),
        name="out_proj_ln",
    )(g, xf, w, gamma.reshape(1, D_MODEL), beta.reshape(1, D_MODEL))


ACC_ROWS = HEAD_DIM + BF16_ROWS


def _head_masked_queries(q2):
    lane = lax.broadcasted_iota(jnp.int32, q2.shape, 1)
    zero = jnp.zeros_like(q2)
    return [jnp.where((lane >= HEAD_DIM * h) & (lane < HEAD_DIM * (h + 1)), q2, zero)
            for h in range(HEADS_PER_TILE)]


def _head_values(vt_blk, h):
    ones = jnp.ones((BF16_ROWS, vt_blk.shape[1]), vt_blk.dtype)
    return jnp.concatenate([vt_blk[HEAD_DIM * h:HEAD_DIM * (h + 1), :], ones], axis=0)


def _online_update(state, a, vt_h):
    u, acc = state
    u_new = jnp.maximum(u, jnp.max(a, axis=0, keepdims=True))
    p = jnp.exp2(a - u_new).astype(jnp.bfloat16)
    alpha = jnp.exp2(u - u_new)
    acc = alpha * acc + jnp.dot(vt_h, p, preferred_element_type=jnp.float32)
    return u_new, acc


def _init_state():
    return (jnp.full((1, TQ), NEG, jnp.float32), jnp.zeros((ACC_ROWS, TQ), jnp.float32))


def _finish(states, z_ref, o_ref, r):
    ot = jnp.concatenate([acc[:HEAD_DIM] / acc[HEAD_DIM:HEAD_DIM + 1] for _, acc in states], axis=0)
    rows = pl.ds(r * TQ, TQ)
    z = z_ref[rows, :].astype(jnp.float32)
    o_ref[rows, :] = (ot.T * (z / (1.0 + jnp.exp(-z)))).astype(o_ref.dtype)


def _attn_specs(seq):
    ng = seq // TG
    tok = pl.BlockSpec((TG, LANES), lambda b, hp, i, *_: (b * ng + i, hp))
    seq_rows = pl.BlockSpec((seq, LANES), lambda b, hp, i, *_: (b, hp))
    seq_cols = pl.BlockSpec((LANES, seq), lambda b, hp, i, *_: (hp, b))
    return ng, tok, seq_rows, seq_cols


A_PAST = N_PAST_CHUNKS * CHUNK
A_WINDOW = A_PAST + TQ
A_BIAS_ROWS = 2 * A_PAST + TQ


def _mixer_a_kernel(q_ref, k_ref, vt_ref, z_ref, bias_ref, o_ref):
    gi = pl.program_id(2)
    chains = []
    for r in range(Q_SUB):
        start = (gi * Q_SUB + r) * TQ - A_PAST
        row0 = pl.multiple_of(jnp.maximum(start, 0), TQ)
        brow = pl.multiple_of(row0 - start, TQ)
        k_win = k_ref[pl.ds(row0, A_WINDOW), :]
        qms = _head_masked_queries(q_ref[pl.ds(r * TQ, TQ), :])
        logits = [_dot_nt(k_win, qms[h]) + bias_ref[h, pl.ds(brow, A_WINDOW), :]
                  for h in range(HEADS_PER_TILE)]
        chains.append((row0, logits))
    for r, (row0, logits) in enumerate(chains):
        vt_win = vt_ref[:, pl.ds(row0, A_WINDOW)]
        states = [_online_update(_init_state(), logits[h], _head_values(vt_win, h))
                  for h in range(HEADS_PER_TILE)]
        _finish(states, z_ref, o_ref, r)


def _mixer_a_bias(rel_bias):
    n = A_BIAS_ROWS
    L = n + TQ + 1
    m = jnp.arange(L)
    d = jnp.where(m < L - n, m, m - L) + A_PAST
    strip = rel_bias.astype(jnp.float32)[:, jnp.clip(d, -MAX_REL, MAX_REL) + MAX_REL] * LOG2E
    skew = jnp.broadcast_to(strip[:, None, :], (N_HEADS, n, L)).reshape(N_HEADS, n * L)
    toep = skew[:, :n * (L - 1)].reshape(N_HEADS, n, L - 1)[:, :, :TQ]
    c = jnp.arange(n)[:, None]
    r = jnp.arange(TQ)[None, :]
    dchunk = r // CHUNK - (c - A_PAST) // CHUNK
    visible = (dchunk >= 0) & (dchunk <= N_PAST_CHUNKS)
    return jnp.where(visible[None], toep, NEG)


def _mixer_a(q, k, vt, z, bias, batch, seq):
    ng, tok, seq_rows, seq_cols = _attn_specs(seq)
    return pl.pallas_call(
        _mixer_a_kernel,
        grid=(batch, N_HEAD_PAIRS, ng),
        in_specs=[tok, seq_rows, seq_cols, tok,
                  pl.BlockSpec((HEADS_PER_TILE, A_BIAS_ROWS, TQ), lambda b, hp, i: (hp, 0, 0))],
        out_specs=tok,
        out_shape=jax.ShapeDtypeStruct(q.shape, jnp.bfloat16),
        compiler_params=_params("arbitrary", "arbitrary", "arbitrary"),
        name="mixer_a",
    )(q, k, vt, z, bias)


N_STATS = 8


def _fox_prep_kernel(lf_ref, bf_ref, q_ref, k_ref, tri_ref, sel_ref, ones_ref, hsel_ref,
                     gk_ref, gq_ref, st_ref, carry_ref):
    @pl.when(pl.program_id(1) == 0)
    def _():
        carry_ref[...] = jnp.zeros_like(carry_ref)

    pre = lf_ref[...] + bf_ref[...]
    logf = (jnp.minimum(pre, 0.0) - jnp.log(1.0 + jnp.exp(-jnp.abs(pre)))) * LOG2E
    tri = tri_ref[...]
    csum = sum(jnp.dot(tri, piece, preferred_element_type=jnp.float32) for piece in _split(logf, N_SPLIT))
    f = csum + carry_ref[...]
    carry_ref[...] = f[TKV - 1:TKV, :]
    pieces = _split(-f, N_SPLIT)
    for side, o_ref in enumerate((gk_ref, gq_ref)):
        o_ref[...] = (sum(jnp.dot(piece, sel_ref[side, n], preferred_element_type=jnp.float32)
                          for n, piece in enumerate(pieces)) + ones_ref[side:side + 1, :]).astype(jnp.bfloat16)

    def max_norm(x_ref):
        x = x_ref[...].astype(jnp.float32)
        sq = sum(jnp.dot(piece, hsel_ref[...], preferred_element_type=jnp.float32)
                 for piece in _split(x * x, 2))
        return jnp.sqrt(jnp.max(sq, axis=0, keepdims=True))

    st_ref[...] = jnp.concatenate(
        [max_norm(q_ref), max_norm(k_ref), f[0:1, :], f[TKV - 1:TKV, :],
         jnp.zeros((N_STATS - 4, LANES), jnp.float32)], axis=0)


def _fox_prep(lf, b_f, q, k, batch, seq):
    nb = seq // TKV
    width = N_HEAD_PAIRS * LANES
    bf = jnp.pad(b_f.astype(jnp.float32), (0, LANES - N_HEADS)).reshape(1, LANES)
    tri = (jnp.arange(TKV)[:, None] >= jnp.arange(TKV)[None, :]).astype(jnp.bfloat16)
    src = jnp.arange(LANES)[:, None]
    dst = jnp.arange(width)[None, :]
    base = (src // HEADS_PER_TILE) * LANES + (src % HEADS_PER_TILE) * N_SPLIT
    n_aug = HEADS_PER_TILE * N_SPLIT
    sel = jnp.stack([
        jnp.stack([jnp.where((dst == base + side * n_aug + n) & (src < N_HEADS), 1.0 - 2.0 * side, 0.0)
                   for n in range(N_SPLIT)]) for side in range(2)]).astype(jnp.bfloat16)
    lane = jnp.arange(width) % LANES
    ones = jnp.stack([(lane >= n_aug) & (lane < 2 * n_aug), lane < n_aug]).astype(jnp.float32)
    hsel = (jnp.arange(MIX_WIDTH)[:, None] // HEAD_DIM == jnp.arange(LANES)[None, :]).astype(jnp.bfloat16)
    tokspec = pl.BlockSpec((TKV, MIX_WIDTH), lambda b, i: (b * nb + i, 0))
    aug = jax.ShapeDtypeStruct((batch * seq, width), jnp.bfloat16)
    return pl.pallas_call(
        _fox_prep_kernel,
        grid=(batch, nb),
        in_specs=[pl.BlockSpec((TKV, LANES), lambda b, i: (b * nb + i, 0)),
                  pl.BlockSpec((1, LANES), lambda b, i: (0, 0)),
                  tokspec, tokspec,
                  pl.BlockSpec((TKV, TKV), lambda b, i: (0, 0)),
                  pl.BlockSpec((2, N_SPLIT, LANES, width), lambda b, i: (0, 0, 0, 0)),
                  pl.BlockSpec((2, width), lambda b, i: (0, 0)),
                  pl.BlockSpec((MIX_WIDTH, LANES), lambda b, i: (0, 0))],
        out_specs=[tokspec, tokspec, pl.BlockSpec((None, None, N_STATS, LANES), lambda b, i: (b, i, 0, 0))],
        out_shape=[aug, aug, jax.ShapeDtypeStruct((batch, nb, N_STATS, LANES), jnp.float32)],
        scratch_shapes=[pltpu.VMEM((1, LANES), jnp.float32)],
        compiler_params=_params("arbitrary", "arbitrary"),
        name="fox_prep",
    )(lf, bf, q, k, tri, sel, ones, hsel)


def _fox_schedule(stats):
    b, nb = stats.shape[0], stats.shape[1]
    ng = nb // KV_PER_G
    qn = stats[:, :, 0, :N_HEADS] * NORM_SLACK
    kn = stats[:, :, 1, :N_HEADS] * NORM_SLACK
    f_first, f_last = stats[:, :, 2, :N_HEADS], stats[:, :, 3, :N_HEADS]
    qn_g = jnp.max(qn.reshape(b, ng, KV_PER_G, N_HEADS), axis=2)
    kn_g = jnp.max(kn.reshape(b, ng, KV_PER_G, N_HEADS), axis=2)
    f_first_g = f_first.reshape(b, ng, KV_PER_G, N_HEADS)[:, :, 0]
    bound = (qn_g[:, :, None] * kn[:, None, :] + f_first_g[:, :, None] - f_last[:, None, :]
             + (qn_g * kn_g)[:, :, None])
    i_idx = jnp.arange(ng)[None, :, None, None]
    j_idx = jnp.arange(nb)[None, None, :, None]
    skip = (bound < -PRUNE_LOG2) & (j_idx < i_idx * KV_PER_G)
    first = jnp.min(jnp.where(skip, nb, j_idx), axis=2)
    first = jnp.minimum(first, jnp.arange(ng)[None, :, None] * KV_PER_G)
    first = jnp.min(first.reshape(b, ng, N_HEAD_PAIRS, HEADS_PER_TILE), axis=-1)
    first = jnp.transpose(first, (0, 2, 1)).reshape(-1)
    worst = jnp.max(qn, axis=1) * jnp.max(kn, axis=1)
    plain = jnp.all((worst < PLAIN_EXP_LOG2).reshape(b, N_HEAD_PAIRS, HEADS_PER_TILE), axis=-1).reshape(-1)
    return jnp.concatenate([first, plain.astype(first.dtype)]).astype(jnp.int32)


def _plain_update(state, a, vt_h):
    return state[0], state[1] + jnp.dot(vt_h, jnp.exp2(a).astype(jnp.bfloat16),
                                        preferred_element_type=jnp.float32)


def _fox_kernel(sched_ref, q_ref, gq_ref, k_ref, gk_ref, vt_ref, z_ref, o_ref):
    gi = pl.program_id(2)
    n_groups = pl.num_programs(2)
    pair = pl.program_id(0) * N_HEAD_PAIRS + pl.program_id(1)
    j_first = sched_ref[pair * n_groups + gi]
    plain = sched_ref[pl.num_programs(0) * N_HEAD_PAIRS * n_groups + pair]
    lane = lax.broadcasted_iota(jnp.int32, (TQ, LANES), 1)
    n_aug = HEADS_PER_TILE * N_SPLIT
    chains = [(r, h) for r in range(Q_SUB) for h in range(HEADS_PER_TILE)]

    def qaug(r):
        rows = pl.ds(r * TQ, TQ)
        qms = _head_masked_queries(q_ref[rows, :])
        gq = gq_ref[rows, :]
        out = []
        for h in range(HEADS_PER_TILE):
            own = ((lane >= N_SPLIT * h) & (lane < N_SPLIT * (h + 1))) | \
                  ((lane >= n_aug + N_SPLIT * h) & (lane < n_aug + N_SPLIT * (h + 1)))
            out.append(jnp.concatenate([qms[h], jnp.where(own, gq, jnp.zeros_like(gq))], axis=1))
        return out

    def keys_at(row0, n):
        rows = pl.ds(row0, n)
        return jnp.concatenate([k_ref[rows, :], gk_ref[rows, :]], axis=1), vt_ref[:, rows]

    def run(update, init):
        qa = [qaug(r) for r in range(Q_SUB)]

        def body(j, flat):
            kaug, vt_blk = keys_at(pl.multiple_of(j * TKV, TKV), TKV)
            logits = [_dot_nt(kaug, qa[r][h]) for r, h in chains]
            new = [update(flat[2 * c:2 * c + 2], logits[c], _head_values(vt_blk, h))
                   for c, (r, h) in enumerate(chains)]
            return tuple(x for s in new for x in s)

        flat = lax.fori_loop(j_first, gi * KV_PER_G, body, tuple(x for _ in chains for x in init))

        kaug, vt_blk = keys_at(pl.multiple_of(gi * TG, TG), TG)
        s_loc = lax.broadcasted_iota(jnp.int32, (TQ, TQ), 0)
        t_loc = lax.broadcasted_iota(jnp.int32, (TQ, TQ), 1)
        logits = []
        for r, h in chains:
            a = _dot_nt(kaug[:(r + 1) * TQ], qa[r][h])
            diag = jnp.where(s_loc <= t_loc, a[r * TQ:], NEG)
            logits.append(diag if r == 0 else jnp.concatenate([a[:r * TQ], diag], axis=0))
        for r in range(Q_SUB):
            states = [update(flat[2 * c:2 * c + 2], logits[c], _head_values(vt_blk[:, :(r + 1) * TQ], h))
                      for c, (rr, h) in enumerate(chains) if rr == r]
            _finish(states, z_ref, o_ref, r)

    @pl.when(plain != 0)
    def _():
        run(_plain_update, (jnp.zeros((1, TQ), jnp.float32), jnp.zeros((ACC_ROWS, TQ), jnp.float32)))

    @pl.when(plain == 0)
    def _():
        run(_online_update, _init_state())


def _fox(sched, q, gq, k, gk, vt, z, batch, seq):
    ng, tok, seq_rows, seq_cols = _attn_specs(seq)
    return pl.pallas_call(
        _fox_kernel,
        grid_spec=pltpu.PrefetchScalarGridSpec(
            num_scalar_prefetch=1,
            grid=(batch, N_HEAD_PAIRS, ng),
            in_specs=[tok, tok, seq_rows, seq_rows, seq_cols, tok],
            out_specs=tok),
        out_shape=jax.ShapeDtypeStruct(q.shape, jnp.bfloat16),
        compiler_params=_params("arbitrary", "arbitrary", "arbitrary"),
        name="fox",
    )(sched, q, gq, k, gk, vt, z)


def _in_weights(w_in, w_gate=None):
    w_in = w_in.astype(jnp.float32)
    e = MIX_WIDTH
    cols = [w_in[:, :e] * (LOG2E / math.sqrt(HEAD_DIM)), w_in[:, e:2 * e], w_in[:, 3 * e:]]
    if w_gate is not None:
        cols.append(jnp.pad(w_gate.astype(jnp.float32), ((0, 0), (0, LANES - N_HEADS))))
    return jnp.concatenate(cols, axis=1).astype(jnp.bfloat16), w_in[:, 2 * e:3 * e].T.astype(jnp.bfloat16)


def kernel(x, w_in_a, rel_bias_a, w_out_a, w_in_b, w_f_b, b_f_b, w_out_b, ln_g, ln_b):
    batch, seq, d = x.shape
    assert seq % TG == 0 and TG % TKV == 0 and seq >= A_WINDOW
    xf = x.reshape(batch * seq, d).astype(jnp.float32)
    for i in range(DEPTH):
        j = i // N_MIXERS
        if i % N_MIXERS == 0:
            q, k, vt, z = _in_proj(xf, *_in_weights(w_in_a[j]))
            g = _mixer_a(q, k, vt, z, _mixer_a_bias(rel_bias_a[j]), batch, seq)
            w_out = w_out_a[j]
        else:
            q, k, vt, z, lf = _in_proj(xf, *_in_weights(w_in_b[j], w_f_b[j]))
            gk, gq, stats = _fox_prep(lf, b_f_b[j], q, k, batch, seq)
            g = _fox(_fox_schedule(stats), q, gq, k, gk, vt, z, batch, seq)
            w_out = w_out_b[j]
        xf = _out_proj_ln(g, xf, w_out.astype(jnp.bfloat16), ln_g[i], ln_b[i])
    return xf.reshape(batch, seq, d).astype(x.dtype)
```

```python
import math

import jax
import jax.numpy as jnp
from jax import lax
from jax.experimental import pallas as pl
from jax.experimental.pallas import tpu as pltpu

D_MODEL = 1024
N_HEADS = 16
HEAD_DIM = 64
MIX_WIDTH = N_HEADS * HEAD_DIM
CHUNK = 64
N_PAST_CHUNKS = 8
MAX_REL = 128
DEPTH = 4
N_MIXERS = 2
LN_EPS = 1e-5
DEEPNORM_ALPHA = (2.0 * DEPTH) ** 0.25
LOG2E = math.log2(math.e)
NEG = -1e30

LANES = 128
HEADS_PER_TILE = LANES // HEAD_DIM
N_HEAD_PAIRS = N_HEADS // HEADS_PER_TILE
BF16_ROWS = 16
N_SPLIT = 3
TQ = 256
Q_SUB = 4
TG = TQ * Q_SUB
TKV = 512
KV_PER_G = TG // TKV
TM_PROJ = 512
VMEM_LIMIT = 56 * 1024 * 1024
PRUNE_LOG2 = 160.0
NORM_SLACK = 1.0 + 2.0 ** -6
PLAIN_EXP_LOG2 = 60.0


def _params(*semantics):
    return pltpu.CompilerParams(dimension_semantics=semantics, vmem_limit_bytes=VMEM_LIMIT)


def _dot_nt(a, b):
    return lax.dot_general(a, b, (((1,), (1,)), ((), ())), preferred_element_type=jnp.float32)


def _split(x, n):
    pieces = []
    for _ in range(n):
        p = x.astype(jnp.bfloat16)
        pieces.append(p)
        x = x - p.astype(jnp.float32)
    return pieces


N_STATS = 8


def _in_proj_kernel(x_ref, w_ref, wvt_ref, hsel_ref, q_ref, k_ref, vt_ref, z_ref, nst_ref, *gate_ref):
    xb = x_ref[...].astype(jnp.bfloat16)
    sq_max = []
    for g, o_ref in enumerate((q_ref, k_ref, z_ref)):
        o = jnp.dot(xb, w_ref[:, g * MIX_WIDTH:(g + 1) * MIX_WIDTH],
                    preferred_element_type=jnp.float32).astype(jnp.bfloat16)
        o_ref[...] = o
        if g < 2:
            of = o.astype(jnp.float32)
            sq = jnp.dot((of * of).astype(jnp.bfloat16), hsel_ref[...], preferred_element_type=jnp.float32)
            sq_max.append(jnp.max(sq, axis=0, keepdims=True))
    nst_ref[...] = jnp.concatenate(sq_max + [jnp.zeros((N_STATS - 2, LANES), jnp.float32)], axis=0)
    vt_ref[...] = _dot_nt(wvt_ref[...], xb).astype(jnp.bfloat16)
    if gate_ref:
        gate_ref[0][...] = jnp.dot(xb, w_ref[:, 3 * MIX_WIDTH:], preferred_element_type=jnp.float32)


def _in_proj(xf, w, wvt):
    m = xf.shape[0]
    n = w.shape[1]
    has_gate = n > 3 * MIX_WIDTH
    hsel = (jnp.arange(MIX_WIDTH)[:, None] // HEAD_DIM == jnp.arange(LANES)[None, :]).astype(jnp.bfloat16)
    row = pl.BlockSpec((TM_PROJ, MIX_WIDTH), lambda i: (i, 0))
    col = pl.BlockSpec((MIX_WIDTH, TM_PROJ), lambda i: (0, i))
    tok = jax.ShapeDtypeStruct((m, MIX_WIDTH), jnp.bfloat16)
    out_shape = [tok, tok, jax.ShapeDtypeStruct((MIX_WIDTH, m), jnp.bfloat16), tok,
                 jax.ShapeDtypeStruct((m // TM_PROJ, N_STATS, LANES), jnp.float32)]
    out_specs = [row, row, col, row, pl.BlockSpec((None, N_STATS, LANES), lambda i: (i, 0, 0))]
    if has_gate:
        out_shape = out_shape + [jax.ShapeDtypeStruct((m, LANES), jnp.float32)]
        out_specs = out_specs + [pl.BlockSpec((TM_PROJ, LANES), lambda i: (i, 0))]
    return pl.pallas_call(
        _in_proj_kernel,
        grid=(m // TM_PROJ,),
        in_specs=[pl.BlockSpec((TM_PROJ, D_MODEL), lambda i: (i, 0)),
                  pl.BlockSpec((D_MODEL, n), lambda i: (0, 0)),
                  pl.BlockSpec((MIX_WIDTH, D_MODEL), lambda i: (0, 0)),
                  pl.BlockSpec((MIX_WIDTH, LANES), lambda i: (0, 0))],
        out_specs=out_specs,
        out_shape=out_shape,
        compiler_params=_params("arbitrary"),
        name="in_proj_gate" if has_gate else "in_proj",
    )(xf, w, wvt, hsel)


def _max_norms(nst, batch):
    nb = nst.shape[0] // batch
    n = jnp.sqrt(nst[:, :2, :N_HEADS]).reshape(batch, nb, 2, N_HEADS) * NORM_SLACK
    return n[:, :, 0], n[:, :, 1]


def _out_proj_ln_kernel(g_ref, x_ref, w_ref, gam_ref, bet_ref, o_ref):
    y = jnp.dot(g_ref[...], w_ref[...], preferred_element_type=jnp.float32)
    r = DEEPNORM_ALPHA * x_ref[...] + y
    mu = jnp.mean(r, axis=-1, keepdims=True)
    c = r - mu
    var = jnp.mean(c * c, axis=-1, keepdims=True)
    o_ref[...] = c * lax.rsqrt(var + LN_EPS) * gam_ref[...] + bet_ref[...]


def _out_proj_ln(g, xf, w, gamma, beta):
    m = xf.shape[0]
    return pl.pallas_call(
        _out_proj_ln_kernel,
        grid=(m // TM_PROJ,),
        in_specs=[pl.BlockSpec((TM_PROJ, MIX_WIDTH), lambda i: (i, 0)),
                  pl.BlockSpec((TM_PROJ, D_MODEL), lambda i: (i, 0)),
                  pl.BlockSpec((MIX_WIDTH, D_MODEL), lambda i: (0, 0)),
                  pl.BlockSpec((1, D_MODEL), lambda i: (0, 0)),
                  pl.BlockSpec((1, D_MODEL), lambda i: (0, 0))],
        out_specs=pl.BlockSpec((TM_PROJ, D_MODEL), lambda i: (i, 0)),
        out_shape=jax.ShapeDtypeStruct((m, D_MODEL), jnp.float32),
        compiler_params=_params("arbitrary"),
        name="out_proj_ln",
    )(g, xf, w, gamma.reshape(1, D_MODEL), beta.reshape(1, D_MODEL))


ACC_ROWS = HEAD_DIM + BF16_ROWS


def _head_masked_queries(q2):
    lane = lax.broadcasted_iota(jnp.int32, q2.shape, 1)
    zero = jnp.zeros_like(q2)
    return [jnp.where((lane >= HEAD_DIM * h) & (lane < HEAD_DIM * (h + 1)), q2, zero)
            for h in range(HEADS_PER_TILE)]


def _head_values(vt_blk, h):
    ones = jnp.ones((BF16_ROWS, vt_blk.shape[1]), vt_blk.dtype)
    return jnp.concatenate([vt_blk[HEAD_DIM * h:HEAD_DIM * (h + 1), :], ones], axis=0)


def _online_update(state, a, vt_h):
    u, acc = state
    u_new = jnp.maximum(u, jnp.max(a, axis=0, keepdims=True))
    p = jnp.exp2(a - u_new).astype(jnp.bfloat16)
    alpha = jnp.exp2(u - u_new)
    acc = alpha * acc + jnp.dot(vt_h, p, preferred_element_type=jnp.float32)
    return u_new, acc


def _plain_update(state, a, vt_h):
    return state[0], state[1] + jnp.dot(vt_h, jnp.exp2(a).astype(jnp.bfloat16),
                                        preferred_element_type=jnp.float32)


def _init_state():
    return (jnp.full((1, TQ), NEG, jnp.float32), jnp.zeros((ACC_ROWS, TQ), jnp.float32))


def _finish(states, z_ref, o_ref, r):
    ot = jnp.concatenate([acc[:HEAD_DIM] / acc[HEAD_DIM:HEAD_DIM + 1] for _, acc in states], axis=0)
    rows = pl.ds(r * TQ, TQ)
    z = z_ref[rows, :].astype(jnp.float32)
    o_ref[rows, :] = (ot.T * (z / (1.0 + jnp.exp(-z)))).astype(o_ref.dtype)


def _attn_specs(seq):
    ng = seq // TG
    tok = pl.BlockSpec((TG, LANES), lambda b, hp, i, *_: (b * ng + i, hp))
    seq_rows = pl.BlockSpec((seq, LANES), lambda b, hp, i, *_: (b, hp))
    seq_cols = pl.BlockSpec((LANES, seq), lambda b, hp, i, *_: (hp, b))
    return ng, tok, seq_rows, seq_cols


A_PAST = N_PAST_CHUNKS * CHUNK
A_WINDOW = A_PAST + TQ
A_BIAS_ROWS = 2 * A_PAST + TQ


def _mixer_a_kernel(plain_ref, q_ref, k_ref, vt_ref, z_ref, bias_ref, o_ref):
    gi = pl.program_id(2)
    plain = plain_ref[pl.program_id(0) * N_HEAD_PAIRS + pl.program_id(1)]

    def run(update, init):
        chains = []
        for r in range(Q_SUB):
            start = (gi * Q_SUB + r) * TQ - A_PAST
            row0 = pl.multiple_of(jnp.maximum(start, 0), TQ)
            brow = pl.multiple_of(row0 - start, TQ)
            k_win = k_ref[pl.ds(row0, A_WINDOW), :]
            qms = _head_masked_queries(q_ref[pl.ds(r * TQ, TQ), :])
            logits = [_dot_nt(k_win, qms[h]) + bias_ref[h, pl.ds(brow, A_WINDOW), :]
                      for h in range(HEADS_PER_TILE)]
            chains.append((row0, logits))
        for r, (row0, logits) in enumerate(chains):
            vt_win = vt_ref[:, pl.ds(row0, A_WINDOW)]
            states = [update(init, logits[h], _head_values(vt_win, h)) for h in range(HEADS_PER_TILE)]
            _finish(states, z_ref, o_ref, r)

    @pl.when(plain != 0)
    def _():
        run(_plain_update, _init_state())

    @pl.when(plain == 0)
    def _():
        run(_online_update, _init_state())


A_BAND_LO = A_PAST - MAX_REL
A_BAND_HI = A_PAST + TQ + CHUNK


def _mixer_a_bias(rel_bias):
    rb = rel_bias.astype(jnp.float32) * LOG2E
    c = jnp.arange(A_BIAS_ROWS)[:, None]
    r = jnp.arange(TQ)[None, :]
    dist = r - (c - A_PAST)
    idx = jnp.clip(dist[A_BAND_LO:A_BAND_HI], -MAX_REL, MAX_REL) + MAX_REL
    far = jnp.broadcast_to(rb[:, 2 * MAX_REL][:, None, None], (N_HEADS, A_BAND_LO, TQ))
    late = jnp.zeros((N_HEADS, A_BIAS_ROWS - A_BAND_HI, TQ), jnp.float32)
    table = jnp.concatenate([far, rb[:, idx], late], axis=1)
    dchunk = r // CHUNK - (c - A_PAST) // CHUNK
    visible = (dchunk >= 0) & (dchunk <= N_PAST_CHUNKS)
    return jnp.where(visible[None], table, NEG)


def _mixer_a_schedule(qn, kn, rel_bias):
    worst = jnp.max(qn, axis=1) * jnp.max(kn, axis=1) + jnp.max(jnp.abs(rel_bias), axis=-1)[None, :] * LOG2E
    ok = worst < PLAIN_EXP_LOG2
    return jnp.all(ok.reshape(-1, N_HEAD_PAIRS, HEADS_PER_TILE), axis=-1).reshape(-1).astype(jnp.int32)


def _mixer_a(plain, q, k, vt, z, bias, batch, seq):
    ng, tok, seq_rows, seq_cols = _attn_specs(seq)
    return pl.pallas_call(
        _mixer_a_kernel,
        grid_spec=pltpu.PrefetchScalarGridSpec(
            num_scalar_prefetch=1,
            grid=(batch, N_HEAD_PAIRS, ng),
            in_specs=[tok, seq_rows, seq_cols, tok,
                      pl.BlockSpec((HEADS_PER_TILE, A_BIAS_ROWS, TQ), lambda b, hp, i, *_: (hp, 0, 0))],
            out_specs=tok),
        out_shape=jax.ShapeDtypeStruct(q.shape, jnp.bfloat16),
        compiler_params=_params("arbitrary", "arbitrary", "arbitrary"),
        name="mixer_a",
    )(plain, q, k, vt, z, bias)


def _fox_prep_kernel(lf_ref, bf_ref, tri_ref, sel_ref, ones_ref, gk_ref, gq_ref, st_ref, carry_ref):
    @pl.when(pl.program_id(1) == 0)
    def _():
        carry_ref[...] = jnp.zeros_like(carry_ref)

    pre = lf_ref[...] + bf_ref[...]
    logf = (jnp.minimum(pre, 0.0) - jnp.log(1.0 + jnp.exp(-jnp.abs(pre)))) * LOG2E
    tri = tri_ref[...]
    csum = sum(jnp.dot(tri, piece, preferred_element_type=jnp.float32) for piece in _split(logf, N_SPLIT))
    f = csum + carry_ref[...]
    carry_ref[...] = f[TKV - 1:TKV, :]
    pieces = _split(-f, N_SPLIT)
    for side, o_ref in enumerate((gk_ref, gq_ref)):
        o_ref[...] = (sum(jnp.dot(piece, sel_ref[side, n], preferred_element_type=jnp.float32)
                          for n, piece in enumerate(pieces)) + ones_ref[side:side + 1, :]).astype(jnp.bfloat16)

    st_ref[...] = jnp.concatenate(
        [f[0:1, :], f[TKV - 1:TKV, :], jnp.zeros((N_STATS - 2, LANES), jnp.float32)], axis=0)


def _fox_prep(lf, b_f, batch, seq):
    nb = seq // TKV
    width = N_HEAD_PAIRS * LANES
    bf = jnp.pad(b_f.astype(jnp.float32), (0, LANES - N_HEADS)).reshape(1, LANES)
    tri = (jnp.arange(TKV)[:, None] >= jnp.arange(TKV)[None, :]).astype(jnp.bfloat16)
    src = jnp.arange(LANES)[:, None]
    dst = jnp.arange(width)[None, :]
    base = (src // HEADS_PER_TILE) * LANES + (src % HEADS_PER_TILE) * N_SPLIT
    n_aug = HEADS_PER_TILE * N_SPLIT
    sel = jnp.stack([
        jnp.stack([jnp.where((dst == base + side * n_aug + n) & (src < N_HEADS), 1.0 - 2.0 * side, 0.0)
                   for n in range(N_SPLIT)]) for side in range(2)]).astype(jnp.bfloat16)
    lane = jnp.arange(width) % LANES
    ones = jnp.stack([(lane >= n_aug) & (lane < 2 * n_aug), lane < n_aug]).astype(jnp.float32)
    tokspec = pl.BlockSpec((TKV, width), lambda b, i: (b * nb + i, 0))
    aug = jax.ShapeDtypeStruct((batch * seq, width), jnp.bfloat16)
    return pl.pallas_call(
        _fox_prep_kernel,
        grid=(batch, nb),
        in_specs=[pl.BlockSpec((TKV, LANES), lambda b, i: (b * nb + i, 0)),
                  pl.BlockSpec((1, LANES), lambda b, i: (0, 0)),
                  pl.BlockSpec((TKV, TKV), lambda b, i: (0, 0)),
                  pl.BlockSpec((2, N_SPLIT, LANES, width), lambda b, i: (0, 0, 0, 0)),
                  pl.BlockSpec((2, width), lambda b, i: (0, 0))],
        out_specs=[tokspec, tokspec, pl.BlockSpec((None, None, N_STATS, LANES), lambda b, i: (b, i, 0, 0))],
        out_shape=[aug, aug, jax.ShapeDtypeStruct((batch, nb, N_STATS, LANES), jnp.float32)],
        scratch_shapes=[pltpu.VMEM((1, LANES), jnp.float32)],
        compiler_params=_params("arbitrary", "arbitrary"),
        name="fox_prep",
    )(lf, bf, tri, sel, ones)


def _fox_schedule(qn, kn, f_stats):
    b, nb = f_stats.shape[0], f_stats.shape[1]
    ng = nb // KV_PER_G
    f_first, f_last = f_stats[:, :, 0, :N_HEADS], f_stats[:, :, 1, :N_HEADS]
    qn_g = jnp.max(qn.reshape(b, ng, KV_PER_G, N_HEADS), axis=2)
    kn_g = jnp.max(kn.reshape(b, ng, KV_PER_G, N_HEADS), axis=2)
    f_first_g = f_first.reshape(b, ng, KV_PER_G, N_HEADS)[:, :, 0]
    bound = (qn_g[:, :, None] * kn[:, None, :] + f_first_g[:, :, None] - f_last[:, None, :]
             + (qn_g * kn_g)[:, :, None])
    i_idx = jnp.arange(ng)[None, :, None, None]
    j_idx = jnp.arange(nb)[None, None, :, None]
    skip = (bound < -PRUNE_LOG2) & (j_idx < i_idx * KV_PER_G)
    first = jnp.min(jnp.where(skip, nb, j_idx), axis=2)
    first = jnp.minimum(first, jnp.arange(ng)[None, :, None] * KV_PER_G)
    first = jnp.min(first.reshape(b, ng, N_HEAD_PAIRS, HEADS_PER_TILE), axis=-1)
    first = jnp.transpose(first, (0, 2, 1)).reshape(-1)
    worst = jnp.max(qn, axis=1) * jnp.max(kn, axis=1)
    plain = jnp.all((worst < PLAIN_EXP_LOG2).reshape(b, N_HEAD_PAIRS, HEADS_PER_TILE), axis=-1).reshape(-1)
    return jnp.concatenate([first, plain.astype(first.dtype)]).astype(jnp.int32)


def _fox_kernel(sched_ref, q_ref, gq_ref, k_ref, gk_ref, vt_ref, z_ref, o_ref):
    gi = pl.program_id(2)
    n_groups = pl.num_programs(2)
    pair = pl.program_id(0) * N_HEAD_PAIRS + pl.program_id(1)
    j_first = sched_ref[pair * n_groups + gi]
    plain = sched_ref[pl.num_programs(0) * N_HEAD_PAIRS * n_groups + pair]
    lane = lax.broadcasted_iota(jnp.int32, (TQ, LANES), 1)
    n_aug = HEADS_PER_TILE * N_SPLIT
    chains = [(r, h) for r in range(Q_SUB) for h in range(HEADS_PER_TILE)]

    def qaug(r):
        rows = pl.ds(r * TQ, TQ)
        qms = _head_masked_queries(q_ref[rows, :])
        gq = gq_ref[rows, :]
        out = []
        for h in range(HEADS_PER_TILE):
            own = ((lane >= N_SPLIT * h) & (lane < N_SPLIT * (h + 1))) | \
                  ((lane >= n_aug + N_SPLIT * h) & (lane < n_aug + N_SPLIT * (h + 1)))
            out.append(jnp.concatenate([qms[h], jnp.where(own, gq, jnp.zeros_like(gq))], axis=1))
        return out

    def keys_at(row0, n):
        rows = pl.ds(row0, n)
        return jnp.concatenate([k_ref[rows, :], gk_ref[rows, :]], axis=1), vt_ref[:, rows]

    def run(update, init):
        qa = [qaug(r) for r in range(Q_SUB)]

        def body(j, flat):
            kaug, vt_blk = keys_at(pl.multiple_of(j * TKV, TKV), TKV)
            logits = [_dot_nt(kaug, qa[r][h]) for r, h in chains]
            new = [update(flat[2 * c:2 * c + 2], logits[c], _head_values(vt_blk, h))
                   for c, (r, h) in enumerate(chains)]
            return tuple(x for s in new for x in s)

        flat = lax.fori_loop(j_first, gi * KV_PER_G, body, tuple(x for _ in chains for x in init))

        kaug, vt_blk = keys_at(pl.multiple_of(gi * TG, TG), TG)
        s_loc = lax.broadcasted_iota(jnp.int32, (TQ, TQ), 0)
        t_loc = lax.broadcasted_iota(jnp.int32, (TQ, TQ), 1)
        logits = []
        for r, h in chains:
            a = _dot_nt(kaug[:(r + 1) * TQ], qa[r][h])
            diag = jnp.where(s_loc <= t_loc, a[r * TQ:], NEG)
            logits.append(diag if r == 0 else jnp.concatenate([a[:r * TQ], diag], axis=0))
        for r in range(Q_SUB):
            states = [update(flat[2 * c:2 * c + 2], logits[c], _head_values(vt_blk[:, :(r + 1) * TQ], h))
                      for c, (rr, h) in enumerate(chains) if rr == r]
            _finish(states, z_ref, o_ref, r)

    @pl.when(plain != 0)
    def _():
        run(_plain_update, (jnp.zeros((1, TQ), jnp.float32), jnp.zeros((ACC_ROWS, TQ), jnp.float32)))

    @pl.when(plain == 0)
    def _():
        run(_online_update, _init_state())


def _fox(sched, q, gq, k, gk, vt, z, batch, seq):
    ng, tok, seq_rows, seq_cols = _attn_specs(seq)
    return pl.pallas_call(
        _fox_kernel,
        grid_spec=pltpu.PrefetchScalarGridSpec(
            num_scalar_prefetch=1,
            grid=(batch, N_HEAD_PAIRS, ng),
            in_specs=[tok, tok, seq_rows, seq_rows, seq_cols, tok],
            out_specs=tok),
        out_shape=jax.ShapeDtypeStruct(q.shape, jnp.bfloat16),
        compiler_params=_params("arbitrary", "arbitrary", "arbitrary"),
        name="fox",
    )(sched, q, gq, k, gk, vt, z)


def _in_weights(w_in, w_gate=None):
    w_in = w_in.astype(jnp.float32)
    e = MIX_WIDTH
    cols = [w_in[:, :e] * (LOG2E / math.sqrt(HEAD_DIM)), w_in[:, e:2 * e], w_in[:, 3 * e:]]
    if w_gate is not None:
        cols.append(jnp.pad(w_gate.astype(jnp.float32), ((0, 0), (0, LANES - N_HEADS))))
    return jnp.concatenate(cols, axis=1).astype(jnp.bfloat16), w_in[:, 2 * e:3 * e].T.astype(jnp.bfloat16)


def kernel(x, w_in_a, rel_bias_a, w_out_a, w_in_b, w_f_b, b_f_b, w_out_b, ln_g, ln_b):
    batch, seq, d = x.shape
    assert seq % TG == 0 and TG % TKV == 0 and TKV == TM_PROJ and seq >= A_WINDOW
    xf = x.reshape(batch * seq, d).astype(jnp.float32)
    for i in range(DEPTH):
        j = i // N_MIXERS
        if i % N_MIXERS == 0:
            q, k, vt, z, nst = _in_proj(xf, *_in_weights(w_in_a[j]))
            plain = _mixer_a_schedule(*_max_norms(nst, batch), rel_bias_a[j])
            g = _mixer_a(plain, q, k, vt, z, _mixer_a_bias(rel_bias_a[j]), batch, seq)
            w_out = w_out_a[j]
        else:
            q, k, vt, z, nst, lf = _in_proj(xf, *_in_weights(w_in_b[j], w_f_b[j]))
            gk, gq, f_stats = _fox_prep(lf, b_f_b[j], batch, seq)
            g = _fox(_fox_schedule(*_max_norms(nst, batch), f_stats), q, gq, k, gk, vt, z, batch, seq)
            w_out = w_out_b[j]
        xf = _out_proj_ln(g, xf, w_out.astype(jnp.bfloat16), ln_g[i], ln_b[i])
    return xf.reshape(batch, seq, d).astype(x.dtype)
```

```python
import math

import jax
import jax.numpy as jnp
from jax import lax
from jax.experimental import pallas as pl
from jax.experimental.pallas import tpu as pltpu

D_MODEL = 1024
N_HEADS = 16
HEAD_DIM = 64
MIX_WIDTH = N_HEADS * HEAD_DIM
CHUNK = 64
N_PAST_CHUNKS = 8
MAX_REL = 128
DEPTH = 4
N_MIXERS = 2
LN_EPS = 1e-5
DEEPNORM_ALPHA = (2.0 * DEPTH) ** 0.25
LOG2E = math.log2(math.e)
NEG = -1e30

LANES = 128
HEADS_PER_TILE = LANES // HEAD_DIM
N_HEAD_PAIRS = N_HEADS // HEADS_PER_TILE
BF16_ROWS = 16
N_SPLIT = 3
TQ = 256
Q_SUB_A = 8
Q_SUB_B = 4
TKV = 512
TM_PROJ = 512
ISSUE_LAG = 8
VMEM_LIMIT = 56 * 1024 * 1024
PRUNE_LOG2 = 160.0
NORM_SLACK = 1.0 + 2.0 ** -6
PLAIN_EXP_LOG2 = 60.0


def _params(*semantics):
    return pltpu.CompilerParams(dimension_semantics=semantics, vmem_limit_bytes=VMEM_LIMIT)


def _dot_nt(a, b):
    return lax.dot_general(a, b, (((1,), (1,)), ((), ())), preferred_element_type=jnp.float32)


def _split(x, n):
    pieces = []
    for _ in range(n):
        p = x.astype(jnp.bfloat16)
        pieces.append(p)
        x = x - p.astype(jnp.float32)
    return pieces


N_STATS = 8


def _in_proj_kernel(x_ref, w_ref, wvt_ref, hsel_ref, q_ref, k_ref, vt_ref, z_ref, nst_ref, *gate_ref):
    xb = x_ref[...].astype(jnp.bfloat16)
    sq_max = []
    for g, o_ref in enumerate((q_ref, k_ref, z_ref)):
        o = jnp.dot(xb, w_ref[:, g * MIX_WIDTH:(g + 1) * MIX_WIDTH],
                    preferred_element_type=jnp.float32).astype(jnp.bfloat16)
        o_ref[...] = o
        if g < 2:
            of = o.astype(jnp.float32)
            sq = jnp.dot((of * of).astype(jnp.bfloat16), hsel_ref[...], preferred_element_type=jnp.float32)
            sq_max.append(jnp.max(sq, axis=0, keepdims=True))
    nst_ref[...] = jnp.concatenate(sq_max + [jnp.zeros((N_STATS - 2, LANES), jnp.float32)], axis=0)
    vt_ref[...] = _dot_nt(wvt_ref[...], xb).astype(jnp.bfloat16)
    if gate_ref:
        gate_ref[0][...] = jnp.dot(xb, w_ref[:, 3 * MIX_WIDTH:], preferred_element_type=jnp.float32)


def _in_proj(xf, w, wvt):
    m = xf.shape[0]
    n = w.shape[1]
    has_gate = n > 3 * MIX_WIDTH
    hsel = (jnp.arange(MIX_WIDTH)[:, None] // HEAD_DIM == jnp.arange(LANES)[None, :]).astype(jnp.bfloat16)
    row = pl.BlockSpec((TM_PROJ, MIX_WIDTH), lambda i: (i, 0))
    col = pl.BlockSpec((MIX_WIDTH, TM_PROJ), lambda i: (0, i))
    tok = jax.ShapeDtypeStruct((m, MIX_WIDTH), jnp.bfloat16)
    out_shape = [tok, tok, jax.ShapeDtypeStruct((MIX_WIDTH, m), jnp.bfloat16), tok,
                 jax.ShapeDtypeStruct((m // TM_PROJ, N_STATS, LANES), jnp.float32)]
    out_specs = [row, row, col, row, pl.BlockSpec((None, N_STATS, LANES), lambda i: (i, 0, 0))]
    if has_gate:
        out_shape = out_shape + [jax.ShapeDtypeStruct((m, LANES), jnp.float32)]
        out_specs = out_specs + [pl.BlockSpec((TM_PROJ, LANES), lambda i: (i, 0))]
    return pl.pallas_call(
        _in_proj_kernel,
        grid=(m // TM_PROJ,),
        in_specs=[pl.BlockSpec((TM_PROJ, D_MODEL), lambda i: (i, 0)),
                  pl.BlockSpec((D_MODEL, n), lambda i: (0, 0)),
                  pl.BlockSpec((MIX_WIDTH, D_MODEL), lambda i: (0, 0)),
                  pl.BlockSpec((MIX_WIDTH, LANES), lambda i: (0, 0))],
        out_specs=out_specs,
        out_shape=out_shape,
        compiler_params=_params("arbitrary"),
        name="in_proj_gate" if has_gate else "in_proj",
    )(xf, w, wvt, hsel)


def _max_norms(nst, batch):
    nb = nst.shape[0] // batch
    n = jnp.sqrt(nst[:, :2, :N_HEADS]).reshape(batch, nb, 2, N_HEADS) * NORM_SLACK
    return n[:, :, 0], n[:, :, 1]


def _lagged(n, issue, consume, lag):
    pending = {}
    for c in range(n + lag):
        if c < n:
            pending[c] = issue(c)
        if c >= lag:
            consume(c - lag, pending.pop(c - lag))


LN_ROW_CHUNKS = 2


def _out_proj_ln_kernel(g_ref, x_ref, w_ref, gam_ref, bet_ref, o_ref):
    rows = TM_PROJ // LN_ROW_CHUNKS

    def issue(c):
        return jnp.dot(g_ref[pl.ds(c * rows, rows), :], w_ref[...], preferred_element_type=jnp.float32)

    def consume(c, y):
        sl = pl.ds(c * rows, rows)
        r = DEEPNORM_ALPHA * x_ref[sl, :] + y
        mu = jnp.mean(r, axis=-1, keepdims=True)
        d = r - mu
        var = jnp.mean(d * d, axis=-1, keepdims=True)
        o_ref[sl, :] = d * lax.rsqrt(var + LN_EPS) * gam_ref[...] + bet_ref[...]

    _lagged(LN_ROW_CHUNKS, issue, consume, lag=1)


def _out_proj_ln(g, xf, w, gamma, beta):
    m = xf.shape[0]
    return pl.pallas_call(
        _out_proj_ln_kernel,
        grid=(m // TM_PROJ,),
        in_specs=[pl.BlockSpec((TM_PROJ, MIX_WIDTH), lambda i: (i, 0)),
                  pl.BlockSpec((TM_PROJ, D_MODEL), lambda i: (i, 0)),
                  pl.BlockSpec((MIX_WIDTH, D_MODEL), lambda i: (0, 0)),
                  pl.BlockSpec((1, D_MODEL), lambda i: (0, 0)),
                  pl.BlockSpec((1, D_MODEL), lambda i: (0, 0))],
        out_specs=pl.BlockSpec((TM_PROJ, D_MODEL), lambda i: (i, 0)),
        out_shape=jax.ShapeDtypeStruct((m, D_MODEL), jnp.float32),
        compiler_params=_params("arbitrary"),
        name="out_proj_ln",
    )(g, xf, w, gamma.reshape(1, D_MODEL), beta.reshape(1, D_MODEL))


ACC_ROWS = HEAD_DIM + BF16_ROWS
U_ROWS = 8


def _head_masked_queries(q2):
    lane = lax.broadcasted_iota(jnp.int32, q2.shape, 1)
    zero = jnp.zeros_like(q2)
    return [jnp.where((lane >= HEAD_DIM * h) & (lane < HEAD_DIM * (h + 1)), q2, zero)
            for h in range(HEADS_PER_TILE)]


def _head_values(vt_blk, h):
    ones = jnp.ones((BF16_ROWS, vt_blk.shape[1]), vt_blk.dtype)
    return jnp.concatenate([vt_blk[HEAD_DIM * h:HEAD_DIM * (h + 1), :], ones], axis=0)


def _online_update(u_ref, acc_ref, c, a, vt_h):
    u = u_ref[c, 0:1, :]
    u_new = jnp.maximum(u, jnp.max(a, axis=0, keepdims=True))
    p = jnp.exp2(a - u_new).astype(jnp.bfloat16)
    acc_ref[c] = jnp.exp2(u - u_new) * acc_ref[c] + jnp.dot(vt_h, p, preferred_element_type=jnp.float32)
    u_ref[c, 0:1, :] = u_new


def _plain_update(u_ref, acc_ref, c, a, vt_h):
    del u_ref
    acc_ref[c] = acc_ref[c] + jnp.dot(vt_h, jnp.exp2(a).astype(jnp.bfloat16),
                                      preferred_element_type=jnp.float32)


def _reset(u_ref, acc_ref):
    u_ref[...] = jnp.full(u_ref.shape, NEG, jnp.float32)
    acc_ref[...] = jnp.zeros(acc_ref.shape, jnp.float32)


def _finish(acc_ref, z_ref, o_ref, r):
    accs = [acc_ref[r * HEADS_PER_TILE + h] for h in range(HEADS_PER_TILE)]
    ot = jnp.concatenate([acc[:HEAD_DIM] / acc[HEAD_DIM:HEAD_DIM + 1] for acc in accs], axis=0)
    rows = pl.ds(r * TQ, TQ)
    z = z_ref[rows, :].astype(jnp.float32)
    o_ref[rows, :] = (ot.T * (z / (1.0 + jnp.exp(-z)))).astype(o_ref.dtype)


def _attn_specs(seq, q_sub):
    tg = q_sub * TQ
    ng = seq // tg
    tok = pl.BlockSpec((tg, LANES), lambda b, hp, i, *_: (b * ng + i, hp))
    seq_rows = pl.BlockSpec((seq, LANES), lambda b, hp, i, *_: (b, hp))
    seq_cols = pl.BlockSpec((LANES, seq), lambda b, hp, i, *_: (hp, b))
    return ng, tok, seq_rows, seq_cols


def _state_scratch(q_sub):
    n = q_sub * HEADS_PER_TILE
    return [pltpu.VMEM((n, U_ROWS, TQ), jnp.float32), pltpu.VMEM((n, ACC_ROWS, TQ), jnp.float32)]


A_PAST = N_PAST_CHUNKS * CHUNK
A_WINDOW = A_PAST + TQ
A_BIAS_ROWS = 2 * A_PAST + TQ
A_BAND_LO = A_PAST - MAX_REL
A_BAND_HI = A_PAST + TQ + CHUNK


def _mixer_a_kernel(plain_ref, q_ref, k_ref, vt_ref, z_ref, bias_ref, o_ref, u_ref, acc_ref):
    gi = pl.program_id(2)
    plain = plain_ref[pl.program_id(0) * N_HEAD_PAIRS + pl.program_id(1)]

    def run(update):
        _reset(u_ref, acc_ref)

        def window(r):
            start = (gi * Q_SUB_A + r) * TQ - A_PAST
            row0 = pl.multiple_of(jnp.maximum(start, 0), TQ)
            return row0, pl.multiple_of(row0 - start, TQ)

        def issue(c):
            r, h = divmod(c, HEADS_PER_TILE)
            row0, brow = window(r)
            qm = _head_masked_queries(q_ref[pl.ds(r * TQ, TQ), :])[h]
            return _dot_nt(k_ref[pl.ds(row0, A_WINDOW), :], qm) + bias_ref[h, pl.ds(brow, A_WINDOW), :]

        def consume(c, a):
            r, h = divmod(c, HEADS_PER_TILE)
            row0, _ = window(r)
            update(u_ref, acc_ref, c, a, _head_values(vt_ref[:, pl.ds(row0, A_WINDOW)], h))
            if h == HEADS_PER_TILE - 1:
                _finish(acc_ref, z_ref, o_ref, r)

        _lagged(Q_SUB_A * HEADS_PER_TILE, issue, consume, ISSUE_LAG)

    @pl.when(plain != 0)
    def _():
        run(_plain_update)

    @pl.when(plain == 0)
    def _():
        run(_online_update)


def _mixer_a_bias(rel_bias):
    rb = rel_bias.astype(jnp.float32) * LOG2E
    c = jnp.arange(A_BIAS_ROWS)[:, None]
    r = jnp.arange(TQ)[None, :]
    dist = r - (c - A_PAST)
    idx = jnp.clip(dist[A_BAND_LO:A_BAND_HI], -MAX_REL, MAX_REL) + MAX_REL
    band = jnp.transpose(rb.T[idx], (2, 0, 1))
    far = jnp.broadcast_to(rb[:, 2 * MAX_REL][:, None, None], (N_HEADS, A_BAND_LO, TQ))
    late = jnp.zeros((N_HEADS, A_BIAS_ROWS - A_BAND_HI, TQ), jnp.float32)
    table = jnp.concatenate([far, band, late], axis=1)
    dchunk = r // CHUNK - (c - A_PAST) // CHUNK
    visible = (dchunk >= 0) & (dchunk <= N_PAST_CHUNKS)
    return jnp.where(visible[None], table, NEG)


def _mixer_a_schedule(qn, kn, rel_bias):
    worst = jnp.max(qn, axis=1) * jnp.max(kn, axis=1) + jnp.max(jnp.abs(rel_bias), axis=-1)[None, :] * LOG2E
    ok = worst < PLAIN_EXP_LOG2
    return jnp.all(ok.reshape(-1, N_HEAD_PAIRS, HEADS_PER_TILE), axis=-1).reshape(-1).astype(jnp.int32)


def _mixer_a(plain, q, k, vt, z, bias, batch, seq):
    ng, tok, seq_rows, seq_cols = _attn_specs(seq, Q_SUB_A)
    return pl.pallas_call(
        _mixer_a_kernel,
        grid_spec=pltpu.PrefetchScalarGridSpec(
            num_scalar_prefetch=1,
            grid=(batch, N_HEAD_PAIRS, ng),
            in_specs=[tok, seq_rows, seq_cols, tok,
                      pl.BlockSpec((HEADS_PER_TILE, A_BIAS_ROWS, TQ), lambda b, hp, i, *_: (hp, 0, 0))],
            out_specs=tok,
            scratch_shapes=_state_scratch(Q_SUB_A)),
        out_shape=jax.ShapeDtypeStruct(q.shape, jnp.bfloat16),
        compiler_params=_params("arbitrary", "arbitrary", "arbitrary"),
        name="mixer_a",
    )(plain, q, k, vt, z, bias)


TG_B = Q_SUB_B * TQ
KV_PER_G = TG_B // TKV
N_AUG = HEADS_PER_TILE * N_SPLIT
AUG_STRIDE = LANES // N_HEAD_PAIRS


def _fox_prep_kernel(lf_ref, bf_ref, tri_ref, sel_ref, ones_ref, gk_ref, gq_ref, st_ref, carry_ref):
    @pl.when(pl.program_id(1) == 0)
    def _():
        carry_ref[...] = jnp.zeros_like(carry_ref)

    pre = lf_ref[...] + bf_ref[...]
    logf = (jnp.minimum(pre, 0.0) - jnp.log(1.0 + jnp.exp(-jnp.abs(pre)))) * LOG2E
    tri = tri_ref[...]
    csum = sum(jnp.dot(tri, piece, preferred_element_type=jnp.float32) for piece in _split(logf, N_SPLIT))
    f = csum + carry_ref[...]
    carry_ref[...] = f[TKV - 1:TKV, :]
    pieces = _split(-f, N_SPLIT)
    for side, o_ref in enumerate((gk_ref, gq_ref)):
        o_ref[...] = (sum(jnp.dot(piece, sel_ref[side, n], preferred_element_type=jnp.float32)
                          for n, piece in enumerate(pieces)) + ones_ref[side:side + 1, :]).astype(jnp.bfloat16)
    st_ref[...] = jnp.concatenate(
        [f[0:1, :], f[TKV - 1:TKV, :], jnp.zeros((N_STATS - 2, LANES), jnp.float32)], axis=0)


def _fox_prep(lf, b_f, batch, seq):
    nb = seq // TKV
    bf = jnp.pad(b_f.astype(jnp.float32), (0, LANES - N_HEADS)).reshape(1, LANES)
    tri = (jnp.arange(TKV)[:, None] >= jnp.arange(TKV)[None, :]).astype(jnp.bfloat16)
    src = jnp.arange(LANES)[:, None]
    dst = jnp.arange(LANES)[None, :]
    base = (src // HEADS_PER_TILE) * AUG_STRIDE + (src % HEADS_PER_TILE) * N_SPLIT
    sel = jnp.stack([
        jnp.stack([jnp.where((dst == base + side * N_AUG + n) & (src < N_HEADS), 1.0 - 2.0 * side, 0.0)
                   for n in range(N_SPLIT)]) for side in range(2)]).astype(jnp.bfloat16)
    lane = jnp.arange(LANES) % AUG_STRIDE
    ones = jnp.stack([(lane >= N_AUG) & (lane < 2 * N_AUG), lane < N_AUG]).astype(jnp.float32)
    tokspec = pl.BlockSpec((TKV, LANES), lambda b, i: (b * nb + i, 0))
    aug = jax.ShapeDtypeStruct((batch * seq, LANES), jnp.bfloat16)
    return pl.pallas_call(
        _fox_prep_kernel,
        grid=(batch, nb),
        in_specs=[tokspec,
                  pl.BlockSpec((1, LANES), lambda b, i: (0, 0)),
                  pl.BlockSpec((TKV, TKV), lambda b, i: (0, 0)),
                  pl.BlockSpec((2, N_SPLIT, LANES, LANES), lambda b, i: (0, 0, 0, 0)),
                  pl.BlockSpec((2, LANES), lambda b, i: (0, 0))],
        out_specs=[tokspec, tokspec, pl.BlockSpec((None, None, N_STATS, LANES), lambda b, i: (b, i, 0, 0))],
        out_shape=[aug, aug, jax.ShapeDtypeStruct((batch, nb, N_STATS, LANES), jnp.float32)],
        scratch_shapes=[pltpu.VMEM((1, LANES), jnp.float32)],
        compiler_params=_params("arbitrary", "arbitrary"),
        name="fox_prep",
    )(lf, bf, tri, sel, ones)


def _fox_schedule(qn, kn, f_stats):
    b, nb = f_stats.shape[0], f_stats.shape[1]
    ng = nb // KV_PER_G
    f_first, f_last = f_stats[:, :, 0, :N_HEADS], f_stats[:, :, 1, :N_HEADS]
    qn_g = jnp.max(qn.reshape(b, ng, KV_PER_G, N_HEADS), axis=2)
    kn_g = jnp.max(kn.reshape(b, ng, KV_PER_G, N_HEADS), axis=2)
    f_first_g = f_first.reshape(b, ng, KV_PER_G, N_HEADS)[:, :, 0]
    bound = (qn_g[:, :, None] * kn[:, None, :] + f_first_g[:, :, None] - f_last[:, None, :]
             + (qn_g * kn_g)[:, :, None])
    i_idx = jnp.arange(ng)[None, :, None, None]
    j_idx = jnp.arange(nb)[None, None, :, None]
    skip = (bound < -PRUNE_LOG2) & (j_idx < i_idx * KV_PER_G)
    first = jnp.min(jnp.where(skip, nb, j_idx), axis=2)
    first = jnp.min(first.reshape(b, ng, N_HEAD_PAIRS, HEADS_PER_TILE), axis=-1)
    first = jnp.transpose(first, (0, 2, 1)).reshape(-1)
    worst = jnp.max(qn, axis=1) * jnp.max(kn, axis=1)
    plain = jnp.all((worst < PLAIN_EXP_LOG2).reshape(b, N_HEAD_PAIRS, HEADS_PER_TILE), axis=-1).reshape(-1)
    return jnp.concatenate([first, plain.astype(first.dtype)]).astype(jnp.int32)


def _fox_kernel(sched_ref, q_ref, gq_ref, k_ref, gk_ref, vt_ref, z_ref, o_ref, u_ref, acc_ref):
    hp = pl.program_id(1)
    gi = pl.program_id(2)
    n_groups = pl.num_programs(2)
    pair = pl.program_id(0) * N_HEAD_PAIRS + hp
    j_first = sched_ref[pair * n_groups + gi]
    plain = sched_ref[pl.num_programs(0) * N_HEAD_PAIRS * n_groups + pair]
    n_chains = Q_SUB_B * HEADS_PER_TILE

    def qaug(c):
        r, h = divmod(c, HEADS_PER_TILE)
        rows = pl.ds(r * TQ, TQ)
        gq = gq_ref[rows, :]
        off = lax.broadcasted_iota(jnp.int32, (TQ, LANES), 1) - hp * AUG_STRIDE
        own = ((off >= N_SPLIT * h) & (off < N_SPLIT * (h + 1))) | \
              ((off >= N_AUG + N_SPLIT * h) & (off < N_AUG + N_SPLIT * (h + 1)))
        return jnp.concatenate([_head_masked_queries(q_ref[rows, :])[h],
                                jnp.where(own, gq, jnp.zeros_like(gq))], axis=1)

    def keys_at(row0, n):
        rows = pl.ds(row0, n)
        return jnp.concatenate([k_ref[rows, :], gk_ref[rows, :]], axis=1), vt_ref[:, rows]

    def run(update):
        _reset(u_ref, acc_ref)
        qa = [qaug(c) for c in range(n_chains)]

        def body(j, carry):
            kaug, vt_blk = keys_at(pl.multiple_of(j * TKV, TKV), TKV)
            _lagged(n_chains,
                    lambda c: _dot_nt(kaug, qa[c]),
                    lambda c, a: update(u_ref, acc_ref, c, a, _head_values(vt_blk, c % HEADS_PER_TILE)),
                    ISSUE_LAG)
            return carry

        lax.fori_loop(j_first, gi * KV_PER_G, body, 0)

        kaug, vt_blk = keys_at(pl.multiple_of(gi * TG_B, TG_B), TG_B)
        s_loc = lax.broadcasted_iota(jnp.int32, (TQ, TQ), 0)
        t_loc = lax.broadcasted_iota(jnp.int32, (TQ, TQ), 1)

        def issue(c):
            r = c // HEADS_PER_TILE
            a = _dot_nt(kaug[:(r + 1) * TQ], qa[c])
            diag = jnp.where(s_loc <= t_loc, a[r * TQ:], NEG)
            return diag if r == 0 else jnp.concatenate([a[:r * TQ], diag], axis=0)

        def consume(c, a):
            r, h = divmod(c, HEADS_PER_TILE)
            update(u_ref, acc_ref, c, a, _head_values(vt_blk[:, :(r + 1) * TQ], h))
            if h == HEADS_PER_TILE - 1:
                _finish(acc_ref, z_ref, o_ref, r)

        _lagged(n_chains, issue, consume, ISSUE_LAG)

    @pl.when(plain != 0)
    def _():
        run(_plain_update)

    @pl.when(plain == 0)
    def _():
        run(_online_update)


def _fox(sched, q, gq, k, gk, vt, z, batch, seq):
    ng, tok, seq_rows, seq_cols = _attn_specs(seq, Q_SUB_B)
    aug_tok = pl.BlockSpec((TG_B, LANES), lambda b, hp, i, *_: (b * ng + i, 0))
    aug_seq = pl.BlockSpec((seq, LANES), lambda b, hp, i, *_: (b, 0))
    return pl.pallas_call(
        _fox_kernel,
        grid_spec=pltpu.PrefetchScalarGridSpec(
            num_scalar_prefetch=1,
            grid=(batch, N_HEAD_PAIRS, ng),
            in_specs=[tok, aug_tok, seq_rows, aug_seq, seq_cols, tok],
            out_specs=tok,
            scratch_shapes=_state_scratch(Q_SUB_B)),
        out_shape=jax.ShapeDtypeStruct(q.shape, jnp.bfloat16),
        compiler_params=_params("arbitrary", "arbitrary", "arbitrary"),
        name="fox",
    )(sched, q, gq, k, gk, vt, z)


def _in_weights(w_in, w_gate=None):
    w_in = w_in.astype(jnp.float32)
    e = MIX_WIDTH
    cols = [w_in[:, :e] * (LOG2E / math.sqrt(HEAD_DIM)), w_in[:, e:2 * e], w_in[:, 3 * e:]]
    if w_gate is not None:
        cols.append(jnp.pad(w_gate.astype(jnp.float32), ((0, 0), (0, LANES - N_HEADS))))
    return jnp.concatenate(cols, axis=1).astype(jnp.bfloat16), w_in[:, 2 * e:3 * e].T.astype(jnp.bfloat16)


def kernel(x, w_in_a, rel_bias_a, w_out_a, w_in_b, w_f_b, b_f_b, w_out_b, ln_g, ln_b):
    batch, seq, d = x.shape
    assert seq % (Q_SUB_A * TQ) == 0 and seq % TG_B == 0 and TG_B % TKV == 0 and TKV == TM_PROJ
    assert seq >= A_WINDOW and N_HEAD_PAIRS * AUG_STRIDE == LANES and 2 * N_AUG <= AUG_STRIDE
    xf = x.reshape(batch * seq, d).astype(jnp.float32)
    for i in range(DEPTH):
        j = i // N_MIXERS
        if i % N_MIXERS == 0:
            q, k, vt, z, nst = _in_proj(xf, *_in_weights(w_in_a[j]))
            plain = _mixer_a_schedule(*_max_norms(nst, batch), rel_bias_a[j])
            g = _mixer_a(plain, q, k, vt, z, _mixer_a_bias(rel_bias_a[j]), batch, seq)
            w_out = w_out_a[j]
        else:
            q, k, vt, z, nst, lf = _in_proj(xf, *_in_weights(w_in_b[j], w_f_b[j]))
            gk, gq, f_stats = _fox_prep(lf, b_f_b[j], batch, seq)
            g = _fox(_fox_schedule(*_max_norms(nst, batch), f_stats), q, gq, k, gk, vt, z, batch, seq)
            w_out = w_out_b[j]
        xf = _out_proj_ln(g, xf, w_out.astype(jnp.bfloat16), ln_g[i], ln_b[i])
    return xf.reshape(batch, seq, d).astype(x.dtype)
```

```python
import functools
import math

import jax
import jax.numpy as jnp
from jax import lax
from jax.experimental import pallas as pl
from jax.experimental.pallas import tpu as pltpu

D_MODEL = 1024
N_HEADS = 16
HEAD_DIM = 64
MIX_WIDTH = N_HEADS * HEAD_DIM
CHUNK = 64
N_PAST_CHUNKS = 8
MAX_REL = 128
DEPTH = 4
N_MIXERS = 2
LN_EPS = 1e-5
DEEPNORM_ALPHA = (2.0 * DEPTH) ** 0.25
LOG2E = math.log2(math.e)
NEG = -1e30

LANES = 128
HEADS_PER_TILE = LANES // HEAD_DIM
N_HEAD_PAIRS = N_HEADS // HEADS_PER_TILE
BF16_ROWS = 16
N_SPLIT = 3
TQ = 256
Q_SUB_A = 8
Q_SUB_B = 4
TKV = 512
TM_PROJ = 512
ISSUE_LAG = 8
VMEM_LIMIT = 56 * 1024 * 1024
PRUNE_LOG2 = 160.0
NORM_SLACK = 1.0 + 2.0 ** -6
PLAIN_EXP_LOG2 = 60.0


def _params(*semantics):
    return pltpu.CompilerParams(dimension_semantics=semantics, vmem_limit_bytes=VMEM_LIMIT)


def _dot_nt(a, b):
    return lax.dot_general(a, b, (((1,), (1,)), ((), ())), preferred_element_type=jnp.float32)


def _split(x, n):
    pieces = []
    for _ in range(n):
        p = x.astype(jnp.bfloat16)
        pieces.append(p)
        x = x - p.astype(jnp.float32)
    return pieces


N_STATS = 8
LN_ROW_CHUNKS = 2


def _lagged(n, issue, consume, lag):
    pending = {}
    for c in range(n + lag):
        if c < n:
            pending[c] = issue(c)
        if c >= lag:
            consume(c - lag, pending.pop(c - lag))


def _out_proj_ln_rows(g_ref, x_ref, wo_ref, gam_ref, bet_ref, xo_ref):
    rows = TM_PROJ // LN_ROW_CHUNKS
    out = []

    def issue(c):
        return jnp.dot(g_ref[pl.ds(c * rows, rows), :], wo_ref[...], preferred_element_type=jnp.float32)

    def consume(c, y):
        sl = pl.ds(c * rows, rows)
        r = DEEPNORM_ALPHA * x_ref[sl, :] + y
        mu = jnp.mean(r, axis=-1, keepdims=True)
        d = r - mu
        var = jnp.mean(d * d, axis=-1, keepdims=True)
        xn = d * lax.rsqrt(var + LN_EPS) * gam_ref[...] + bet_ref[...]
        xo_ref[sl, :] = xn
        out.append(xn)

    _lagged(LN_ROW_CHUNKS, issue, consume, lag=1)
    return jnp.concatenate(out, axis=0)


def _in_proj_rows(x, w_ref, wvt_ref, hsel_ref, q_ref, k_ref, vt_ref, z_ref, nst_ref, gate_ref):
    xb = x.astype(jnp.bfloat16)
    sq_max = []
    for g, o_ref in enumerate((q_ref, k_ref, z_ref)):
        o = jnp.dot(xb, w_ref[:, g * MIX_WIDTH:(g + 1) * MIX_WIDTH],
                    preferred_element_type=jnp.float32).astype(jnp.bfloat16)
        o_ref[...] = o
        if g < 2:
            of = o.astype(jnp.float32)
            sq = jnp.dot((of * of).astype(jnp.bfloat16), hsel_ref[...], preferred_element_type=jnp.float32)
            sq_max.append(jnp.max(sq, axis=0, keepdims=True))
    nst_ref[...] = jnp.concatenate(sq_max + [jnp.zeros((N_STATS - 2, LANES), jnp.float32)], axis=0)
    vt_ref[...] = _dot_nt(wvt_ref[...], xb).astype(jnp.bfloat16)
    if gate_ref is not None:
        gate_ref[...] = jnp.dot(xb, w_ref[:, 3 * MIX_WIDTH:], preferred_element_type=jnp.float32)


def _proj_kernel(*refs, has_out, has_in, has_gate):
    refs = list(refs)
    n_in = (5 if has_out else 1) + (3 if has_in else 0)
    ins, outs = refs[:n_in], refs[n_in:]
    if has_out:
        x = _out_proj_ln_rows(*ins[:5], outs.pop(0))
        ins = ins[5:]
    else:
        x = ins.pop(0)[...]
    if has_in:
        _in_proj_rows(x, *ins, *outs[:5], outs[5] if has_gate else None)


def _proj(xf, prev=None, nxt=None):
    m = xf.shape[0]
    row_d = pl.BlockSpec((TM_PROJ, D_MODEL), lambda i: (i, 0))
    row_e = pl.BlockSpec((TM_PROJ, MIX_WIDTH), lambda i: (i, 0))

    def whole(shape):
        return pl.BlockSpec(shape, lambda i: (0,) * len(shape))

    args, in_specs, out_shape, out_specs = [], [], [], []
    if prev is not None:
        g, w_out, gamma, beta = prev
        args += [g, xf, w_out, gamma.reshape(1, D_MODEL), beta.reshape(1, D_MODEL)]
        in_specs += [row_e, row_d, whole((MIX_WIDTH, D_MODEL)), whole((1, D_MODEL)), whole((1, D_MODEL))]
        out_shape.append(jax.ShapeDtypeStruct((m, D_MODEL), jnp.float32))
        out_specs.append(row_d)
    else:
        args.append(xf)
        in_specs.append(row_d)
    has_gate = False
    if nxt is not None:
        w, wvt = nxt
        has_gate = w.shape[1] > 3 * MIX_WIDTH
        hsel = (jnp.arange(MIX_WIDTH)[:, None] // HEAD_DIM == jnp.arange(LANES)[None, :]).astype(jnp.bfloat16)
        args += [w, wvt, hsel]
        in_specs += [whole(w.shape), whole(wvt.shape), whole(hsel.shape)]
        tok = jax.ShapeDtypeStruct((m, MIX_WIDTH), jnp.bfloat16)
        out_shape += [tok, tok, jax.ShapeDtypeStruct((MIX_WIDTH, m), jnp.bfloat16), tok,
                      jax.ShapeDtypeStruct((m // TM_PROJ, N_STATS, LANES), jnp.float32)]
        out_specs += [row_e, row_e, pl.BlockSpec((MIX_WIDTH, TM_PROJ), lambda i: (0, i)), row_e,
                      pl.BlockSpec((None, N_STATS, LANES), lambda i: (i, 0, 0))]
        if has_gate:
            out_shape.append(jax.ShapeDtypeStruct((m, LANES), jnp.float32))
            out_specs.append(pl.BlockSpec((TM_PROJ, LANES), lambda i: (i, 0)))
    name = (("out_" if prev is not None else "") + ("in_" if nxt is not None else "") + "proj"
            + ("_gate" if has_gate else ""))
    return pl.pallas_call(
        functools.partial(_proj_kernel, has_out=prev is not None, has_in=nxt is not None, has_gate=has_gate),
        grid=(m // TM_PROJ,),
        in_specs=in_specs,
        out_specs=out_specs,
        out_shape=out_shape,
        compiler_params=_params("arbitrary"),
        name=name,
    )(*args)


def _max_norms(nst, batch):
    nb = nst.shape[0] // batch
    n = jnp.sqrt(nst[:, :2, :N_HEADS]).reshape(batch, nb, 2, N_HEADS) * NORM_SLACK
    return n[:, :, 0], n[:, :, 1]


ACC_ROWS = HEAD_DIM + BF16_ROWS
U_ROWS = 8


def _head_masked_queries(q2):
    lane = lax.broadcasted_iota(jnp.int32, q2.shape, 1)
    zero = jnp.zeros_like(q2)
    return [jnp.where((lane >= HEAD_DIM * h) & (lane < HEAD_DIM * (h + 1)), q2, zero)
            for h in range(HEADS_PER_TILE)]


def _head_values(vt_blk, h):
    ones = jnp.ones((BF16_ROWS, vt_blk.shape[1]), vt_blk.dtype)
    return jnp.concatenate([vt_blk[HEAD_DIM * h:HEAD_DIM * (h + 1), :], ones], axis=0)


def _online_update(u_ref, acc_ref, c, a, vt_h):
    u = u_ref[c, 0:1, :]
    u_new = jnp.maximum(u, jnp.max(a, axis=0, keepdims=True))
    p = jnp.exp2(a - u_new).astype(jnp.bfloat16)
    acc_ref[c] = jnp.exp2(u - u_new) * acc_ref[c] + jnp.dot(vt_h, p, preferred_element_type=jnp.float32)
    u_ref[c, 0:1, :] = u_new


def _plain_update(u_ref, acc_ref, c, a, vt_h):
    del u_ref
    acc_ref[c] = acc_ref[c] + jnp.dot(vt_h, jnp.exp2(a).astype(jnp.bfloat16),
                                      preferred_element_type=jnp.float32)


def _reset(u_ref, acc_ref):
    u_ref[...] = jnp.full(u_ref.shape, NEG, jnp.float32)
    acc_ref[...] = jnp.zeros(acc_ref.shape, jnp.float32)


def _finish(acc_ref, z_ref, o_ref, r):
    accs = [acc_ref[r * HEADS_PER_TILE + h] for h in range(HEADS_PER_TILE)]
    ot = jnp.concatenate([acc[:HEAD_DIM] / acc[HEAD_DIM:HEAD_DIM + 1] for acc in accs], axis=0)
    rows = pl.ds(r * TQ, TQ)
    z = z_ref[rows, :].astype(jnp.float32)
    o_ref[rows, :] = (ot.T * (z / (1.0 + jnp.exp(-z)))).astype(o_ref.dtype)


def _attn_specs(seq, q_sub):
    tg = q_sub * TQ
    ng = seq // tg
    tok = pl.BlockSpec((tg, LANES), lambda b, hp, i, *_: (b * ng + i, hp))
    seq_rows = pl.BlockSpec((seq, LANES), lambda b, hp, i, *_: (b, hp))
    seq_cols = pl.BlockSpec((LANES, seq), lambda b, hp, i, *_: (hp, b))
    return ng, tok, seq_rows, seq_cols


def _state_scratch(q_sub):
    n = q_sub * HEADS_PER_TILE
    return [pltpu.VMEM((n, U_ROWS, TQ), jnp.float32), pltpu.VMEM((n, ACC_ROWS, TQ), jnp.float32)]


A_PAST = N_PAST_CHUNKS * CHUNK
A_WINDOW = A_PAST + TQ
A_BIAS_ROWS = 2 * A_PAST + TQ
A_BAND_LO = A_PAST - MAX_REL
A_BAND_HI = A_PAST + TQ + CHUNK


def _mixer_a_kernel(plain_ref, q_ref, k_ref, vt_ref, z_ref, bias_ref, o_ref, u_ref, acc_ref):
    gi = pl.program_id(2)
    plain = plain_ref[pl.program_id(0) * N_HEAD_PAIRS + pl.program_id(1)]

    def run(update):
        _reset(u_ref, acc_ref)

        def window(r):
            start = (gi * Q_SUB_A + r) * TQ - A_PAST
            row0 = pl.multiple_of(jnp.maximum(start, 0), TQ)
            return row0, pl.multiple_of(row0 - start, TQ)

        def issue(c):
            r, h = divmod(c, HEADS_PER_TILE)
            row0, brow = window(r)
            qm = _head_masked_queries(q_ref[pl.ds(r * TQ, TQ), :])[h]
            return _dot_nt(k_ref[pl.ds(row0, A_WINDOW), :], qm) + bias_ref[h, pl.ds(brow, A_WINDOW), :]

        def consume(c, a):
            r, h = divmod(c, HEADS_PER_TILE)
            row0, _ = window(r)
            update(u_ref, acc_ref, c, a, _head_values(vt_ref[:, pl.ds(row0, A_WINDOW)], h))
            if h == HEADS_PER_TILE - 1:
                _finish(acc_ref, z_ref, o_ref, r)

        _lagged(Q_SUB_A * HEADS_PER_TILE, issue, consume, ISSUE_LAG)

    @pl.when(plain != 0)
    def _():
        run(_plain_update)

    @pl.when(plain == 0)
    def _():
        run(_online_update)


A_BAND = A_BAND_HI - A_BAND_LO
A_STRIP = 1024
A_STRIP_OFF = 512


def _bias_band_kernel(s_ref, o_ref):
    x = jnp.broadcast_to(s_ref[0], (A_BAND, A_STRIP))
    o_ref[0] = pltpu.roll(x, 0, 1, stride=1, stride_axis=0)[:, A_STRIP_OFF:A_STRIP_OFF + TQ]


def _mixer_a_bias(rel_bias):
    assert A_STRIP_OFF >= A_BAND - 1 and A_STRIP_OFF + TQ <= A_STRIP
    rb = rel_bias.astype(jnp.float32) * LOG2E
    c = jnp.arange(A_BIAS_ROWS)[:, None]
    r = jnp.arange(TQ)[None, :]
    strip_dist = jnp.arange(A_STRIP) - A_STRIP_OFF + (A_PAST - A_BAND_LO)
    strip = rb[:, jnp.clip(strip_dist, -MAX_REL, MAX_REL) + MAX_REL].reshape(N_HEADS, 1, A_STRIP)
    band = pl.pallas_call(
        _bias_band_kernel,
        grid=(N_HEADS,),
        in_specs=[pl.BlockSpec((1, 1, A_STRIP), lambda h: (h, 0, 0))],
        out_specs=pl.BlockSpec((1, A_BAND, TQ), lambda h: (h, 0, 0)),
        out_shape=jax.ShapeDtypeStruct((N_HEADS, A_BAND, TQ), jnp.float32),
        compiler_params=_params("arbitrary"),
        name="bias_band",
    )(strip)
    far = jnp.broadcast_to(rb[:, 2 * MAX_REL][:, None, None], (N_HEADS, A_BAND_LO, TQ))
    late = jnp.zeros((N_HEADS, A_BIAS_ROWS - A_BAND_HI, TQ), jnp.float32)
    table = jnp.concatenate([far, band, late], axis=1)
    dchunk = r // CHUNK - (c - A_PAST) // CHUNK
    visible = (dchunk >= 0) & (dchunk <= N_PAST_CHUNKS)
    return jnp.where(visible[None], table, NEG)


def _mixer_a_schedule(qn, kn, rel_bias):
    worst = jnp.max(qn, axis=1) * jnp.max(kn, axis=1) + jnp.max(jnp.abs(rel_bias), axis=-1)[None, :] * LOG2E
    ok = worst < PLAIN_EXP_LOG2
    return jnp.all(ok.reshape(-1, N_HEAD_PAIRS, HEADS_PER_TILE), axis=-1).reshape(-1).astype(jnp.int32)


def _mixer_a(plain, q, k, vt, z, bias, batch, seq):
    ng, tok, seq_rows, seq_cols = _attn_specs(seq, Q_SUB_A)
    return pl.pallas_call(
        _mixer_a_kernel,
        grid_spec=pltpu.PrefetchScalarGridSpec(
            num_scalar_prefetch=1,
            grid=(batch, N_HEAD_PAIRS, ng),
            in_specs=[tok, seq_rows, seq_cols, tok,
                      pl.BlockSpec((HEADS_PER_TILE, A_BIAS_ROWS, TQ), lambda b, hp, i, *_: (hp, 0, 0))],
            out_specs=tok,
            scratch_shapes=_state_scratch(Q_SUB_A)),
        out_shape=jax.ShapeDtypeStruct(q.shape, jnp.bfloat16),
        compiler_params=_params("arbitrary", "arbitrary", "arbitrary"),
        name="mixer_a",
    )(plain, q, k, vt, z, bias)


TG_B = Q_SUB_B * TQ
KV_PER_G = TG_B // TKV
N_AUG = HEADS_PER_TILE * N_SPLIT
AUG_STRIDE = LANES // N_HEAD_PAIRS


def _fox_prep_kernel(lf_ref, bf_ref, tri_ref, sel_ref, ones_ref, gk_ref, gq_ref, st_ref, carry_ref):
    @pl.when(pl.program_id(1) == 0)
    def _():
        carry_ref[...] = jnp.zeros_like(carry_ref)

    pre = lf_ref[...] + bf_ref[...]
    logf = (jnp.minimum(pre, 0.0) - jnp.log(1.0 + jnp.exp(-jnp.abs(pre)))) * LOG2E
    tri = tri_ref[...]
    csum = sum(jnp.dot(tri, piece, preferred_element_type=jnp.float32) for piece in _split(logf, N_SPLIT))
    f = csum + carry_ref[...]
    carry_ref[...] = f[TKV - 1:TKV, :]
    pieces = _split(-f, N_SPLIT)
    for side, o_ref in enumerate((gk_ref, gq_ref)):
        o_ref[...] = (sum(jnp.dot(piece, sel_ref[side, n], preferred_element_type=jnp.float32)
                          for n, piece in enumerate(pieces)) + ones_ref[side:side + 1, :]).astype(jnp.bfloat16)
    st_ref[...] = jnp.concatenate(
        [f[0:1, :], f[TKV - 1:TKV, :], jnp.zeros((N_STATS - 2, LANES), jnp.float32)], axis=0)


def _fox_prep(lf, b_f, batch, seq):
    nb = seq // TKV
    bf = jnp.pad(b_f.astype(jnp.float32), (0, LANES - N_HEADS)).reshape(1, LANES)
    tri = (jnp.arange(TKV)[:, None] >= jnp.arange(TKV)[None, :]).astype(jnp.bfloat16)
    src = jnp.arange(LANES)[:, None]
    dst = jnp.arange(LANES)[None, :]
    base = (src // HEADS_PER_TILE) * AUG_STRIDE + (src % HEADS_PER_TILE) * N_SPLIT
    sel = jnp.stack([
        jnp.stack([jnp.where((dst == base + side * N_AUG + n) & (src < N_HEADS), 1.0 - 2.0 * side, 0.0)
                   for n in range(N_SPLIT)]) for side in range(2)]).astype(jnp.bfloat16)
    lane = jnp.arange(LANES) % AUG_STRIDE
    ones = jnp.stack([(lane >= N_AUG) & (lane < 2 * N_AUG), lane < N_AUG]).astype(jnp.float32)
    tokspec = pl.BlockSpec((TKV, LANES), lambda b, i: (b * nb + i, 0))
    aug = jax.ShapeDtypeStruct((batch * seq, LANES), jnp.bfloat16)
    return pl.pallas_call(
        _fox_prep_kernel,
        grid=(batch, nb),
        in_specs=[tokspec,
                  pl.BlockSpec((1, LANES), lambda b, i: (0, 0)),
                  pl.BlockSpec((TKV, TKV), lambda b, i: (0, 0)),
                  pl.BlockSpec((2, N_SPLIT, LANES, LANES), lambda b, i: (0, 0, 0, 0)),
                  pl.BlockSpec((2, LANES), lambda b, i: (0, 0))],
        out_specs=[tokspec, tokspec, pl.BlockSpec((None, None, N_STATS, LANES), lambda b, i: (b, i, 0, 0))],
        out_shape=[aug, aug, jax.ShapeDtypeStruct((batch, nb, N_STATS, LANES), jnp.float32)],
        scratch_shapes=[pltpu.VMEM((1, LANES), jnp.float32)],
        compiler_params=_params("arbitrary", "arbitrary"),
        name="fox_prep",
    )(lf, bf, tri, sel, ones)


def _fox_schedule(qn, kn, f_stats):
    b, nb = f_stats.shape[0], f_stats.shape[1]
    ng = nb // KV_PER_G
    f_first, f_last = f_stats[:, :, 0, :N_HEADS], f_stats[:, :, 1, :N_HEADS]
    qn_g = jnp.max(qn.reshape(b, ng, KV_PER_G, N_HEADS), axis=2)
    kn_g = jnp.max(kn.reshape(b, ng, KV_PER_G, N_HEADS), axis=2)
    f_first_g = f_first.reshape(b, ng, KV_PER_G, N_HEADS)[:, :, 0]
    bound = (qn_g[:, :, None] * kn[:, None, :] + f_first_g[:, :, None] - f_last[:, None, :]
             + (qn_g * kn_g)[:, :, None])
    i_idx = jnp.arange(ng)[None, :, None, None]
    j_idx = jnp.arange(nb)[None, None, :, None]
    skip = (bound < -PRUNE_LOG2) & (j_idx < i_idx * KV_PER_G)
    first = jnp.min(jnp.where(skip, nb, j_idx), axis=2)
    first = jnp.min(first.reshape(b, ng, N_HEAD_PAIRS, HEADS_PER_TILE), axis=-1)
    first = jnp.transpose(first, (0, 2, 1)).reshape(-1)
    worst = jnp.max(qn, axis=1) * jnp.max(kn, axis=1)
    plain = jnp.all((worst < PLAIN_EXP_LOG2).reshape(b, N_HEAD_PAIRS, HEADS_PER_TILE), axis=-1).reshape(-1)
    return jnp.concatenate([first, plain.astype(first.dtype)]).astype(jnp.int32)


def _fox_kernel(sched_ref, q_ref, gq_ref, k_ref, gk_ref, vt_ref, z_ref, o_ref, u_ref, acc_ref):
    hp = pl.program_id(1)
    gi = pl.program_id(2)
    n_groups = pl.num_programs(2)
    pair = pl.program_id(0) * N_HEAD_PAIRS + hp
    j_first = sched_ref[pair * n_groups + gi]
    plain = sched_ref[pl.num_programs(0) * N_HEAD_PAIRS * n_groups + pair]
    n_chains = Q_SUB_B * HEADS_PER_TILE

    def qaug(c):
        r, h = divmod(c, HEADS_PER_TILE)
        rows = pl.ds(r * TQ, TQ)
        gq = gq_ref[rows, :]
        off = lax.broadcasted_iota(jnp.int32, (TQ, LANES), 1) - hp * AUG_STRIDE
        own = ((off >= N_SPLIT * h) & (off < N_SPLIT * (h + 1))) | \
              ((off >= N_AUG + N_SPLIT * h) & (off < N_AUG + N_SPLIT * (h + 1)))
        return jnp.concatenate([_head_masked_queries(q_ref[rows, :])[h],
                                jnp.where(own, gq, jnp.zeros_like(gq))], axis=1)

    def keys_at(row0, n):
        rows = pl.ds(row0, n)
        return jnp.concatenate([k_ref[rows, :], gk_ref[rows, :]], axis=1), vt_ref[:, rows]

    def run(update):
        _reset(u_ref, acc_ref)
        qa = [qaug(c) for c in range(n_chains)]

        def body(j, carry):
            kaug, vt_blk = keys_at(pl.multiple_of(j * TKV, TKV), TKV)
            _lagged(n_chains,
                    lambda c: _dot_nt(kaug, qa[c]),
                    lambda c, a: update(u_ref, acc_ref, c, a, _head_values(vt_blk, c % HEADS_PER_TILE)),
                    ISSUE_LAG)
            return carry

        lax.fori_loop(j_first, gi * KV_PER_G, body, 0)

        kaug, vt_blk = keys_at(pl.multiple_of(gi * TG_B, TG_B), TG_B)
        s_loc = lax.broadcasted_iota(jnp.int32, (TQ, TQ), 0)
        t_loc = lax.broadcasted_iota(jnp.int32, (TQ, TQ), 1)

        def issue(c):
            r = c // HEADS_PER_TILE
            a = _dot_nt(kaug[:(r + 1) * TQ], qa[c])
            diag = jnp.where(s_loc <= t_loc, a[r * TQ:], NEG)
            return diag if r == 0 else jnp.concatenate([a[:r * TQ], diag], axis=0)

        def consume(c, a):
            r, h = divmod(c, HEADS_PER_TILE)
            update(u_ref, acc_ref, c, a, _head_values(vt_blk[:, :(r + 1) * TQ], h))
            if h == HEADS_PER_TILE - 1:
                _finish(acc_ref, z_ref, o_ref, r)

        _lagged(n_chains, issue, consume, ISSUE_LAG)

    @pl.when(plain != 0)
    def _():
        run(_plain_update)

    @pl.when(plain == 0)
    def _():
        run(_online_update)


def _fox(sched, q, gq, k, gk, vt, z, batch, seq):
    ng, tok, seq_rows, seq_cols = _attn_specs(seq, Q_SUB_B)
    aug_tok = pl.BlockSpec((TG_B, LANES), lambda b, hp, i, *_: (b * ng + i, 0))
    aug_seq = pl.BlockSpec((seq, LANES), lambda b, hp, i, *_: (b, 0))
    return pl.pallas_call(
        _fox_kernel,
        grid_spec=pltpu.PrefetchScalarGridSpec(
            num_scalar_prefetch=1,
            grid=(batch, N_HEAD_PAIRS, ng),
            in_specs=[tok, aug_tok, seq_rows, aug_seq, seq_cols, tok],
            out_specs=tok,
            scratch_shapes=_state_scratch(Q_SUB_B)),
        out_shape=jax.ShapeDtypeStruct(q.shape, jnp.bfloat16),
        compiler_params=_params("arbitrary", "arbitrary", "arbitrary"),
        name="fox",
    )(sched, q, gq, k, gk, vt, z)


def _in_weights(w_in, w_gate=None):
    w_in = w_in.astype(jnp.float32)
    e = MIX_WIDTH
    cols = [w_in[:, :e] * (LOG2E / math.sqrt(HEAD_DIM)), w_in[:, e:2 * e], w_in[:, 3 * e:]]
    if w_gate is not None:
        cols.append(jnp.pad(w_gate.astype(jnp.float32), ((0, 0), (0, LANES - N_HEADS))))
    return jnp.concatenate(cols, axis=1).astype(jnp.bfloat16), w_in[:, 2 * e:3 * e].T.astype(jnp.bfloat16)


def _reorder_heads(order, w_in, w_f, b_f, w_out):
    d = w_in.shape[0]
    w_in = w_in.reshape(d, 4, N_HEADS, HEAD_DIM)[:, :, order].reshape(d, 4 * MIX_WIDTH)
    w_out = w_out.reshape(N_HEADS, HEAD_DIM, -1)[order].reshape(MIX_WIDTH, -1)
    return w_in, w_f[:, order], b_f[order], w_out


def kernel(x, w_in_a, rel_bias_a, w_out_a, w_in_b, w_f_b, b_f_b, w_out_b, ln_g, ln_b):
    batch, seq, d = x.shape
    assert seq % (Q_SUB_A * TQ) == 0 and seq % TG_B == 0 and TG_B % TKV == 0 and TKV == TM_PROJ
    assert seq >= A_WINDOW and N_HEAD_PAIRS * AUG_STRIDE == LANES and 2 * N_AUG <= AUG_STRIDE
    xf = x.reshape(batch * seq, d).astype(jnp.float32)

    layers = []
    for i in range(DEPTH):
        j = i // N_MIXERS
        if i % N_MIXERS == 0:
            layers.append((_in_weights(w_in_a[j]), w_out_a[j].astype(jnp.bfloat16), None))
        else:
            order = jnp.argsort(b_f_b[j])
            w_in, w_f, b_f, w_out = _reorder_heads(order, w_in_b[j], w_f_b[j], b_f_b[j], w_out_b[j])
            layers.append((_in_weights(w_in, w_f), w_out.astype(jnp.bfloat16), b_f))

    outs = _proj(xf, nxt=layers[0][0])
    for i in range(DEPTH):
        _, w_out, b_f = layers[i]
        j = i // N_MIXERS
        if b_f is None:
            q, k, vt, z, nst = outs
            plain = _mixer_a_schedule(*_max_norms(nst, batch), rel_bias_a[j])
            g = _mixer_a(plain, q, k, vt, z, _mixer_a_bias(rel_bias_a[j]), batch, seq)
        else:
            q, k, vt, z, nst, lf = outs
            gk, gq, f_stats = _fox_prep(lf, b_f, batch, seq)
            g = _fox(_fox_schedule(*_max_norms(nst, batch), f_stats), q, gq, k, gk, vt, z, batch, seq)
        xf, *outs = _proj(xf, prev=(g, w_out, ln_g[i], ln_b[i]),
                          nxt=layers[i + 1][0] if i + 1 < DEPTH else None)
    return xf.reshape(batch, seq, d).astype(x.dtype)
```

```python
import functools
import math

import jax
import jax.numpy as jnp
from jax import lax
from jax.experimental import pallas as pl
from jax.experimental.pallas import tpu as pltpu

D_MODEL = 1024
N_HEADS = 16
HEAD_DIM = 64
MIX_WIDTH = N_HEADS * HEAD_DIM
CHUNK = 64
N_PAST_CHUNKS = 8
MAX_REL = 128
DEPTH = 4
N_MIXERS = 2
LN_EPS = 1e-5
DEEPNORM_ALPHA = (2.0 * DEPTH) ** 0.25
LOG2E = math.log2(math.e)
NEG = -1e30

LANES = 128
HEADS_PER_TILE = LANES // HEAD_DIM
N_HEAD_PAIRS = N_HEADS // HEADS_PER_TILE
BF16_ROWS = 16
N_SPLIT = 3
TQ = 256
Q_SUB_A = 8
Q_SUB_B = 4
TKV = 512
TM_PROJ = 512
ISSUE_LAG = 8
VMEM_LIMIT = 56 * 1024 * 1024
PRUNE_LOG2 = 160.0
NORM_SLACK = 1.0 + 2.0 ** -6
PLAIN_EXP_LOG2 = 60.0


def _params(*semantics):
    return pltpu.CompilerParams(dimension_semantics=semantics, vmem_limit_bytes=VMEM_LIMIT)


def _dot_nt(a, b):
    return lax.dot_general(a, b, (((1,), (1,)), ((), ())), preferred_element_type=jnp.float32)


def _split(x, n):
    pieces = []
    for _ in range(n):
        p = x.astype(jnp.bfloat16)
        pieces.append(p)
        x = x - p.astype(jnp.float32)
    return pieces


N_STATS = 8
LN_ROW_CHUNKS = 2


def _lagged(n, issue, consume, lag):
    pending = {}
    for c in range(n + lag):
        if c < n:
            pending[c] = issue(c)
        if c >= lag:
            consume(c - lag, pending.pop(c - lag))


def _out_proj_ln_rows(g_ref, x_ref, wo_ref, gam_ref, bet_ref, xo_ref):
    rows = TM_PROJ // LN_ROW_CHUNKS
    out = []

    def issue(c):
        return jnp.dot(g_ref[pl.ds(c * rows, rows), :], wo_ref[...], preferred_element_type=jnp.float32)

    def consume(c, y):
        sl = pl.ds(c * rows, rows)
        r = DEEPNORM_ALPHA * x_ref[sl, :] + y
        mu = jnp.mean(r, axis=-1, keepdims=True)
        d = r - mu
        var = jnp.mean(d * d, axis=-1, keepdims=True)
        xn = d * lax.rsqrt(var + LN_EPS) * gam_ref[...] + bet_ref[...]
        xo_ref[sl, :] = xn
        out.append(xn)

    _lagged(LN_ROW_CHUNKS, issue, consume, lag=1)
    return jnp.concatenate(out, axis=0)


def _in_proj_rows(x, w_ref, wvt_ref, hsel_ref, q_ref, k_ref, vt_ref, z_ref, nst_ref, gate_ref):
    xb = x.astype(jnp.bfloat16)
    sq_max = []
    for g, o_ref in enumerate((q_ref, k_ref, z_ref)):
        o = jnp.dot(xb, w_ref[:, g * MIX_WIDTH:(g + 1) * MIX_WIDTH],
                    preferred_element_type=jnp.float32).astype(jnp.bfloat16)
        o_ref[...] = o
        if g < 2:
            of = o.astype(jnp.float32)
            sq = jnp.dot((of * of).astype(jnp.bfloat16), hsel_ref[...], preferred_element_type=jnp.float32)
            sq_max.append(jnp.max(sq, axis=0, keepdims=True))
    nst_ref[...] = jnp.concatenate(sq_max + [jnp.zeros((N_STATS - 2, LANES), jnp.float32)], axis=0)
    vt_ref[...] = _dot_nt(wvt_ref[...], xb).astype(jnp.bfloat16)
    if gate_ref is not None:
        gate_ref[...] = jnp.dot(xb, w_ref[:, 3 * MIX_WIDTH:], preferred_element_type=jnp.float32)


def _proj_kernel(*refs, has_out, has_in, has_gate):
    refs = list(refs)
    n_in = (5 if has_out else 1) + (3 if has_in else 0)
    ins, outs = refs[:n_in], refs[n_in:]
    if has_out:
        x = _out_proj_ln_rows(*ins[:5], outs.pop(0))
        ins = ins[5:]
    else:
        x = ins.pop(0)[...]
    if has_in:
        _in_proj_rows(x, *ins, *outs[:5], outs[5] if has_gate else None)


def _proj(xf, prev=None, nxt=None):
    m = xf.shape[0]
    row_d = pl.BlockSpec((TM_PROJ, D_MODEL), lambda i: (i, 0))
    row_e = pl.BlockSpec((TM_PROJ, MIX_WIDTH), lambda i: (i, 0))

    def whole(shape):
        return pl.BlockSpec(shape, lambda i: (0,) * len(shape))

    args, in_specs, out_shape, out_specs = [], [], [], []
    if prev is not None:
        g, w_out, gamma, beta = prev
        args += [g, xf, w_out, gamma.reshape(1, D_MODEL), beta.reshape(1, D_MODEL)]
        in_specs += [row_e, row_d, whole((MIX_WIDTH, D_MODEL)), whole((1, D_MODEL)), whole((1, D_MODEL))]
        out_shape.append(jax.ShapeDtypeStruct((m, D_MODEL), jnp.float32))
        out_specs.append(row_d)
    else:
        args.append(xf)
        in_specs.append(row_d)
    has_gate = False
    if nxt is not None:
        w, wvt = nxt
        has_gate = w.shape[1] > 3 * MIX_WIDTH
        hsel = (jnp.arange(MIX_WIDTH)[:, None] // HEAD_DIM == jnp.arange(LANES)[None, :]).astype(jnp.bfloat16)
        args += [w, wvt, hsel]
        in_specs += [whole(w.shape), whole(wvt.shape), whole(hsel.shape)]
        tok = jax.ShapeDtypeStruct((m, MIX_WIDTH), jnp.bfloat16)
        out_shape += [tok, tok, jax.ShapeDtypeStruct((MIX_WIDTH, m), jnp.bfloat16), tok,
                      jax.ShapeDtypeStruct((m // TM_PROJ, N_STATS, LANES), jnp.float32)]
        out_specs += [row_e, row_e, pl.BlockSpec((MIX_WIDTH, TM_PROJ), lambda i: (0, i)), row_e,
                      pl.BlockSpec((None, N_STATS, LANES), lambda i: (i, 0, 0))]
        if has_gate:
            out_shape.append(jax.ShapeDtypeStruct((m, LANES), jnp.float32))
            out_specs.append(pl.BlockSpec((TM_PROJ, LANES), lambda i: (i, 0)))
    name = (("out_" if prev is not None else "") + ("in_" if nxt is not None else "") + "proj"
            + ("_gate" if has_gate else ""))
    return pl.pallas_call(
        functools.partial(_proj_kernel, has_out=prev is not None, has_in=nxt is not None, has_gate=has_gate),
        grid=(m // TM_PROJ,),
        in_specs=in_specs,
        out_specs=out_specs,
        out_shape=out_shape,
        compiler_params=_params("arbitrary"),
        name=name,
    )(*args)


def _max_norms(nst, batch):
    nb = nst.shape[0] // batch
    n = jnp.sqrt(nst[:, :2, :N_HEADS]).reshape(batch, nb, 2, N_HEADS) * NORM_SLACK
    return n[:, :, 0], n[:, :, 1]


ACC_ROWS = HEAD_DIM + BF16_ROWS
U_ROWS = 8


def _head_masked_queries(q2):
    lane = lax.broadcasted_iota(jnp.int32, q2.shape, 1)
    zero = jnp.zeros_like(q2)
    return [jnp.where((lane >= HEAD_DIM * h) & (lane < HEAD_DIM * (h + 1)), q2, zero)
            for h in range(HEADS_PER_TILE)]


def _head_values(vt_blk, h):
    ones = jnp.ones((BF16_ROWS, vt_blk.shape[1]), vt_blk.dtype)
    return jnp.concatenate([vt_blk[HEAD_DIM * h:HEAD_DIM * (h + 1), :], ones], axis=0)


def _online_update(u_ref, acc_ref, c, a, vt_h):
    u = u_ref[c, 0:1, :]
    u_new = jnp.maximum(u, jnp.max(a, axis=0, keepdims=True))
    p = jnp.exp2(a - u_new).astype(jnp.bfloat16)
    acc_ref[c] = jnp.exp2(u - u_new) * acc_ref[c] + jnp.dot(vt_h, p, preferred_element_type=jnp.float32)
    u_ref[c, 0:1, :] = u_new


def _plain_update(u_ref, acc_ref, c, a, vt_h):
    del u_ref
    acc_ref[c] = acc_ref[c] + jnp.dot(vt_h, jnp.exp2(a).astype(jnp.bfloat16),
                                      preferred_element_type=jnp.float32)


def _reset(u_ref, acc_ref):
    u_ref[...] = jnp.full(u_ref.shape, NEG, jnp.float32)
    acc_ref[...] = jnp.zeros(acc_ref.shape, jnp.float32)


def _finish(acc_ref, z_ref, o_ref, r):
    accs = [acc_ref[r * HEADS_PER_TILE + h] for h in range(HEADS_PER_TILE)]
    ot = jnp.concatenate([acc[:HEAD_DIM] / acc[HEAD_DIM:HEAD_DIM + 1] for acc in accs], axis=0)
    rows = pl.ds(r * TQ, TQ)
    z = z_ref[rows, :].astype(jnp.float32)
    o_ref[rows, :] = (ot.T * (z / (1.0 + jnp.exp(-z)))).astype(o_ref.dtype)


def _attn_specs(seq, q_sub):
    tg = q_sub * TQ
    ng = seq // tg
    tok = pl.BlockSpec((tg, LANES), lambda b, hp, i, *_: (b * ng + i, hp))
    seq_rows = pl.BlockSpec((seq, LANES), lambda b, hp, i, *_: (b, hp))
    seq_cols = pl.BlockSpec((LANES, seq), lambda b, hp, i, *_: (hp, b))
    return ng, tok, seq_rows, seq_cols


def _state_scratch(q_sub):
    n = q_sub * HEADS_PER_TILE
    return [pltpu.VMEM((n, U_ROWS, TQ), jnp.float32), pltpu.VMEM((n, ACC_ROWS, TQ), jnp.float32)]


A_PAST = N_PAST_CHUNKS * CHUNK
A_WINDOW = A_PAST + TQ
A_BIAS_ROWS = 2 * A_PAST + TQ
A_BAND_LO = A_PAST - MAX_REL
A_BAND_HI = A_PAST + TQ + CHUNK


def _mixer_a_kernel(plain_ref, q_ref, k_ref, vt_ref, z_ref, bias_ref, o_ref, u_ref, acc_ref):
    gi = pl.program_id(2)
    plain = plain_ref[pl.program_id(0) * N_HEAD_PAIRS + pl.program_id(1)]

    def run(update):
        _reset(u_ref, acc_ref)

        def window(r):
            start = (gi * Q_SUB_A + r) * TQ - A_PAST
            row0 = pl.multiple_of(jnp.maximum(start, 0), TQ)
            return row0, pl.multiple_of(row0 - start, TQ)

        def issue(c):
            r, h = divmod(c, HEADS_PER_TILE)
            row0, brow = window(r)
            qm = _head_masked_queries(q_ref[pl.ds(r * TQ, TQ), :])[h]
            return _dot_nt(k_ref[pl.ds(row0, A_WINDOW), :], qm) + bias_ref[h, pl.ds(brow, A_WINDOW), :]

        def consume(c, a):
            r, h = divmod(c, HEADS_PER_TILE)
            row0, _ = window(r)
            update(u_ref, acc_ref, c, a, _head_values(vt_ref[:, pl.ds(row0, A_WINDOW)], h))
            if h == HEADS_PER_TILE - 1:
                _finish(acc_ref, z_ref, o_ref, r)

        _lagged(Q_SUB_A * HEADS_PER_TILE, issue, consume, ISSUE_LAG)

    @pl.when(plain != 0)
    def _():
        run(_plain_update)

    @pl.when(plain == 0)
    def _():
        run(_online_update)


A_BAND = A_BAND_HI - A_BAND_LO
A_STRIP = 1024
A_STRIP_OFF = 512


def _bias_band_kernel(s_ref, o_ref):
    x = jnp.broadcast_to(s_ref[0], (A_BAND, A_STRIP))
    o_ref[0] = pltpu.roll(x, 0, 1, stride=1, stride_axis=0)[:, A_STRIP_OFF:A_STRIP_OFF + TQ]


def _mixer_a_bias(rel_bias):
    assert A_STRIP_OFF >= A_BAND - 1 and A_STRIP_OFF + TQ <= A_STRIP
    rb = rel_bias.astype(jnp.float32) * LOG2E
    c = jnp.arange(A_BIAS_ROWS)[:, None]
    r = jnp.arange(TQ)[None, :]
    strip_dist = jnp.arange(A_STRIP) - A_STRIP_OFF + (A_PAST - A_BAND_LO)
    strip = rb[:, jnp.clip(strip_dist, -MAX_REL, MAX_REL) + MAX_REL].reshape(N_HEADS, 1, A_STRIP)
    band = pl.pallas_call(
        _bias_band_kernel,
        grid=(N_HEADS,),
        in_specs=[pl.BlockSpec((1, 1, A_STRIP), lambda h: (h, 0, 0))],
        out_specs=pl.BlockSpec((1, A_BAND, TQ), lambda h: (h, 0, 0)),
        out_shape=jax.ShapeDtypeStruct((N_HEADS, A_BAND, TQ), jnp.float32),
        compiler_params=_params("arbitrary"),
        name="bias_band",
    )(strip)
    far = jnp.broadcast_to(rb[:, 2 * MAX_REL][:, None, None], (N_HEADS, A_BAND_LO, TQ))
    late = jnp.zeros((N_HEADS, A_BIAS_ROWS - A_BAND_HI, TQ), jnp.float32)
    table = jnp.concatenate([far, band, late], axis=1)
    dchunk = r // CHUNK - (c - A_PAST) // CHUNK
    visible = (dchunk >= 0) & (dchunk <= N_PAST_CHUNKS)
    return jnp.where(visible[None], table, NEG)


def _mixer_a_schedule(qn, kn, rel_bias):
    worst = jnp.max(qn, axis=1) * jnp.max(kn, axis=1) + jnp.max(jnp.abs(rel_bias), axis=-1)[None, :] * LOG2E
    ok = worst < PLAIN_EXP_LOG2
    return jnp.all(ok.reshape(-1, N_HEAD_PAIRS, HEADS_PER_TILE), axis=-1).reshape(-1).astype(jnp.int32)


def _mixer_a(plain, q, k, vt, z, bias, batch, seq):
    ng, tok, seq_rows, seq_cols = _attn_specs(seq, Q_SUB_A)
    return pl.pallas_call(
        _mixer_a_kernel,
        grid_spec=pltpu.PrefetchScalarGridSpec(
            num_scalar_prefetch=1,
            grid=(batch, N_HEAD_PAIRS, ng),
            in_specs=[tok, seq_rows, seq_cols, tok,
                      pl.BlockSpec((HEADS_PER_TILE, A_BIAS_ROWS, TQ), lambda b, hp, i, *_: (hp, 0, 0))],
            out_specs=tok,
            scratch_shapes=_state_scratch(Q_SUB_A)),
        out_shape=jax.ShapeDtypeStruct(q.shape, jnp.bfloat16),
        compiler_params=_params("arbitrary", "arbitrary", "arbitrary"),
        name="mixer_a",
    )(plain, q, k, vt, z, bias)


TG_B = Q_SUB_B * TQ
KV_PER_G = TG_B // TKV
N_AUG = HEADS_PER_TILE * N_SPLIT
AUG_STRIDE = LANES // N_HEAD_PAIRS


def _fox_prep_kernel(lf_ref, bf_ref, tri_ref, sel_ref, ones_ref, gk_ref, gq_ref, st_ref, carry_ref):
    @pl.when(pl.program_id(1) == 0)
    def _():
        carry_ref[...] = jnp.zeros_like(carry_ref)

    pre = lf_ref[...] + bf_ref[...]
    logf = (jnp.minimum(pre, 0.0) - jnp.log(1.0 + jnp.exp(-jnp.abs(pre)))) * LOG2E
    tri = tri_ref[...]
    csum = sum(jnp.dot(tri, piece, preferred_element_type=jnp.float32) for piece in _split(logf, N_SPLIT))
    f = csum + carry_ref[...]
    carry_ref[...] = f[TKV - 1:TKV, :]
    pieces = _split(-f, N_SPLIT)
    for side, o_ref in enumerate((gk_ref, gq_ref)):
        o_ref[...] = (sum(jnp.dot(piece, sel_ref[side, n], preferred_element_type=jnp.float32)
                          for n, piece in enumerate(pieces)) + ones_ref[side:side + 1, :]).astype(jnp.bfloat16)
    st_ref[...] = jnp.concatenate(
        [f[0:1, :], f[TKV - 1:TKV, :], jnp.zeros((N_STATS - 2, LANES), jnp.float32)], axis=0)


def _fox_prep(lf, b_f, batch, seq):
    nb = seq // TKV
    bf = jnp.pad(b_f.astype(jnp.float32), (0, LANES - N_HEADS)).reshape(1, LANES)
    tri = (jnp.arange(TKV)[:, None] >= jnp.arange(TKV)[None, :]).astype(jnp.bfloat16)
    src = jnp.arange(LANES)[:, None]
    dst = jnp.arange(LANES)[None, :]
    base = (src // HEADS_PER_TILE) * AUG_STRIDE + (src % HEADS_PER_TILE) * N_SPLIT
    sel = jnp.stack([
        jnp.stack([jnp.where((dst == base + side * N_AUG + n) & (src < N_HEADS), 1.0 - 2.0 * side, 0.0)
                   for n in range(N_SPLIT)]) for side in range(2)]).astype(jnp.bfloat16)
    lane = jnp.arange(LANES) % AUG_STRIDE
    ones = jnp.stack([(lane >= N_AUG) & (lane < 2 * N_AUG), lane < N_AUG]).astype(jnp.float32)
    tokspec = pl.BlockSpec((TKV, LANES), lambda b, i: (b * nb + i, 0))
    aug = jax.ShapeDtypeStruct((batch * seq, LANES), jnp.bfloat16)
    return pl.pallas_call(
        _fox_prep_kernel,
        grid=(batch, nb),
        in_specs=[tokspec,
                  pl.BlockSpec((1, LANES), lambda b, i: (0, 0)),
                  pl.BlockSpec((TKV, TKV), lambda b, i: (0, 0)),
                  pl.BlockSpec((2, N_SPLIT, LANES, LANES), lambda b, i: (0, 0, 0, 0)),
                  pl.BlockSpec((2, LANES), lambda b, i: (0, 0))],
        out_specs=[tokspec, tokspec, pl.BlockSpec((None, None, N_STATS, LANES), lambda b, i: (b, i, 0, 0))],
        out_shape=[aug, aug, jax.ShapeDtypeStruct((batch, nb, N_STATS, LANES), jnp.float32)],
        scratch_shapes=[pltpu.VMEM((1, LANES), jnp.float32)],
        compiler_params=_params("arbitrary", "arbitrary"),
        name="fox_prep",
    )(lf, bf, tri, sel, ones)


def _fox_schedule(qn, kn, f_stats):
    b, nb = f_stats.shape[0], f_stats.shape[1]
    ng = nb // KV_PER_G
    f_first, f_last = f_stats[:, :, 0, :N_HEADS], f_stats[:, :, 1, :N_HEADS]
    qn_g = jnp.max(qn.reshape(b, ng, KV_PER_G, N_HEADS), axis=2)
    kn_g = jnp.max(kn.reshape(b, ng, KV_PER_G, N_HEADS), axis=2)
    f_first_g = f_first.reshape(b, ng, KV_PER_G, N_HEADS)[:, :, 0]
    bound = (qn_g[:, :, None] * kn[:, None, :] + f_first_g[:, :, None] - f_last[:, None, :]
             + (qn_g * kn_g)[:, :, None])
    i_idx = jnp.arange(ng)[None, :, None, None]
    j_idx = jnp.arange(nb)[None, None, :, None]
    skip = (bound < -PRUNE_LOG2) & (j_idx < i_idx * KV_PER_G)
    first = jnp.min(jnp.where(skip, nb, j_idx), axis=2)
    first = jnp.min(first.reshape(b, ng, N_HEAD_PAIRS, HEADS_PER_TILE), axis=-1)
    first = jnp.transpose(first, (0, 2, 1)).reshape(-1)
    worst = jnp.max(qn, axis=1) * jnp.max(kn, axis=1)
    plain = jnp.all((worst < PLAIN_EXP_LOG2).reshape(b, N_HEAD_PAIRS, HEADS_PER_TILE), axis=-1).reshape(-1)
    return jnp.concatenate([first, plain.astype(first.dtype)]).astype(jnp.int32)


def _fox_kernel(sched_ref, q_ref, gq_ref, k_ref, gk_ref, vt_ref, z_ref, o_ref, u_ref, acc_ref):
    hp = pl.program_id(1)
    gi = pl.program_id(2)
    n_groups = pl.num_programs(2)
    pair = pl.program_id(0) * N_HEAD_PAIRS + hp
    j_first = sched_ref[pair * n_groups + gi]
    plain = sched_ref[pl.num_programs(0) * N_HEAD_PAIRS * n_groups + pair]
    n_chains = Q_SUB_B * HEADS_PER_TILE

    def qaug(c):
        r, h = divmod(c, HEADS_PER_TILE)
        rows = pl.ds(r * TQ, TQ)
        gq = gq_ref[rows, :]
        off = lax.broadcasted_iota(jnp.int32, (TQ, LANES), 1) - hp * AUG_STRIDE
        own = ((off >= N_SPLIT * h) & (off < N_SPLIT * (h + 1))) | \
              ((off >= N_AUG + N_SPLIT * h) & (off < N_AUG + N_SPLIT * (h + 1)))
        return jnp.concatenate([_head_masked_queries(q_ref[rows, :])[h],
                                jnp.where(own, gq, jnp.zeros_like(gq))], axis=1)

    def keys_at(row0, n):
        rows = pl.ds(row0, n)
        return jnp.concatenate([k_ref[rows, :], gk_ref[rows, :]], axis=1), vt_ref[:, rows]

    def run(update):
        _reset(u_ref, acc_ref)
        qa = [qaug(c) for c in range(n_chains)]

        def keys_block(j, n_steps):
            kaug, vt_blk = keys_at(pl.multiple_of(j * TKV, TKV), n_steps * TKV)
            _lagged(n_chains,
                    lambda c: _dot_nt(kaug, qa[c]),
                    lambda c, a: update(u_ref, acc_ref, c, a, _head_values(vt_blk, c % HEADS_PER_TILE)),
                    ISSUE_LAG)

        n_before = gi * KV_PER_G - j_first
        odd = n_before & 1

        def pair(p, carry):
            keys_block(j_first + odd + 2 * p, 2)
            return carry

        lax.fori_loop(0, lax.shift_right_logical(n_before, 1), pair, 0)

        def own_block(with_step):
            kaug, vt_blk = keys_at(pl.multiple_of(gi * TG_B, TG_B), TG_B)
            s_loc = lax.broadcasted_iota(jnp.int32, (TQ, TQ), 0)
            t_loc = lax.broadcasted_iota(jnp.int32, (TQ, TQ), 1)
            n_extra = n_chains if with_step else 0
            if with_step:
                kaug_e, vt_e = keys_at(pl.multiple_of(j_first * TKV, TKV), TKV)

            def issue(i):
                if i < n_extra:
                    return _dot_nt(kaug_e, qa[i])
                c = i - n_extra
                r = c // HEADS_PER_TILE
                a = _dot_nt(kaug[:(r + 1) * TQ], qa[c])
                diag = jnp.where(s_loc <= t_loc, a[r * TQ:], NEG)
                return diag if r == 0 else jnp.concatenate([a[:r * TQ], diag], axis=0)

            def consume(i, a):
                if i < n_extra:
                    update(u_ref, acc_ref, i, a, _head_values(vt_e, i % HEADS_PER_TILE))
                    return
                c = i - n_extra
                r, h = divmod(c, HEADS_PER_TILE)
                update(u_ref, acc_ref, c, a, _head_values(vt_blk[:, :(r + 1) * TQ], h))
                if h == HEADS_PER_TILE - 1:
                    _finish(acc_ref, z_ref, o_ref, r)

            _lagged(n_extra + n_chains, issue, consume, ISSUE_LAG)

        @pl.when(odd != 0)
        def _():
            own_block(True)

        @pl.when(odd == 0)
        def _():
            own_block(False)

    @pl.when(plain != 0)
    def _():
        run(_plain_update)

    @pl.when(plain == 0)
    def _():
        run(_online_update)


def _fox(sched, q, gq, k, gk, vt, z, batch, seq):
    ng, tok, seq_rows, seq_cols = _attn_specs(seq, Q_SUB_B)
    aug_tok = pl.BlockSpec((TG_B, LANES), lambda b, hp, i, *_: (b * ng + i, 0))
    aug_seq = pl.BlockSpec((seq, LANES), lambda b, hp, i, *_: (b, 0))
    return pl.pallas_call(
        _fox_kernel,
        grid_spec=pltpu.PrefetchScalarGridSpec(
            num_scalar_prefetch=1,
            grid=(batch, N_HEAD_PAIRS, ng),
            in_specs=[tok, aug_tok, seq_rows, aug_seq, seq_cols, tok],
            out_specs=tok,
            scratch_shapes=_state_scratch(Q_SUB_B)),
        out_shape=jax.ShapeDtypeStruct(q.shape, jnp.bfloat16),
        compiler_params=_params("arbitrary", "arbitrary", "arbitrary"),
        name="fox",
    )(sched, q, gq, k, gk, vt, z)


def _in_weights(w_in, w_gate=None):
    w_in = w_in.astype(jnp.float32)
    e = MIX_WIDTH
    cols = [w_in[:, :e] * (LOG2E / math.sqrt(HEAD_DIM)), w_in[:, e:2 * e], w_in[:, 3 * e:]]
    if w_gate is not None:
        cols.append(jnp.pad(w_gate.astype(jnp.float32), ((0, 0), (0, LANES - N_HEADS))))
    return jnp.concatenate(cols, axis=1).astype(jnp.bfloat16), w_in[:, 2 * e:3 * e].T.astype(jnp.bfloat16)


def _reorder_heads(order, w_in, w_f, b_f, w_out):
    d = w_in.shape[0]
    w_in = w_in.reshape(d, 4, N_HEADS, HEAD_DIM)[:, :, order].reshape(d, 4 * MIX_WIDTH)
    w_out = w_out.reshape(N_HEADS, HEAD_DIM, -1)[order].reshape(MIX_WIDTH, -1)
    return w_in, w_f[:, order], b_f[order], w_out


def kernel(x, w_in_a, rel_bias_a, w_out_a, w_in_b, w_f_b, b_f_b, w_out_b, ln_g, ln_b):
    batch, seq, d = x.shape
    assert seq % (Q_SUB_A * TQ) == 0 and seq % TG_B == 0 and TG_B % TKV == 0 and TKV == TM_PROJ
    assert seq >= A_WINDOW and N_HEAD_PAIRS * AUG_STRIDE == LANES and 2 * N_AUG <= AUG_STRIDE
    xf = x.reshape(batch * seq, d).astype(jnp.float32)

    layers = []
    for i in range(DEPTH):
        j = i // N_MIXERS
        if i % N_MIXERS == 0:
            layers.append((_in_weights(w_in_a[j]), w_out_a[j].astype(jnp.bfloat16), None))
        else:
            order = jnp.argsort(b_f_b[j])
            w_in, w_f, b_f, w_out = _reorder_heads(order, w_in_b[j], w_f_b[j], b_f_b[j], w_out_b[j])
            layers.append((_in_weights(w_in, w_f), w_out.astype(jnp.bfloat16), b_f))

    outs = _proj(xf, nxt=layers[0][0])
    for i in range(DEPTH):
        _, w_out, b_f = layers[i]
        j = i // N_MIXERS
        if b_f is None:
            q, k, vt, z, nst = outs
            plain = _mixer_a_schedule(*_max_norms(nst, batch), rel_bias_a[j])
            g = _mixer_a(plain, q, k, vt, z, _mixer_a_bias(rel_bias_a[j]), batch, seq)
        else:
            q, k, vt, z, nst, lf = outs
            gk, gq, f_stats = _fox_prep(lf, b_f, batch, seq)
            g = _fox(_fox_schedule(*_max_norms(nst, batch), f_stats), q, gq, k, gk, vt, z, batch, seq)
        xf, *outs = _proj(xf, prev=(g, w_out, ln_g[i], ln_b[i]),
                          nxt=layers[i + 1][0] if i + 1 < DEPTH else None)
    return xf.reshape(batch, seq, d).astype(x.dtype)
```

```python
import functools
import math

import jax
import jax.numpy as jnp
from jax import lax
from jax.experimental import pallas as pl
from jax.experimental.pallas import tpu as pltpu

D_MODEL = 1024
N_HEADS = 16
HEAD_DIM = 64
MIX_WIDTH = N_HEADS * HEAD_DIM
CHUNK = 64
N_PAST_CHUNKS = 8
MAX_REL = 128
DEPTH = 4
N_MIXERS = 2
LN_EPS = 1e-5
DEEPNORM_ALPHA = (2.0 * DEPTH) ** 0.25
LOG2E = math.log2(math.e)
NEG = -1e30

LANES = 128
HEADS_PER_TILE = LANES // HEAD_DIM
N_HEAD_PAIRS = N_HEADS // HEADS_PER_TILE
BF16_ROWS = 16
N_SPLIT = 3
TQ = 256
Q_SUB_A = 8
Q_SUB_B = 4
TKV = 512
TM_PROJ = 512
ISSUE_LAG = 8
VMEM_LIMIT = 56 * 1024 * 1024
PRUNE_LOG2 = 152.0
NORM_SLACK = 1.0 + 2.0 ** -6
PLAIN_EXP_LOG2 = 60.0


def _params(*semantics):
    return pltpu.CompilerParams(dimension_semantics=semantics, vmem_limit_bytes=VMEM_LIMIT)


def _dot_nt(a, b):
    return lax.dot_general(a, b, (((1,), (1,)), ((), ())), preferred_element_type=jnp.float32)


def _split(x, n):
    pieces = []
    for _ in range(n):
        p = x.astype(jnp.bfloat16)
        pieces.append(p)
        x = x - p.astype(jnp.float32)
    return pieces


N_STATS = 8
LN_ROW_CHUNKS = 2


def _lagged(n, issue, consume, lag):
    pending = {}
    for c in range(n + lag):
        if c < n:
            pending[c] = issue(c)
        if c >= lag:
            consume(c - lag, pending.pop(c - lag))


def _out_proj_ln_rows(g_ref, x_ref, wo_ref, gam_ref, bet_ref, xo_ref):
    rows = TM_PROJ // LN_ROW_CHUNKS
    out = []

    def issue(c):
        return jnp.dot(g_ref[pl.ds(c * rows, rows), :], wo_ref[...], preferred_element_type=jnp.float32)

    def consume(c, y):
        sl = pl.ds(c * rows, rows)
        r = DEEPNORM_ALPHA * x_ref[sl, :] + y
        mu = jnp.mean(r, axis=-1, keepdims=True)
        d = r - mu
        var = jnp.mean(d * d, axis=-1, keepdims=True)
        xn = d * lax.rsqrt(var + LN_EPS) * gam_ref[...] + bet_ref[...]
        xo_ref[sl, :] = xn
        out.append(xn)

    _lagged(LN_ROW_CHUNKS, issue, consume, lag=1)
    return jnp.concatenate(out, axis=0)


def _in_proj_rows(x, w_ref, wvt_ref, hsel_ref, q_ref, k_ref, vt_ref, z_ref, nst_ref, gate_ref):
    xb = x.astype(jnp.bfloat16)
    sq_max = []
    for g, o_ref in enumerate((q_ref, k_ref, z_ref)):
        o = jnp.dot(xb, w_ref[:, g * MIX_WIDTH:(g + 1) * MIX_WIDTH],
                    preferred_element_type=jnp.float32).astype(jnp.bfloat16)
        o_ref[...] = o
        if g < 2:
            of = o.astype(jnp.float32)
            sq = jnp.dot((of * of).astype(jnp.bfloat16), hsel_ref[...], preferred_element_type=jnp.float32)
            sq_max.append(jnp.max(sq, axis=0, keepdims=True))
    nst_ref[...] = jnp.concatenate(sq_max + [jnp.zeros((N_STATS - 2, LANES), jnp.float32)], axis=0)
    vt_ref[...] = _dot_nt(wvt_ref[...], xb).astype(jnp.bfloat16)
    if gate_ref is not None:
        gate_ref[...] = jnp.dot(xb, w_ref[:, 3 * MIX_WIDTH:], preferred_element_type=jnp.float32)


def _proj_kernel(*refs, has_out, has_in, has_gate):
    refs = list(refs)
    n_in = (5 if has_out else 1) + (3 if has_in else 0)
    ins, outs = refs[:n_in], refs[n_in:]
    if has_out:
        x = _out_proj_ln_rows(*ins[:5], outs.pop(0))
        ins = ins[5:]
    else:
        x = ins.pop(0)[...]
    if has_in:
        _in_proj_rows(x, *ins, *outs[:5], outs[5] if has_gate else None)


def _proj(xf, prev=None, nxt=None):
    m = xf.shape[0]
    row_d = pl.BlockSpec((TM_PROJ, D_MODEL), lambda i: (i, 0))
    row_e = pl.BlockSpec((TM_PROJ, MIX_WIDTH), lambda i: (i, 0))

    def whole(shape):
        return pl.BlockSpec(shape, lambda i: (0,) * len(shape))

    args, in_specs, out_shape, out_specs = [], [], [], []
    if prev is not None:
        g, w_out, gamma, beta = prev
        args += [g, xf, w_out, gamma.reshape(1, D_MODEL), beta.reshape(1, D_MODEL)]
        in_specs += [row_e, row_d, whole((MIX_WIDTH, D_MODEL)), whole((1, D_MODEL)), whole((1, D_MODEL))]
        out_shape.append(jax.ShapeDtypeStruct((m, D_MODEL), jnp.float32))
        out_specs.append(row_d)
    else:
        args.append(xf)
        in_specs.append(row_d)
    has_gate = False
    if nxt is not None:
        w, wvt = nxt
        has_gate = w.shape[1] > 3 * MIX_WIDTH
        hsel = (jnp.arange(MIX_WIDTH)[:, None] // HEAD_DIM == jnp.arange(LANES)[None, :]).astype(jnp.bfloat16)
        args += [w, wvt, hsel]
        in_specs += [whole(w.shape), whole(wvt.shape), whole(hsel.shape)]
        tok = jax.ShapeDtypeStruct((m, MIX_WIDTH), jnp.bfloat16)
        out_shape += [tok, tok, jax.ShapeDtypeStruct((MIX_WIDTH, m), jnp.bfloat16), tok,
                      jax.ShapeDtypeStruct((m // TM_PROJ, N_STATS, LANES), jnp.float32)]
        out_specs += [row_e, row_e, pl.BlockSpec((MIX_WIDTH, TM_PROJ), lambda i: (0, i)), row_e,
                      pl.BlockSpec((None, N_STATS, LANES), lambda i: (i, 0, 0))]
        if has_gate:
            out_shape.append(jax.ShapeDtypeStruct((m, LANES), jnp.float32))
            out_specs.append(pl.BlockSpec((TM_PROJ, LANES), lambda i: (i, 0)))
    name = (("out_" if prev is not None else "") + ("in_" if nxt is not None else "") + "proj"
            + ("_gate" if has_gate else ""))
    return pl.pallas_call(
        functools.partial(_proj_kernel, has_out=prev is not None, has_in=nxt is not None, has_gate=has_gate),
        grid=(m // TM_PROJ,),
        in_specs=in_specs,
        out_specs=out_specs,
        out_shape=out_shape,
        compiler_params=_params("arbitrary"),
        name=name,
    )(*args)


def _max_norms(nst, batch):
    nb = nst.shape[0] // batch
    n = jnp.sqrt(nst[:, :2, :N_HEADS]).reshape(batch, nb, 2, N_HEADS) * NORM_SLACK
    return n[:, :, 0], n[:, :, 1]


ACC_ROWS = HEAD_DIM + BF16_ROWS
U_ROWS = 8


def _head_masked_queries(q2):
    lane = lax.broadcasted_iota(jnp.int32, q2.shape, 1)
    zero = jnp.zeros_like(q2)
    return [jnp.where((lane >= HEAD_DIM * h) & (lane < HEAD_DIM * (h + 1)), q2, zero)
            for h in range(HEADS_PER_TILE)]


def _head_values(vt_blk, h):
    ones = jnp.ones((BF16_ROWS, vt_blk.shape[1]), vt_blk.dtype)
    return jnp.concatenate([vt_blk[HEAD_DIM * h:HEAD_DIM * (h + 1), :], ones], axis=0)


def _online_update(u_ref, acc_ref, c, a, vt_h):
    u = u_ref[c, 0:1, :]
    u_new = jnp.maximum(u, jnp.max(a, axis=0, keepdims=True))
    p = jnp.exp2(a - u_new).astype(jnp.bfloat16)
    acc_ref[c] = jnp.exp2(u - u_new) * acc_ref[c] + jnp.dot(vt_h, p, preferred_element_type=jnp.float32)
    u_ref[c, 0:1, :] = u_new


def _plain_update(u_ref, acc_ref, c, a, vt_h):
    del u_ref
    acc_ref[c] = acc_ref[c] + jnp.dot(vt_h, jnp.exp2(a).astype(jnp.bfloat16),
                                      preferred_element_type=jnp.float32)


def _reset(u_ref, acc_ref):
    u_ref[...] = jnp.full(u_ref.shape, NEG, jnp.float32)
    acc_ref[...] = jnp.zeros(acc_ref.shape, jnp.float32)


def _finish(acc_ref, z_ref, o_ref, r):
    accs = [acc_ref[r * HEADS_PER_TILE + h] for h in range(HEADS_PER_TILE)]
    ot = jnp.concatenate([acc[:HEAD_DIM] / acc[HEAD_DIM:HEAD_DIM + 1] for acc in accs], axis=0)
    rows = pl.ds(r * TQ, TQ)
    z = z_ref[rows, :].astype(jnp.float32)
    o_ref[rows, :] = (ot.T * (z / (1.0 + jnp.exp(-z)))).astype(o_ref.dtype)


def _attn_specs(seq, q_sub, pair_major=False):
    tg = q_sub * TQ
    ng = seq // tg

    def spec(block, index):
        if pair_major:
            return pl.BlockSpec(block, lambda hp, b, i, *_: index(b, hp, i))
        return pl.BlockSpec(block, lambda b, hp, i, *_: index(b, hp, i))

    tok = spec((tg, LANES), lambda b, hp, i: (b * ng + i, hp))
    seq_rows = spec((seq, LANES), lambda b, hp, i: (b, hp))
    seq_cols = spec((LANES, seq), lambda b, hp, i: (hp, b))
    return ng, tok, seq_rows, seq_cols, spec


def _state_scratch(q_sub):
    n = q_sub * HEADS_PER_TILE
    return [pltpu.VMEM((n, U_ROWS, TQ), jnp.float32), pltpu.VMEM((n, ACC_ROWS, TQ), jnp.float32)]


A_PAST = N_PAST_CHUNKS * CHUNK
A_WINDOW = A_PAST + TQ
A_BIAS_ROWS = 2 * A_PAST + TQ
A_BAND_LO = A_PAST - MAX_REL
A_BAND_HI = A_PAST + TQ + CHUNK


def _mixer_a_kernel(plain_ref, q_ref, k_ref, vt_ref, z_ref, bias_ref, o_ref, u_ref, acc_ref):
    gi = pl.program_id(2)
    plain = plain_ref[pl.program_id(1) * N_HEAD_PAIRS + pl.program_id(0)]

    def run(update):
        _reset(u_ref, acc_ref)

        def window(r):
            start = (gi * Q_SUB_A + r) * TQ - A_PAST
            row0 = pl.multiple_of(jnp.maximum(start, 0), TQ)
            return row0, pl.multiple_of(row0 - start, TQ)

        def issue(c):
            r, h = divmod(c, HEADS_PER_TILE)
            row0, brow = window(r)
            qm = _head_masked_queries(q_ref[pl.ds(r * TQ, TQ), :])[h]
            return _dot_nt(k_ref[pl.ds(row0, A_WINDOW), :], qm) + bias_ref[h, pl.ds(brow, A_WINDOW), :]

        def consume(c, a):
            r, h = divmod(c, HEADS_PER_TILE)
            row0, _ = window(r)
            update(u_ref, acc_ref, c, a, _head_values(vt_ref[:, pl.ds(row0, A_WINDOW)], h))
            if h == HEADS_PER_TILE - 1:
                _finish(acc_ref, z_ref, o_ref, r)

        _lagged(Q_SUB_A * HEADS_PER_TILE, issue, consume, ISSUE_LAG)

    @pl.when(plain != 0)
    def _():
        run(_plain_update)

    @pl.when(plain == 0)
    def _():
        run(_online_update)


A_BAND = A_BAND_HI - A_BAND_LO
A_STRIP = 1024
A_STRIP_OFF = 512


def _bias_band_kernel(s_ref, o_ref):
    x = jnp.broadcast_to(s_ref[0], (A_BAND, A_STRIP))
    o_ref[0] = pltpu.roll(x, 0, 1, stride=1, stride_axis=0)[:, A_STRIP_OFF:A_STRIP_OFF + TQ]


def _mixer_a_bias(rel_bias):
    assert A_STRIP_OFF >= A_BAND - 1 and A_STRIP_OFF + TQ <= A_STRIP
    rb = rel_bias.astype(jnp.float32) * LOG2E
    c = jnp.arange(A_BIAS_ROWS)[:, None]
    r = jnp.arange(TQ)[None, :]
    strip_dist = jnp.arange(A_STRIP) - A_STRIP_OFF + (A_PAST - A_BAND_LO)
    strip = rb[:, jnp.clip(strip_dist, -MAX_REL, MAX_REL) + MAX_REL].reshape(N_HEADS, 1, A_STRIP)
    band = pl.pallas_call(
        _bias_band_kernel,
        grid=(N_HEADS,),
        in_specs=[pl.BlockSpec((1, 1, A_STRIP), lambda h: (h, 0, 0))],
        out_specs=pl.BlockSpec((1, A_BAND, TQ), lambda h: (h, 0, 0)),
        out_shape=jax.ShapeDtypeStruct((N_HEADS, A_BAND, TQ), jnp.float32),
        compiler_params=_params("arbitrary"),
        name="bias_band",
    )(strip)
    far = jnp.broadcast_to(rb[:, 2 * MAX_REL][:, None, None], (N_HEADS, A_BAND_LO, TQ))
    late = jnp.zeros((N_HEADS, A_BIAS_ROWS - A_BAND_HI, TQ), jnp.float32)
    table = jnp.concatenate([far, band, late], axis=1)
    dchunk = r // CHUNK - (c - A_PAST) // CHUNK
    visible = (dchunk >= 0) & (dchunk <= N_PAST_CHUNKS)
    return jnp.where(visible[None], table, NEG)


def _mixer_a_schedule(qn, kn, rel_bias):
    worst = jnp.max(qn, axis=1) * jnp.max(kn, axis=1) + jnp.max(jnp.abs(rel_bias), axis=-1)[None, :] * LOG2E
    ok = worst < PLAIN_EXP_LOG2
    return jnp.all(ok.reshape(-1, N_HEAD_PAIRS, HEADS_PER_TILE), axis=-1).reshape(-1).astype(jnp.int32)


def _mixer_a(plain, q, k, vt, z, bias, batch, seq):
    ng, tok, seq_rows, seq_cols, spec = _attn_specs(seq, Q_SUB_A, pair_major=True)
    return pl.pallas_call(
        _mixer_a_kernel,
        grid_spec=pltpu.PrefetchScalarGridSpec(
            num_scalar_prefetch=1,
            grid=(N_HEAD_PAIRS, batch, ng),
            in_specs=[tok, seq_rows, seq_cols, tok,
                      spec((HEADS_PER_TILE, A_BIAS_ROWS, TQ), lambda b, hp, i: (hp, 0, 0))],
            out_specs=tok,
            scratch_shapes=_state_scratch(Q_SUB_A)),
        out_shape=jax.ShapeDtypeStruct(q.shape, jnp.bfloat16),
        compiler_params=_params("arbitrary", "arbitrary", "arbitrary"),
        name="mixer_a",
    )(plain, q, k, vt, z, bias)


TG_B = Q_SUB_B * TQ
KV_PER_G = TG_B // TKV
N_AUG = HEADS_PER_TILE * N_SPLIT
AUG_STRIDE = LANES // N_HEAD_PAIRS


def _fox_prep_kernel(lf_ref, bf_ref, tri_ref, sel_ref, ones_ref, gk_ref, gq_ref, st_ref, carry_ref):
    @pl.when(pl.program_id(1) == 0)
    def _():
        carry_ref[...] = jnp.zeros_like(carry_ref)

    pre = lf_ref[...] + bf_ref[...]
    logf = (jnp.minimum(pre, 0.0) - jnp.log(1.0 + jnp.exp(-jnp.abs(pre)))) * LOG2E
    tri = tri_ref[...]
    csum = sum(jnp.dot(tri, piece, preferred_element_type=jnp.float32) for piece in _split(logf, N_SPLIT))
    f = csum + carry_ref[...]
    carry_ref[...] = f[TKV - 1:TKV, :]
    pieces = _split(-f, N_SPLIT)
    for side, o_ref in enumerate((gk_ref, gq_ref)):
        o_ref[...] = (sum(jnp.dot(piece, sel_ref[side, n], preferred_element_type=jnp.float32)
                          for n, piece in enumerate(pieces)) + ones_ref[side:side + 1, :]).astype(jnp.bfloat16)
    st_ref[...] = jnp.concatenate(
        [f[0:1, :], f[TKV - 1:TKV, :], jnp.zeros((N_STATS - 2, LANES), jnp.float32)], axis=0)


def _fox_prep(lf, b_f, batch, seq):
    nb = seq // TKV
    bf = jnp.pad(b_f.astype(jnp.float32), (0, LANES - N_HEADS)).reshape(1, LANES)
    tri = (jnp.arange(TKV)[:, None] >= jnp.arange(TKV)[None, :]).astype(jnp.bfloat16)
    src = jnp.arange(LANES)[:, None]
    dst = jnp.arange(LANES)[None, :]
    base = (src // HEADS_PER_TILE) * AUG_STRIDE + (src % HEADS_PER_TILE) * N_SPLIT
    sel = jnp.stack([
        jnp.stack([jnp.where((dst == base + side * N_AUG + n) & (src < N_HEADS), 1.0 - 2.0 * side, 0.0)
                   for n in range(N_SPLIT)]) for side in range(2)]).astype(jnp.bfloat16)
    lane = jnp.arange(LANES) % AUG_STRIDE
    ones = jnp.stack([(lane >= N_AUG) & (lane < 2 * N_AUG), lane < N_AUG]).astype(jnp.float32)
    tokspec = pl.BlockSpec((TKV, LANES), lambda b, i: (b * nb + i, 0))
    aug = jax.ShapeDtypeStruct((batch * seq, LANES), jnp.bfloat16)
    return pl.pallas_call(
        _fox_prep_kernel,
        grid=(batch, nb),
        in_specs=[tokspec,
                  pl.BlockSpec((1, LANES), lambda b, i: (0, 0)),
                  pl.BlockSpec((TKV, TKV), lambda b, i: (0, 0)),
                  pl.BlockSpec((2, N_SPLIT, LANES, LANES), lambda b, i: (0, 0, 0, 0)),
                  pl.BlockSpec((2, LANES), lambda b, i: (0, 0))],
        out_specs=[tokspec, tokspec, pl.BlockSpec((None, None, N_STATS, LANES), lambda b, i: (b, i, 0, 0))],
        out_shape=[aug, aug, jax.ShapeDtypeStruct((batch, nb, N_STATS, LANES), jnp.float32)],
        scratch_shapes=[pltpu.VMEM((1, LANES), jnp.float32)],
        compiler_params=_params("arbitrary", "arbitrary"),
        name="fox_prep",
    )(lf, bf, tri, sel, ones)


def _fox_schedule(qn, kn, f_stats):
    b, nb = f_stats.shape[0], f_stats.shape[1]
    ng = nb // KV_PER_G
    f_first, f_last = f_stats[:, :, 0, :N_HEADS], f_stats[:, :, 1, :N_HEADS]
    qn_g = jnp.max(qn.reshape(b, ng, KV_PER_G, N_HEADS), axis=2)
    kn_g = jnp.max(kn.reshape(b, ng, KV_PER_G, N_HEADS), axis=2)
    f_first_g = f_first.reshape(b, ng, KV_PER_G, N_HEADS)[:, :, 0]
    bound = (qn_g[:, :, None] * kn[:, None, :] + f_first_g[:, :, None] - f_last[:, None, :]
             + (qn_g * kn_g)[:, :, None])
    i_idx = jnp.arange(ng)[None, :, None, None]
    j_idx = jnp.arange(nb)[None, None, :, None]
    skip = (bound < -PRUNE_LOG2) & (j_idx < i_idx * KV_PER_G)
    first = jnp.min(jnp.where(skip, nb, j_idx), axis=2)
    first = jnp.min(first.reshape(b, ng, N_HEAD_PAIRS, HEADS_PER_TILE), axis=-1)
    first = jnp.transpose(first, (0, 2, 1)).reshape(-1)
    worst = jnp.max(qn, axis=1) * jnp.max(kn, axis=1)
    plain = jnp.all((worst < PLAIN_EXP_LOG2).reshape(b, N_HEAD_PAIRS, HEADS_PER_TILE), axis=-1).reshape(-1)
    return jnp.concatenate([first, plain.astype(first.dtype)]).astype(jnp.int32)


def _fox_kernel(sched_ref, q_ref, gq_ref, k_ref, gk_ref, vt_ref, z_ref, o_ref, u_ref, acc_ref):
    hp = pl.program_id(1)
    gi = pl.program_id(2)
    n_groups = pl.num_programs(2)
    pair = pl.program_id(0) * N_HEAD_PAIRS + hp
    j_first = sched_ref[pair * n_groups + gi]
    plain = sched_ref[pl.num_programs(0) * N_HEAD_PAIRS * n_groups + pair]
    n_chains = Q_SUB_B * HEADS_PER_TILE

    def qaug(c):
        r, h = divmod(c, HEADS_PER_TILE)
        rows = pl.ds(r * TQ, TQ)
        gq = gq_ref[rows, :]
        off = lax.broadcasted_iota(jnp.int32, (TQ, LANES), 1) - hp * AUG_STRIDE
        own = ((off >= N_SPLIT * h) & (off < N_SPLIT * (h + 1))) | \
              ((off >= N_AUG + N_SPLIT * h) & (off < N_AUG + N_SPLIT * (h + 1)))
        return jnp.concatenate([_head_masked_queries(q_ref[rows, :])[h],
                                jnp.where(own, gq, jnp.zeros_like(gq))], axis=1)

    def keys_at(row0, n):
        rows = pl.ds(row0, n)
        return jnp.concatenate([k_ref[rows, :], gk_ref[rows, :]], axis=1), vt_ref[:, rows]

    def run(update):
        _reset(u_ref, acc_ref)
        qa = [qaug(c) for c in range(n_chains)]

        def keys_block(j, n_steps):
            kaug, vt_blk = keys_at(pl.multiple_of(j * TKV, TKV), n_steps * TKV)
            _lagged(n_chains,
                    lambda c: _dot_nt(kaug, qa[c]),
                    lambda c, a: update(u_ref, acc_ref, c, a, _head_values(vt_blk, c % HEADS_PER_TILE)),
                    ISSUE_LAG)

        n_before = gi * KV_PER_G - j_first
        odd = n_before & 1

        def pair(p, carry):
            keys_block(j_first + odd + 2 * p, 2)
            return carry

        lax.fori_loop(0, lax.shift_right_logical(n_before, 1), pair, 0)

        def own_block(with_step):
            kaug, vt_blk = keys_at(pl.multiple_of(gi * TG_B, TG_B), TG_B)
            s_loc = lax.broadcasted_iota(jnp.int32, (TQ, TQ), 0)
            t_loc = lax.broadcasted_iota(jnp.int32, (TQ, TQ), 1)
            n_extra = n_chains if with_step else 0
            if with_step:
                kaug_e, vt_e = keys_at(pl.multiple_of(j_first * TKV, TKV), TKV)

            def issue(i):
                if i < n_extra:
                    return _dot_nt(kaug_e, qa[i])
                c = i - n_extra
                r = c // HEADS_PER_TILE
                a = _dot_nt(kaug[:(r + 1) * TQ], qa[c])
                diag = jnp.where(s_loc <= t_loc, a[r * TQ:], NEG)
                return diag if r == 0 else jnp.concatenate([a[:r * TQ], diag], axis=0)

            def consume(i, a):
                if i < n_extra:
                    update(u_ref, acc_ref, i, a, _head_values(vt_e, i % HEADS_PER_TILE))
                    return
                c = i - n_extra
                r, h = divmod(c, HEADS_PER_TILE)
                update(u_ref, acc_ref, c, a, _head_values(vt_blk[:, :(r + 1) * TQ], h))
                if h == HEADS_PER_TILE - 1:
                    _finish(acc_ref, z_ref, o_ref, r)

            _lagged(n_extra + n_chains, issue, consume, ISSUE_LAG)

        @pl.when(odd != 0)
        def _():
            own_block(True)

        @pl.when(odd == 0)
        def _():
            own_block(False)

    @pl.when(plain != 0)
    def _():
        run(_plain_update)

    @pl.when(plain == 0)
    def _():
        run(_online_update)


def _fox(sched, q, gq, k, gk, vt, z, batch, seq):
    ng, tok, seq_rows, seq_cols, spec = _attn_specs(seq, Q_SUB_B)
    aug_tok = spec((TG_B, LANES), lambda b, hp, i: (b * ng + i, 0))
    aug_seq = spec((seq, LANES), lambda b, hp, i: (b, 0))
    return pl.pallas_call(
        _fox_kernel,
        grid_spec=pltpu.PrefetchScalarGridSpec(
            num_scalar_prefetch=1,
            grid=(batch, N_HEAD_PAIRS, ng),
            in_specs=[tok, aug_tok, seq_rows, aug_seq, seq_cols, tok],
            out_specs=tok,
            scratch_shapes=_state_scratch(Q_SUB_B)),
        out_shape=jax.ShapeDtypeStruct(q.shape, jnp.bfloat16),
        compiler_params=_params("arbitrary", "arbitrary", "arbitrary"),
        name="fox",
    )(sched, q, gq, k, gk, vt, z)


def _in_weights(w_in, w_gate=None):
    w_in = w_in.astype(jnp.float32)
    e = MIX_WIDTH
    cols = [w_in[:, :e] * (LOG2E / math.sqrt(HEAD_DIM)), w_in[:, e:2 * e], w_in[:, 3 * e:]]
    if w_gate is not None:
        cols.append(jnp.pad(w_gate.astype(jnp.float32), ((0, 0), (0, LANES - N_HEADS))))
    return jnp.concatenate(cols, axis=1).astype(jnp.bfloat16), w_in[:, 2 * e:3 * e].T.astype(jnp.bfloat16)


def _reorder_heads(order, w_in, w_f, b_f, w_out):
    d = w_in.shape[0]
    w_in = w_in.reshape(d, 4, N_HEADS, HEAD_DIM)[:, :, order].reshape(d, 4 * MIX_WIDTH)
    w_out = w_out.reshape(N_HEADS, HEAD_DIM, -1)[order].reshape(MIX_WIDTH, -1)
    return w_in, w_f[:, order], b_f[order], w_out


def kernel(x, w_in_a, rel_bias_a, w_out_a, w_in_b, w_f_b, b_f_b, w_out_b, ln_g, ln_b):
    batch, seq, d = x.shape
    assert seq % (Q_SUB_A * TQ) == 0 and seq % TG_B == 0 and TG_B % TKV == 0 and TKV == TM_PROJ
    assert seq >= A_WINDOW and N_HEAD_PAIRS * AUG_STRIDE == LANES and 2 * N_AUG <= AUG_STRIDE
    xf = x.reshape(batch * seq, d).astype(jnp.float32)

    layers = []
    for i in range(DEPTH):
        j = i // N_MIXERS
        if i % N_MIXERS == 0:
            layers.append((_in_weights(w_in_a[j]), w_out_a[j].astype(jnp.bfloat16), None))
        else:
            order = jnp.argsort(b_f_b[j])
            w_in, w_f, b_f, w_out = _reorder_heads(order, w_in_b[j], w_f_b[j], b_f_b[j], w_out_b[j])
            layers.append((_in_weights(w_in, w_f), w_out.astype(jnp.bfloat16), b_f))

    outs = _proj(xf, nxt=layers[0][0])
    for i in range(DEPTH):
        _, w_out, b_f = layers[i]
        j = i // N_MIXERS
        if b_f is None:
            q, k, vt, z, nst = outs
            plain = _mixer_a_schedule(*_max_norms(nst, batch), rel_bias_a[j])
            g = _mixer_a(plain, q, k, vt, z, _mixer_a_bias(rel_bias_a[j]), batch, seq)
        else:
            q, k, vt, z, nst, lf = outs
            gk, gq, f_stats = _fox_prep(lf, b_f, batch, seq)
            g = _fox(_fox_schedule(*_max_norms(nst, batch), f_stats), q, gq, k, gk, vt, z, batch, seq)
        xf, *outs = _proj(xf, prev=(g, w_out, ln_g[i], ln_b[i]),
                          nxt=layers[i + 1][0] if i + 1 < DEPTH else None)
    return xf.reshape(batch, seq, d).astype(x.dtype)
```

```python
import functools
import math

import jax
import jax.numpy as jnp
from jax import lax
from jax.experimental import pallas as pl
from jax.experimental.pallas import tpu as pltpu

D_MODEL = 1024
N_HEADS = 16
HEAD_DIM = 64
MIX_WIDTH = N_HEADS * HEAD_DIM
CHUNK = 64
N_PAST_CHUNKS = 8
MAX_REL = 128
DEPTH = 4
N_MIXERS = 2
LN_EPS = 1e-5
DEEPNORM_ALPHA = (2.0 * DEPTH) ** 0.25
LOG2E = math.log2(math.e)
NEG = -1e30

LANES = 128
HEADS_PER_TILE = LANES // HEAD_DIM
N_HEAD_PAIRS = N_HEADS // HEADS_PER_TILE
BF16_ROWS = 16
N_SPLIT = 3
TQ = 256
Q_SUB_A = 8
Q_SUB_B = 4
TKV = 512
TM_PROJ = 512
ISSUE_LAG = 8
VMEM_LIMIT = 56 * 1024 * 1024
PRUNE_LOG2 = 152.0
NORM_SLACK = 1.0 + 2.0 ** -6
PLAIN_EXP_LOG2 = 60.0


def _params(*semantics):
    return pltpu.CompilerParams(dimension_semantics=semantics, vmem_limit_bytes=VMEM_LIMIT)


def _dot_nt(a, b):
    return lax.dot_general(a, b, (((1,), (1,)), ((), ())), preferred_element_type=jnp.float32)


def _split(x, n):
    pieces = []
    for _ in range(n):
        p = x.astype(jnp.bfloat16)
        pieces.append(p)
        x = x - p.astype(jnp.float32)
    return pieces


N_STATS = 8
LN_ROW_CHUNKS = 2


def _lagged(n, issue, consume, lag):
    pending = {}
    for c in range(n + lag):
        if c < n:
            pending[c] = issue(c)
        if c >= lag:
            consume(c - lag, pending.pop(c - lag))


def _out_proj_ln_rows(g_ref, x_ref, wo_ref, gam_ref, bet_ref, xo_ref):
    rows = TM_PROJ // LN_ROW_CHUNKS
    out = []

    def issue(c):
        return jnp.dot(g_ref[pl.ds(c * rows, rows), :], wo_ref[...], preferred_element_type=jnp.float32)

    def consume(c, y):
        sl = pl.ds(c * rows, rows)
        r = DEEPNORM_ALPHA * x_ref[sl, :] + y
        mu = jnp.mean(r, axis=-1, keepdims=True)
        d = r - mu
        var = jnp.mean(d * d, axis=-1, keepdims=True)
        xn = d * lax.rsqrt(var + LN_EPS) * gam_ref[...] + bet_ref[...]
        xo_ref[sl, :] = xn
        out.append(xn)

    _lagged(LN_ROW_CHUNKS, issue, consume, lag=1)
    return jnp.concatenate(out, axis=0)


def _in_proj_rows(x, w_ref, wvt_ref, hsel_ref, q_ref, k_ref, vt_ref, z_ref, nst_ref, gate_ref):
    xb = x.astype(jnp.bfloat16)
    sq_max = []
    for g, o_ref in enumerate((q_ref, k_ref, z_ref)):
        o = jnp.dot(xb, w_ref[:, g * MIX_WIDTH:(g + 1) * MIX_WIDTH],
                    preferred_element_type=jnp.float32).astype(jnp.bfloat16)
        o_ref[...] = o
        if g < 2:
            of = o.astype(jnp.float32)
            sq = jnp.dot((of * of).astype(jnp.bfloat16), hsel_ref[...], preferred_element_type=jnp.float32)
            sq_max.append(jnp.max(sq, axis=0, keepdims=True))
    nst_ref[...] = jnp.concatenate(sq_max + [jnp.zeros((N_STATS - 2, LANES), jnp.float32)], axis=0)
    vt_ref[...] = _dot_nt(wvt_ref[...], xb).astype(jnp.bfloat16)
    if gate_ref is not None:
        gate_ref[...] = jnp.dot(xb, w_ref[:, 3 * MIX_WIDTH:], preferred_element_type=jnp.float32)


def _proj_kernel(*refs, has_out, has_in, has_gate):
    refs = list(refs)
    n_in = (5 if has_out else 1) + (3 if has_in else 0)
    ins, outs = refs[:n_in], refs[n_in:]
    if has_out:
        x = _out_proj_ln_rows(*ins[:5], outs.pop(0))
        ins = ins[5:]
    else:
        x = ins.pop(0)[...]
    if has_in:
        _in_proj_rows(x, *ins, *outs[:5], outs[5] if has_gate else None)


def _proj(xf, prev=None, nxt=None):
    m = xf.shape[0]
    row_d = pl.BlockSpec((TM_PROJ, D_MODEL), lambda i: (i, 0))
    row_e = pl.BlockSpec((TM_PROJ, MIX_WIDTH), lambda i: (i, 0))

    def whole(shape):
        return pl.BlockSpec(shape, lambda i: (0,) * len(shape))

    args, in_specs, out_shape, out_specs = [], [], [], []
    if prev is not None:
        g, w_out, gamma, beta = prev
        args += [g, xf, w_out, gamma.reshape(1, D_MODEL), beta.reshape(1, D_MODEL)]
        in_specs += [row_e, row_d, whole((MIX_WIDTH, D_MODEL)), whole((1, D_MODEL)), whole((1, D_MODEL))]
        out_shape.append(jax.ShapeDtypeStruct((m, D_MODEL), jnp.float32))
        out_specs.append(row_d)
    else:
        args.append(xf)
        in_specs.append(row_d)
    has_gate = False
    if nxt is not None:
        w, wvt = nxt
        has_gate = w.shape[1] > 3 * MIX_WIDTH
        hsel = (jnp.arange(MIX_WIDTH)[:, None] // HEAD_DIM == jnp.arange(LANES)[None, :]).astype(jnp.bfloat16)
        args += [w, wvt, hsel]
        in_specs += [whole(w.shape), whole(wvt.shape), whole(hsel.shape)]
        tok = jax.ShapeDtypeStruct((m, MIX_WIDTH), jnp.bfloat16)
        out_shape += [tok, tok, jax.ShapeDtypeStruct((MIX_WIDTH, m), jnp.bfloat16), tok,
                      jax.ShapeDtypeStruct((m // TM_PROJ, N_STATS, LANES), jnp.float32)]
        out_specs += [row_e, row_e, pl.BlockSpec((MIX_WIDTH, TM_PROJ), lambda i: (0, i)), row_e,
                      pl.BlockSpec((None, N_STATS, LANES), lambda i: (i, 0, 0))]
        if has_gate:
            out_shape.append(jax.ShapeDtypeStruct((m, LANES), jnp.float32))
            out_specs.append(pl.BlockSpec((TM_PROJ, LANES), lambda i: (i, 0)))
    name = (("out_" if prev is not None else "") + ("in_" if nxt is not None else "") + "proj"
            + ("_gate" if has_gate else ""))
    return pl.pallas_call(
        functools.partial(_proj_kernel, has_out=prev is not None, has_in=nxt is not None, has_gate=has_gate),
        grid=(m // TM_PROJ,),
        in_specs=in_specs,
        out_specs=out_specs,
        out_shape=out_shape,
        compiler_params=_params("arbitrary"),
        name=name,
    )(*args)


def _max_norms(nst, batch):
    nb = nst.shape[0] // batch
    n = jnp.sqrt(nst[:, :2, :N_HEADS]).reshape(batch, nb, 2, N_HEADS) * NORM_SLACK
    return n[:, :, 0], n[:, :, 1]


ACC_ROWS = HEAD_DIM + BF16_ROWS
U_ROWS = 8


def _head_masked_queries(q2):
    lane = lax.broadcasted_iota(jnp.int32, q2.shape, 1)
    zero = jnp.zeros_like(q2)
    return [jnp.where((lane >= HEAD_DIM * h) & (lane < HEAD_DIM * (h + 1)), q2, zero)
            for h in range(HEADS_PER_TILE)]


def _head_values(vt_blk, h):
    ones = jnp.ones((BF16_ROWS, vt_blk.shape[1]), vt_blk.dtype)
    return jnp.concatenate([vt_blk[HEAD_DIM * h:HEAD_DIM * (h + 1), :], ones], axis=0)


def _online_update(u_ref, acc_ref, c, a, vt_h):
    u = u_ref[c, 0:1, :]
    u_new = jnp.maximum(u, jnp.max(a, axis=0, keepdims=True))
    p = jnp.exp2(a - u_new).astype(jnp.bfloat16)
    acc_ref[c] = jnp.exp2(u - u_new) * acc_ref[c] + jnp.dot(vt_h, p, preferred_element_type=jnp.float32)
    u_ref[c, 0:1, :] = u_new


def _plain_update(u_ref, acc_ref, c, a, vt_h):
    del u_ref
    acc_ref[c] = acc_ref[c] + jnp.dot(vt_h, jnp.exp2(a).astype(jnp.bfloat16),
                                      preferred_element_type=jnp.float32)


def _reset(u_ref, acc_ref):
    u_ref[...] = jnp.full(u_ref.shape, NEG, jnp.float32)
    acc_ref[...] = jnp.zeros(acc_ref.shape, jnp.float32)


def _finish(acc_ref, z_ref, o_ref, r, row0):
    accs = [acc_ref[r * HEADS_PER_TILE + h] for h in range(HEADS_PER_TILE)]
    ot = jnp.concatenate([acc[:HEAD_DIM] / acc[HEAD_DIM:HEAD_DIM + 1] for acc in accs], axis=0)
    rows = pl.ds(row0, TQ)
    z = z_ref[rows, :].astype(jnp.float32)
    o_ref[rows, :] = (ot.T * (z / (1.0 + jnp.exp(-z)))).astype(o_ref.dtype)


def _attn_specs(seq, pair_major=False):
    def spec(block, index):
        if pair_major:
            return pl.BlockSpec(block, lambda hp, b, *_: index(b, hp))
        return pl.BlockSpec(block, lambda b, hp, *_: index(b, hp))

    seq_rows = spec((seq, LANES), lambda b, hp: (b, hp))
    seq_cols = spec((LANES, seq), lambda b, hp: (hp, b))
    return seq_rows, seq_cols, spec


def _state_scratch(q_sub):
    n = q_sub * HEADS_PER_TILE
    return [pltpu.VMEM((n, U_ROWS, TQ), jnp.float32), pltpu.VMEM((n, ACC_ROWS, TQ), jnp.float32)]


A_PAST = N_PAST_CHUNKS * CHUNK
A_WINDOW = A_PAST + TQ
A_BIAS_ROWS = 2 * A_PAST + TQ
A_BAND_LO = A_PAST - MAX_REL
A_BAND_HI = A_PAST + TQ + CHUNK


def _mixer_a_kernel(plain_ref, q_ref, k_ref, vt_ref, z_ref, bias_ref, o_ref, u_ref, acc_ref):
    plain = plain_ref[pl.program_id(1) * N_HEAD_PAIRS + pl.program_id(0)]
    n_groups = q_ref.shape[0] // (Q_SUB_A * TQ)

    def group(gi, update):
        _reset(u_ref, acc_ref)

        def qrow(r):
            return pl.multiple_of((gi * Q_SUB_A + r) * TQ, TQ)

        def window(r):
            start = qrow(r) - A_PAST
            row0 = pl.multiple_of(jnp.maximum(start, 0), TQ)
            return row0, pl.multiple_of(row0 - start, TQ)

        def issue(c):
            r, h = divmod(c, HEADS_PER_TILE)
            row0, brow = window(r)
            qm = _head_masked_queries(q_ref[pl.ds(qrow(r), TQ), :])[h]
            return _dot_nt(k_ref[pl.ds(row0, A_WINDOW), :], qm) + bias_ref[h, pl.ds(brow, A_WINDOW), :]

        def consume(c, a):
            r, h = divmod(c, HEADS_PER_TILE)
            row0, _ = window(r)
            update(u_ref, acc_ref, c, a, _head_values(vt_ref[:, pl.ds(row0, A_WINDOW)], h))
            if h == HEADS_PER_TILE - 1:
                _finish(acc_ref, z_ref, o_ref, r, qrow(r))

        _lagged(Q_SUB_A * HEADS_PER_TILE, issue, consume, ISSUE_LAG)

    def run(update):
        def body(gi, carry):
            group(gi, update)
            return carry
        lax.fori_loop(0, n_groups, body, 0)

    @pl.when(plain != 0)
    def _():
        run(_plain_update)

    @pl.when(plain == 0)
    def _():
        run(_online_update)


A_BAND = A_BAND_HI - A_BAND_LO
A_STRIP = 1024
A_STRIP_OFF = 512


def _bias_band_kernel(s_ref, o_ref):
    x = jnp.broadcast_to(s_ref[0], (A_BAND, A_STRIP))
    o_ref[0] = pltpu.roll(x, 0, 1, stride=1, stride_axis=0)[:, A_STRIP_OFF:A_STRIP_OFF + TQ]


def _mixer_a_bias(rel_bias):
    assert A_STRIP_OFF >= A_BAND - 1 and A_STRIP_OFF + TQ <= A_STRIP
    rb = rel_bias.astype(jnp.float32) * LOG2E
    c = jnp.arange(A_BIAS_ROWS)[:, None]
    r = jnp.arange(TQ)[None, :]
    strip_dist = jnp.arange(A_STRIP) - A_STRIP_OFF + (A_PAST - A_BAND_LO)
    strip = rb[:, jnp.clip(strip_dist, -MAX_REL, MAX_REL) + MAX_REL].reshape(N_HEADS, 1, A_STRIP)
    band = pl.pallas_call(
        _bias_band_kernel,
        grid=(N_HEADS,),
        in_specs=[pl.BlockSpec((1, 1, A_STRIP), lambda h: (h, 0, 0))],
        out_specs=pl.BlockSpec((1, A_BAND, TQ), lambda h: (h, 0, 0)),
        out_shape=jax.ShapeDtypeStruct((N_HEADS, A_BAND, TQ), jnp.float32),
        compiler_params=_params("arbitrary"),
        name="bias_band",
    )(strip)
    far = jnp.broadcast_to(rb[:, 2 * MAX_REL][:, None, None], (N_HEADS, A_BAND_LO, TQ))
    late = jnp.zeros((N_HEADS, A_BIAS_ROWS - A_BAND_HI, TQ), jnp.float32)
    table = jnp.concatenate([far, band, late], axis=1)
    dchunk = r // CHUNK - (c - A_PAST) // CHUNK
    visible = (dchunk >= 0) & (dchunk <= N_PAST_CHUNKS)
    return jnp.where(visible[None], table, NEG)


def _mixer_a_schedule(qn, kn, rel_bias):
    worst = jnp.max(qn, axis=1) * jnp.max(kn, axis=1) + jnp.max(jnp.abs(rel_bias), axis=-1)[None, :] * LOG2E
    ok = worst < PLAIN_EXP_LOG2
    return jnp.all(ok.reshape(-1, N_HEAD_PAIRS, HEADS_PER_TILE), axis=-1).reshape(-1).astype(jnp.int32)


def _mixer_a(plain, q, k, vt, z, bias, batch, seq):
    seq_rows, seq_cols, spec = _attn_specs(seq, pair_major=True)
    return pl.pallas_call(
        _mixer_a_kernel,
        grid_spec=pltpu.PrefetchScalarGridSpec(
            num_scalar_prefetch=1,
            grid=(N_HEAD_PAIRS, batch),
            in_specs=[seq_rows, seq_rows, seq_cols, seq_rows,
                      spec((HEADS_PER_TILE, A_BIAS_ROWS, TQ), lambda b, hp: (hp, 0, 0))],
            out_specs=seq_rows,
            scratch_shapes=_state_scratch(Q_SUB_A)),
        out_shape=jax.ShapeDtypeStruct(q.shape, jnp.bfloat16),
        compiler_params=_params("arbitrary", "arbitrary"),
        name="mixer_a",
    )(plain, q, k, vt, z, bias)


TG_B = Q_SUB_B * TQ
KV_PER_G = TG_B // TKV
N_AUG = HEADS_PER_TILE * N_SPLIT
AUG_STRIDE = LANES // N_HEAD_PAIRS


def _fox_prep_kernel(lf_ref, bf_ref, tri_ref, sel_ref, ones_ref, gk_ref, gq_ref, st_ref, carry_ref):
    @pl.when(pl.program_id(1) == 0)
    def _():
        carry_ref[...] = jnp.zeros_like(carry_ref)

    pre = lf_ref[...] + bf_ref[...]
    logf = (jnp.minimum(pre, 0.0) - jnp.log(1.0 + jnp.exp(-jnp.abs(pre)))) * LOG2E
    tri = tri_ref[...]
    csum = sum(jnp.dot(tri, piece, preferred_element_type=jnp.float32) for piece in _split(logf, N_SPLIT))
    f = csum + carry_ref[...]
    carry_ref[...] = f[TKV - 1:TKV, :]
    pieces = _split(-f, N_SPLIT)
    for side, o_ref in enumerate((gk_ref, gq_ref)):
        o_ref[...] = (sum(jnp.dot(piece, sel_ref[side, n], preferred_element_type=jnp.float32)
                          for n, piece in enumerate(pieces)) + ones_ref[side:side + 1, :]).astype(jnp.bfloat16)
    st_ref[...] = jnp.concatenate(
        [f[0:1, :], f[TKV - 1:TKV, :], jnp.zeros((N_STATS - 2, LANES), jnp.float32)], axis=0)


def _fox_prep(lf, b_f, batch, seq):
    nb = seq // TKV
    bf = jnp.pad(b_f.astype(jnp.float32), (0, LANES - N_HEADS)).reshape(1, LANES)
    tri = (jnp.arange(TKV)[:, None] >= jnp.arange(TKV)[None, :]).astype(jnp.bfloat16)
    src = jnp.arange(LANES)[:, None]
    dst = jnp.arange(LANES)[None, :]
    base = (src // HEADS_PER_TILE) * AUG_STRIDE + (src % HEADS_PER_TILE) * N_SPLIT
    sel = jnp.stack([
        jnp.stack([jnp.where((dst == base + side * N_AUG + n) & (src < N_HEADS), 1.0 - 2.0 * side, 0.0)
                   for n in range(N_SPLIT)]) for side in range(2)]).astype(jnp.bfloat16)
    lane = jnp.arange(LANES) % AUG_STRIDE
    ones = jnp.stack([(lane >= N_AUG) & (lane < 2 * N_AUG), lane < N_AUG]).astype(jnp.float32)
    tokspec = pl.BlockSpec((TKV, LANES), lambda b, i: (b * nb + i, 0))
    aug = jax.ShapeDtypeStruct((batch * seq, LANES), jnp.bfloat16)
    return pl.pallas_call(
        _fox_prep_kernel,
        grid=(batch, nb),
        in_specs=[tokspec,
                  pl.BlockSpec((1, LANES), lambda b, i: (0, 0)),
                  pl.BlockSpec((TKV, TKV), lambda b, i: (0, 0)),
                  pl.BlockSpec((2, N_SPLIT, LANES, LANES), lambda b, i: (0, 0, 0, 0)),
                  pl.BlockSpec((2, LANES), lambda b, i: (0, 0))],
        out_specs=[tokspec, tokspec, pl.BlockSpec((None, None, N_STATS, LANES), lambda b, i: (b, i, 0, 0))],
        out_shape=[aug, aug, jax.ShapeDtypeStruct((batch, nb, N_STATS, LANES), jnp.float32)],
        scratch_shapes=[pltpu.VMEM((1, LANES), jnp.float32)],
        compiler_params=_params("arbitrary", "arbitrary"),
        name="fox_prep",
    )(lf, bf, tri, sel, ones)


def _fox_schedule(qn, kn, f_stats):
    b, nb = f_stats.shape[0], f_stats.shape[1]
    ng = nb // KV_PER_G
    f_first, f_last = f_stats[:, :, 0, :N_HEADS], f_stats[:, :, 1, :N_HEADS]
    qn_g = jnp.max(qn.reshape(b, ng, KV_PER_G, N_HEADS), axis=2)
    kn_g = jnp.max(kn.reshape(b, ng, KV_PER_G, N_HEADS), axis=2)
    f_first_g = f_first.reshape(b, ng, KV_PER_G, N_HEADS)[:, :, 0]
    bound = (qn_g[:, :, None] * kn[:, None, :] + f_first_g[:, :, None] - f_last[:, None, :]
             + (qn_g * kn_g)[:, :, None])
    i_idx = jnp.arange(ng)[None, :, None, None]
    j_idx = jnp.arange(nb)[None, None, :, None]
    skip = (bound < -PRUNE_LOG2) & (j_idx < i_idx * KV_PER_G)
    first = jnp.min(jnp.where(skip, nb, j_idx), axis=2)
    first = jnp.min(first.reshape(b, ng, N_HEAD_PAIRS, HEADS_PER_TILE), axis=-1)
    first = jnp.transpose(first, (0, 2, 1)).reshape(-1)
    worst = jnp.max(qn, axis=1) * jnp.max(kn, axis=1)
    plain = jnp.all((worst < PLAIN_EXP_LOG2).reshape(b, N_HEAD_PAIRS, HEADS_PER_TILE), axis=-1).reshape(-1)
    return jnp.concatenate([first, plain.astype(first.dtype)]).astype(jnp.int32)


def _fox_kernel(sched_ref, q_ref, gq_ref, k_ref, gk_ref, vt_ref, z_ref, o_ref, u_ref, acc_ref):
    hp = pl.program_id(1)
    n_groups = q_ref.shape[0] // TG_B
    pair = pl.program_id(0) * N_HEAD_PAIRS + hp
    plain = sched_ref[pl.num_programs(0) * N_HEAD_PAIRS * n_groups + pair]
    n_chains = Q_SUB_B * HEADS_PER_TILE

    def keys_at(row0, n):
        rows = pl.ds(row0, n)
        return jnp.concatenate([k_ref[rows, :], gk_ref[rows, :]], axis=1), vt_ref[:, rows]

    def group(gi, update):
        j_first = sched_ref[pair * n_groups + gi]

        def qrow(r):
            return pl.multiple_of(gi * TG_B + r * TQ, TQ)

        def qaug(c):
            r, h = divmod(c, HEADS_PER_TILE)
            rows = pl.ds(qrow(r), TQ)
            gq = gq_ref[rows, :]
            off = lax.broadcasted_iota(jnp.int32, (TQ, LANES), 1) - hp * AUG_STRIDE
            own = ((off >= N_SPLIT * h) & (off < N_SPLIT * (h + 1))) | \
                  ((off >= N_AUG + N_SPLIT * h) & (off < N_AUG + N_SPLIT * (h + 1)))
            return jnp.concatenate([_head_masked_queries(q_ref[rows, :])[h],
                                    jnp.where(own, gq, jnp.zeros_like(gq))], axis=1)

        _reset(u_ref, acc_ref)
        qa = [qaug(c) for c in range(n_chains)]

        def keys_block(j, n_steps):
            kaug, vt_blk = keys_at(pl.multiple_of(j * TKV, TKV), n_steps * TKV)
            _lagged(n_chains,
                    lambda c: _dot_nt(kaug, qa[c]),
                    lambda c, a: update(u_ref, acc_ref, c, a, _head_values(vt_blk, c % HEADS_PER_TILE)),
                    ISSUE_LAG)

        n_before = gi * KV_PER_G - j_first
        odd = n_before & 1

        def two_steps(p, carry):
            keys_block(j_first + odd + 2 * p, 2)
            return carry

        lax.fori_loop(0, lax.shift_right_logical(n_before, 1), two_steps, 0)

        def own_block(with_step):
            kaug, vt_blk = keys_at(pl.multiple_of(gi * TG_B, TG_B), TG_B)
            s_loc = lax.broadcasted_iota(jnp.int32, (TQ, TQ), 0)
            t_loc = lax.broadcasted_iota(jnp.int32, (TQ, TQ), 1)
            n_extra = n_chains if with_step else 0
            if with_step:
                kaug_e, vt_e = keys_at(pl.multiple_of(j_first * TKV, TKV), TKV)

            def issue(i):
                if i < n_extra:
                    return _dot_nt(kaug_e, qa[i])
                c = i - n_extra
                r = c // HEADS_PER_TILE
                a = _dot_nt(kaug[:(r + 1) * TQ], qa[c])
                diag = jnp.where(s_loc <= t_loc, a[r * TQ:], NEG)
                return diag if r == 0 else jnp.concatenate([a[:r * TQ], diag], axis=0)

            def consume(i, a):
                if i < n_extra:
                    update(u_ref, acc_ref, i, a, _head_values(vt_e, i % HEADS_PER_TILE))
                    return
                c = i - n_extra
                r, h = divmod(c, HEADS_PER_TILE)
                update(u_ref, acc_ref, c, a, _head_values(vt_blk[:, :(r + 1) * TQ], h))
                if h == HEADS_PER_TILE - 1:
                    _finish(acc_ref, z_ref, o_ref, r, qrow(r))

            _lagged(n_extra + n_chains, issue, consume, ISSUE_LAG)

        @pl.when(odd != 0)
        def _():
            own_block(True)

        @pl.when(odd == 0)
        def _():
            own_block(False)

    def run(update):
        def body(gi, carry):
            group(gi, update)
            return carry
        lax.fori_loop(0, n_groups, body, 0)

    @pl.when(plain != 0)
    def _():
        run(_plain_update)

    @pl.when(plain == 0)
    def _():
        run(_online_update)


def _fox(sched, q, gq, k, gk, vt, z, batch, seq):
    seq_rows, seq_cols, spec = _attn_specs(seq)
    aug_seq = spec((seq, LANES), lambda b, hp: (b, 0))
    return pl.pallas_call(
        _fox_kernel,
        grid_spec=pltpu.PrefetchScalarGridSpec(
            num_scalar_prefetch=1,
            grid=(batch, N_HEAD_PAIRS),
            in_specs=[seq_rows, aug_seq, seq_rows, aug_seq, seq_cols, seq_rows],
            out_specs=seq_rows,
            scratch_shapes=_state_scratch(Q_SUB_B)),
        out_shape=jax.ShapeDtypeStruct(q.shape, jnp.bfloat16),
        compiler_params=_params("arbitrary", "arbitrary"),
        name="fox",
    )(sched, q, gq, k, gk, vt, z)


def _in_weights(w_in, w_gate=None):
    w_in = w_in.astype(jnp.float32)
    e = MIX_WIDTH
    cols = [w_in[:, :e] * (LOG2E / math.sqrt(HEAD_DIM)), w_in[:, e:2 * e], w_in[:, 3 * e:]]
    if w_gate is not None:
        cols.append(jnp.pad(w_gate.astype(jnp.float32), ((0, 0), (0, LANES - N_HEADS))))
    return jnp.concatenate(cols, axis=1).astype(jnp.bfloat16), w_in[:, 2 * e:3 * e].T.astype(jnp.bfloat16)


def _reorder_heads(order, w_in, w_f, b_f, w_out):
    d = w_in.shape[0]
    w_in = w_in.reshape(d, 4, N_HEADS, HEAD_DIM)[:, :, order].reshape(d, 4 * MIX_WIDTH)
    w_out = w_out.reshape(N_HEADS, HEAD_DIM, -1)[order].reshape(MIX_WIDTH, -1)
    return w_in, w_f[:, order], b_f[order], w_out


def kernel(x, w_in_a, rel_bias_a, w_out_a, w_in_b, w_f_b, b_f_b, w_out_b, ln_g, ln_b):
    batch, seq, d = x.shape
    assert seq % (Q_SUB_A * TQ) == 0 and seq % TG_B == 0 and TG_B % TKV == 0 and TKV == TM_PROJ
    assert seq >= A_WINDOW and N_HEAD_PAIRS * AUG_STRIDE == LANES and 2 * N_AUG <= AUG_STRIDE
    xf = x.reshape(batch * seq, d).astype(jnp.float32)

    layers = []
    for i in range(DEPTH):
        j = i // N_MIXERS
        if i % N_MIXERS == 0:
            layers.append((_in_weights(w_in_a[j]), w_out_a[j].astype(jnp.bfloat16), None))
        else:
            order = jnp.argsort(b_f_b[j])
            w_in, w_f, b_f, w_out = _reorder_heads(order, w_in_b[j], w_f_b[j], b_f_b[j], w_out_b[j])
            layers.append((_in_weights(w_in, w_f), w_out.astype(jnp.bfloat16), b_f))

    outs = _proj(xf, nxt=layers[0][0])
    for i in range(DEPTH):
        _, w_out, b_f = layers[i]
        j = i // N_MIXERS
        if b_f is None:
            q, k, vt, z, nst = outs
            plain = _mixer_a_schedule(*_max_norms(nst, batch), rel_bias_a[j])
            g = _mixer_a(plain, q, k, vt, z, _mixer_a_bias(rel_bias_a[j]), batch, seq)
        else:
            q, k, vt, z, nst, lf = outs
            gk, gq, f_stats = _fox_prep(lf, b_f, batch, seq)
            g = _fox(_fox_schedule(*_max_norms(nst, batch), f_stats), q, gq, k, gk, vt, z, batch, seq)
        xf, *outs = _proj(xf, prev=(g, w_out, ln_g[i], ln_b[i]),
                          nxt=layers[i + 1][0] if i + 1 < DEPTH else None)
    return xf.reshape(batch, seq, d).astype(x.dtype)
```

```python
import functools
import math

import jax
import jax.numpy as jnp
from jax import lax
from jax.experimental import pallas as pl
from jax.experimental.pallas import tpu as pltpu

D_MODEL = 1024
N_HEADS = 16
HEAD_DIM = 64
MIX_WIDTH = N_HEADS * HEAD_DIM
CHUNK = 64
N_PAST_CHUNKS = 8
MAX_REL = 128
DEPTH = 4
N_MIXERS = 2
LN_EPS = 1e-5
DEEPNORM_ALPHA = (2.0 * DEPTH) ** 0.25
LOG2E = math.log2(math.e)
NEG = -1e30

LANES = 128
HEADS_PER_TILE = LANES // HEAD_DIM
N_HEAD_PAIRS = N_HEADS // HEADS_PER_TILE
BF16_ROWS = 16
N_SPLIT = 3
TQ = 256
Q_SUB_A = 8
Q_SUB_B = 4
TKV = 512
TM_PROJ = 512
ISSUE_LAG = 8
VMEM_LIMIT = 56 * 1024 * 1024
PRUNE_LOG2 = 152.0
NORM_SLACK = 1.0 + 2.0 ** -6
PLAIN_EXP_LOG2 = 60.0


def _params(*semantics):
    return pltpu.CompilerParams(dimension_semantics=semantics, vmem_limit_bytes=VMEM_LIMIT)


def _dot_nt(a, b):
    return lax.dot_general(a, b, (((1,), (1,)), ((), ())), preferred_element_type=jnp.float32)


def _split(x, n):
    pieces = []
    for _ in range(n):
        p = x.astype(jnp.bfloat16)
        pieces.append(p)
        x = x - p.astype(jnp.float32)
    return pieces


N_STATS = 8
LN_ROW_CHUNKS = 2


def _lagged(n, issue, consume, lag):
    pending = {}
    for c in range(n + lag):
        if c < n:
            pending[c] = issue(c)
        if c >= lag:
            consume(c - lag, pending.pop(c - lag))


def _out_proj_ln_rows(g_ref, x_ref, wo_ref, gam_ref, bet_ref, xo_ref):
    rows = TM_PROJ // LN_ROW_CHUNKS
    out = []

    def issue(c):
        return jnp.dot(g_ref[pl.ds(c * rows, rows), :], wo_ref[...], preferred_element_type=jnp.float32)

    def consume(c, y):
        sl = pl.ds(c * rows, rows)
        r = DEEPNORM_ALPHA * x_ref[sl, :] + y
        mu = jnp.mean(r, axis=-1, keepdims=True)
        d = r - mu
        var = jnp.mean(d * d, axis=-1, keepdims=True)
        xn = d * lax.rsqrt(var + LN_EPS) * gam_ref[...] + bet_ref[...]
        xo_ref[sl, :] = xn
        out.append(xn)

    _lagged(LN_ROW_CHUNKS, issue, consume, lag=1)
    return jnp.concatenate(out, axis=0)


def _in_proj_rows(x, w_ref, wvt_ref, hsel_ref, q_ref, k_ref, vt_ref, z_ref, nst_ref, gate_ref):
    xb = x.astype(jnp.bfloat16)
    sq_max = []
    for g, o_ref in enumerate((q_ref, k_ref, z_ref)):
        o = jnp.dot(xb, w_ref[:, g * MIX_WIDTH:(g + 1) * MIX_WIDTH], preferred_element_type=jnp.float32)
        if g == 2:
            o = o / (1.0 + jnp.exp(-o))
        o = o.astype(jnp.bfloat16)
        o_ref[...] = o
        if g < 2:
            of = o.astype(jnp.float32)
            sq = jnp.dot((of * of).astype(jnp.bfloat16), hsel_ref[...], preferred_element_type=jnp.float32)
            sq_max.append(jnp.max(sq, axis=0, keepdims=True))
    nst_ref[...] = jnp.concatenate(sq_max + [jnp.zeros((N_STATS - 2, LANES), jnp.float32)], axis=0)
    vt_ref[...] = _dot_nt(wvt_ref[...], xb).astype(jnp.bfloat16)
    if gate_ref is not None:
        gate_ref[...] = jnp.dot(xb, w_ref[:, 3 * MIX_WIDTH:], preferred_element_type=jnp.float32)


def _proj_kernel(*refs, has_out, has_in, has_gate):
    refs = list(refs)
    n_in = (5 if has_out else 1) + (3 if has_in else 0)
    ins, outs = refs[:n_in], refs[n_in:]
    if has_out:
        x = _out_proj_ln_rows(*ins[:5], outs.pop(0))
        ins = ins[5:]
    else:
        x = ins.pop(0)[...]
    if has_in:
        _in_proj_rows(x, *ins, *outs[:5], outs[5] if has_gate else None)


def _proj(xf, prev=None, nxt=None):
    m = xf.shape[0]
    row_d = pl.BlockSpec((TM_PROJ, D_MODEL), lambda i: (i, 0))
    row_e = pl.BlockSpec((TM_PROJ, MIX_WIDTH), lambda i: (i, 0))

    def whole(shape):
        return pl.BlockSpec(shape, lambda i: (0,) * len(shape))

    args, in_specs, out_shape, out_specs = [], [], [], []
    if prev is not None:
        g, w_out, gamma, beta = prev
        args += [g, xf, w_out, gamma.reshape(1, D_MODEL), beta.reshape(1, D_MODEL)]
        in_specs += [row_e, row_d, whole((MIX_WIDTH, D_MODEL)), whole((1, D_MODEL)), whole((1, D_MODEL))]
        out_shape.append(jax.ShapeDtypeStruct((m, D_MODEL), jnp.float32))
        out_specs.append(row_d)
    else:
        args.append(xf)
        in_specs.append(row_d)
    has_gate = False
    if nxt is not None:
        w, wvt = nxt
        has_gate = w.shape[1] > 3 * MIX_WIDTH
        hsel = (jnp.arange(MIX_WIDTH)[:, None] // HEAD_DIM == jnp.arange(LANES)[None, :]).astype(jnp.bfloat16)
        args += [w, wvt, hsel]
        in_specs += [whole(w.shape), whole(wvt.shape), whole(hsel.shape)]
        tok = jax.ShapeDtypeStruct((m, MIX_WIDTH), jnp.bfloat16)
        out_shape += [tok, tok, jax.ShapeDtypeStruct((MIX_WIDTH, m), jnp.bfloat16), tok,
                      jax.ShapeDtypeStruct((m // TM_PROJ, N_STATS, LANES), jnp.float32)]
        out_specs += [row_e, row_e, pl.BlockSpec((MIX_WIDTH, TM_PROJ), lambda i: (0, i)), row_e,
                      pl.BlockSpec((None, N_STATS, LANES), lambda i: (i, 0, 0))]
        if has_gate:
            out_shape.append(jax.ShapeDtypeStruct((m, LANES), jnp.float32))
            out_specs.append(pl.BlockSpec((TM_PROJ, LANES), lambda i: (i, 0)))
    name = (("out_" if prev is not None else "") + ("in_" if nxt is not None else "") + "proj"
            + ("_gate" if has_gate else ""))
    return pl.pallas_call(
        functools.partial(_proj_kernel, has_out=prev is not None, has_in=nxt is not None, has_gate=has_gate),
        grid=(m // TM_PROJ,),
        in_specs=in_specs,
        out_specs=out_specs,
        out_shape=out_shape,
        compiler_params=_params("arbitrary"),
        name=name,
    )(*args)


def _max_norms(nst, batch):
    nb = nst.shape[0] // batch
    n = jnp.sqrt(nst[:, :2, :N_HEADS]).reshape(batch, nb, 2, N_HEADS) * NORM_SLACK
    return n[:, :, 0], n[:, :, 1]


ACC_ROWS = HEAD_DIM + BF16_ROWS
U_ROWS = 8


def _head_masked_queries(q2):
    lane = lax.broadcasted_iota(jnp.int32, q2.shape, 1)
    zero = jnp.zeros_like(q2)
    return [jnp.where((lane >= HEAD_DIM * h) & (lane < HEAD_DIM * (h + 1)), q2, zero)
            for h in range(HEADS_PER_TILE)]


def _head_values(vt_blk, h):
    ones = jnp.ones((BF16_ROWS, vt_blk.shape[1]), vt_blk.dtype)
    return jnp.concatenate([vt_blk[HEAD_DIM * h:HEAD_DIM * (h + 1), :], ones], axis=0)


def _online_update(u_ref, acc_ref, c, a, vt_h):
    u = u_ref[c, 0:1, :]
    u_new = jnp.maximum(u, jnp.max(a, axis=0, keepdims=True))
    p = jnp.exp2(a - u_new).astype(jnp.bfloat16)
    acc_ref[c] = jnp.exp2(u - u_new) * acc_ref[c] + jnp.dot(vt_h, p, preferred_element_type=jnp.float32)
    u_ref[c, 0:1, :] = u_new


def _plain_update(u_ref, acc_ref, c, a, vt_h):
    del u_ref
    acc_ref[c] = acc_ref[c] + jnp.dot(vt_h, jnp.exp2(a).astype(jnp.bfloat16),
                                      preferred_element_type=jnp.float32)


def _reset(u_ref, acc_ref):
    u_ref[...] = jnp.full(u_ref.shape, NEG, jnp.float32)
    acc_ref[...] = jnp.zeros(acc_ref.shape, jnp.float32)


def _finish(acc_ref, z_ref, o_ref, r, row0):
    accs = [acc_ref[r * HEADS_PER_TILE + h] for h in range(HEADS_PER_TILE)]
    ot = jnp.concatenate([acc[:HEAD_DIM] / acc[HEAD_DIM:HEAD_DIM + 1] for acc in accs], axis=0)
    rows = pl.ds(row0, TQ)
    o_ref[rows, :] = (ot.T * z_ref[rows, :].astype(jnp.float32)).astype(o_ref.dtype)


def _attn_specs(seq, pair_major=False):
    def spec(block, index):
        if pair_major:
            return pl.BlockSpec(block, lambda hp, b, *_: index(b, hp))
        return pl.BlockSpec(block, lambda b, hp, *_: index(b, hp))

    seq_rows = spec((seq, LANES), lambda b, hp: (b, hp))
    seq_cols = spec((LANES, seq), lambda b, hp: (hp, b))
    return seq_rows, seq_cols, spec


def _state_scratch(q_sub):
    n = q_sub * HEADS_PER_TILE
    return [pltpu.VMEM((n, U_ROWS, TQ), jnp.float32), pltpu.VMEM((n, ACC_ROWS, TQ), jnp.float32)]


A_PAST = N_PAST_CHUNKS * CHUNK
A_WINDOW = A_PAST + TQ
A_BIAS_ROWS = 2 * A_PAST + TQ
A_BAND_LO = A_PAST - MAX_REL
A_BAND_HI = A_PAST + TQ + CHUNK


def _mixer_a_kernel(plain_ref, q_ref, k_ref, vt_ref, z_ref, bias_ref, o_ref, u_ref, acc_ref):
    plain = plain_ref[pl.program_id(1) * N_HEAD_PAIRS + pl.program_id(0)]
    n_groups = q_ref.shape[0] // (Q_SUB_A * TQ)

    def group(gi, update):
        _reset(u_ref, acc_ref)

        def qrow(r):
            return pl.multiple_of((gi * Q_SUB_A + r) * TQ, TQ)

        def window(r):
            start = qrow(r) - A_PAST
            row0 = pl.multiple_of(jnp.maximum(start, 0), TQ)
            return row0, pl.multiple_of(row0 - start, TQ)

        def issue(c):
            r, h = divmod(c, HEADS_PER_TILE)
            row0, brow = window(r)
            qm = _head_masked_queries(q_ref[pl.ds(qrow(r), TQ), :])[h]
            return _dot_nt(k_ref[pl.ds(row0, A_WINDOW), :], qm) + bias_ref[h, pl.ds(brow, A_WINDOW), :]

        def consume(c, a):
            r, h = divmod(c, HEADS_PER_TILE)
            row0, _ = window(r)
            update(u_ref, acc_ref, c, a, _head_values(vt_ref[:, pl.ds(row0, A_WINDOW)], h))
            if h == HEADS_PER_TILE - 1:
                _finish(acc_ref, z_ref, o_ref, r, qrow(r))

        _lagged(Q_SUB_A * HEADS_PER_TILE, issue, consume, ISSUE_LAG)

    def run(update):
        def body(gi, carry):
            group(gi, update)
            return carry
        lax.fori_loop(0, n_groups, body, 0)

    @pl.when(plain != 0)
    def _():
        run(_plain_update)

    @pl.when(plain == 0)
    def _():
        run(_online_update)


A_BAND = A_BAND_HI - A_BAND_LO
A_STRIP = 1024
A_STRIP_OFF = 512


def _bias_band_kernel(s_ref, o_ref):
    x = jnp.broadcast_to(s_ref[0], (A_BAND, A_STRIP))
    o_ref[0] = pltpu.roll(x, 0, 1, stride=1, stride_axis=0)[:, A_STRIP_OFF:A_STRIP_OFF + TQ]


def _mixer_a_bias(rel_bias):
    assert A_STRIP_OFF >= A_BAND - 1 and A_STRIP_OFF + TQ <= A_STRIP
    rb = rel_bias.astype(jnp.float32) * LOG2E
    c = jnp.arange(A_BIAS_ROWS)[:, None]
    r = jnp.arange(TQ)[None, :]
    strip_dist = jnp.arange(A_STRIP) - A_STRIP_OFF + (A_PAST - A_BAND_LO)
    strip = rb[:, jnp.clip(strip_dist, -MAX_REL, MAX_REL) + MAX_REL].reshape(N_HEADS, 1, A_STRIP)
    band = pl.pallas_call(
        _bias_band_kernel,
        grid=(N_HEADS,),
        in_specs=[pl.BlockSpec((1, 1, A_STRIP), lambda h: (h, 0, 0))],
        out_specs=pl.BlockSpec((1, A_BAND, TQ), lambda h: (h, 0, 0)),
        out_shape=jax.ShapeDtypeStruct((N_HEADS, A_BAND, TQ), jnp.float32),
        compiler_params=_params("arbitrary"),
        name="bias_band",
    )(strip)
    far = jnp.broadcast_to(rb[:, 2 * MAX_REL][:, None, None], (N_HEADS, A_BAND_LO, TQ))
    late = jnp.zeros((N_HEADS, A_BIAS_ROWS - A_BAND_HI, TQ), jnp.float32)
    table = jnp.concatenate([far, band, late], axis=1)
    dchunk = r // CHUNK - (c - A_PAST) // CHUNK
    visible = (dchunk >= 0) & (dchunk <= N_PAST_CHUNKS)
    return jnp.where(visible[None], table, NEG)


def _mixer_a_schedule(qn, kn, rel_bias):
    worst = jnp.max(qn, axis=1) * jnp.max(kn, axis=1) + jnp.max(jnp.abs(rel_bias), axis=-1)[None, :] * LOG2E
    ok = worst < PLAIN_EXP_LOG2
    return jnp.all(ok.reshape(-1, N_HEAD_PAIRS, HEADS_PER_TILE), axis=-1).reshape(-1).astype(jnp.int32)


def _mixer_a(plain, q, k, vt, z, bias, batch, seq):
    seq_rows, seq_cols, spec = _attn_specs(seq, pair_major=True)
    return pl.pallas_call(
        _mixer_a_kernel,
        grid_spec=pltpu.PrefetchScalarGridSpec(
            num_scalar_prefetch=1,
            grid=(N_HEAD_PAIRS, batch),
            in_specs=[seq_rows, seq_rows, seq_cols, seq_rows,
                      spec((HEADS_PER_TILE, A_BIAS_ROWS, TQ), lambda b, hp: (hp, 0, 0))],
            out_specs=seq_rows,
            scratch_shapes=_state_scratch(Q_SUB_A)),
        out_shape=jax.ShapeDtypeStruct(q.shape, jnp.bfloat16),
        compiler_params=_params("arbitrary", "arbitrary"),
        name="mixer_a",
    )(plain, q, k, vt, z, bias)


TG_B = Q_SUB_B * TQ
KV_PER_G = TG_B // TKV
N_AUG = HEADS_PER_TILE * N_SPLIT
AUG_STRIDE = LANES // N_HEAD_PAIRS


def _fox_prep_kernel(lf_ref, bf_ref, tri_ref, sel_ref, ones_ref, gk_ref, gq_ref, st_ref, carry_ref):
    @pl.when(pl.program_id(1) == 0)
    def _():
        carry_ref[...] = jnp.zeros_like(carry_ref)

    pre = lf_ref[...] + bf_ref[...]
    logf = (jnp.minimum(pre, 0.0) - jnp.log(1.0 + jnp.exp(-jnp.abs(pre)))) * LOG2E
    parts = jnp.dot(tri_ref[...], jnp.concatenate(_split(logf, N_SPLIT), axis=1),
                    preferred_element_type=jnp.float32)
    csum = sum(parts[:, n * LANES:(n + 1) * LANES] for n in range(N_SPLIT))
    f = csum + carry_ref[...]
    carry_ref[...] = f[TKV - 1:TKV, :]
    g2 = jnp.dot(jnp.concatenate(_split(-f, N_SPLIT), axis=1), sel_ref[...],
                 preferred_element_type=jnp.float32) + ones_ref[...]
    gk_ref[...] = g2[:, :LANES].astype(jnp.bfloat16)
    gq_ref[...] = g2[:, LANES:].astype(jnp.bfloat16)
    st_ref[...] = jnp.concatenate(
        [f[0:1, :], f[TKV - 1:TKV, :], jnp.zeros((N_STATS - 2, LANES), jnp.float32)], axis=0)


def _fox_prep(lf, b_f, batch, seq):
    nb = seq // TKV
    bf = jnp.pad(b_f.astype(jnp.float32), (0, LANES - N_HEADS)).reshape(1, LANES)
    tri = (jnp.arange(TKV)[:, None] >= jnp.arange(TKV)[None, :]).astype(jnp.bfloat16)
    src = jnp.arange(LANES)[:, None]
    dst = jnp.arange(LANES)[None, :]
    base = (src // HEADS_PER_TILE) * AUG_STRIDE + (src % HEADS_PER_TILE) * N_SPLIT
    sel = jnp.concatenate([
        jnp.concatenate([jnp.where((dst == base + side * N_AUG + n) & (src < N_HEADS), 1.0 - 2.0 * side, 0.0)
                         for side in range(2)], axis=1) for n in range(N_SPLIT)], axis=0).astype(jnp.bfloat16)
    lane = jnp.arange(LANES) % AUG_STRIDE
    ones = jnp.concatenate([(lane >= N_AUG) & (lane < 2 * N_AUG), lane < N_AUG]).astype(jnp.float32)
    ones = ones.reshape(1, 2 * LANES)
    tokspec = pl.BlockSpec((TKV, LANES), lambda b, i: (b * nb + i, 0))
    aug = jax.ShapeDtypeStruct((batch * seq, LANES), jnp.bfloat16)
    return pl.pallas_call(
        _fox_prep_kernel,
        grid=(batch, nb),
        in_specs=[tokspec,
                  pl.BlockSpec((1, LANES), lambda b, i: (0, 0)),
                  pl.BlockSpec((TKV, TKV), lambda b, i: (0, 0)),
                  pl.BlockSpec((N_SPLIT * LANES, 2 * LANES), lambda b, i: (0, 0)),
                  pl.BlockSpec((1, 2 * LANES), lambda b, i: (0, 0))],
        out_specs=[tokspec, tokspec, pl.BlockSpec((None, None, N_STATS, LANES), lambda b, i: (b, i, 0, 0))],
        out_shape=[aug, aug, jax.ShapeDtypeStruct((batch, nb, N_STATS, LANES), jnp.float32)],
        scratch_shapes=[pltpu.VMEM((1, LANES), jnp.float32)],
        compiler_params=_params("arbitrary", "arbitrary"),
        name="fox_prep",
    )(lf, bf, tri, sel, ones)


def _fox_schedule(qn, kn, f_stats):
    b, nb = f_stats.shape[0], f_stats.shape[1]
    ng = nb // KV_PER_G
    f_first, f_last = f_stats[:, :, 0, :N_HEADS], f_stats[:, :, 1, :N_HEADS]
    qn_g = jnp.max(qn.reshape(b, ng, KV_PER_G, N_HEADS), axis=2)
    kn_g = jnp.max(kn.reshape(b, ng, KV_PER_G, N_HEADS), axis=2)
    f_first_g = f_first.reshape(b, ng, KV_PER_G, N_HEADS)[:, :, 0]
    bound = (qn_g[:, :, None] * kn[:, None, :] + f_first_g[:, :, None] - f_last[:, None, :]
             + (qn_g * kn_g)[:, :, None])
    i_idx = jnp.arange(ng)[None, :, None, None]
    j_idx = jnp.arange(nb)[None, None, :, None]
    skip = (bound < -PRUNE_LOG2) & (j_idx < i_idx * KV_PER_G)
    first = jnp.min(jnp.where(skip, nb, j_idx), axis=2)
    first = jnp.min(first.reshape(b, ng, N_HEAD_PAIRS, HEADS_PER_TILE), axis=-1)
    first = jnp.transpose(first, (0, 2, 1)).reshape(-1)
    worst = jnp.max(qn, axis=1) * jnp.max(kn, axis=1)
    plain = jnp.all((worst < PLAIN_EXP_LOG2).reshape(b, N_HEAD_PAIRS, HEADS_PER_TILE), axis=-1).reshape(-1)
    return jnp.concatenate([first, plain.astype(first.dtype)]).astype(jnp.int32)


def _fox_kernel(sched_ref, q_ref, gq_ref, k_ref, gk_ref, vt_ref, z_ref, o_ref, u_ref, acc_ref):
    hp = pl.program_id(1)
    n_groups = q_ref.shape[0] // TG_B
    pair = pl.program_id(0) * N_HEAD_PAIRS + hp
    plain = sched_ref[pl.num_programs(0) * N_HEAD_PAIRS * n_groups + pair]
    n_chains = Q_SUB_B * HEADS_PER_TILE

    def keys_at(row0, n):
        rows = pl.ds(row0, n)
        return jnp.concatenate([k_ref[rows, :], gk_ref[rows, :]], axis=1), vt_ref[:, rows]

    def group(gi, update):
        j_first = sched_ref[pair * n_groups + gi]

        def qrow(r):
            return pl.multiple_of(gi * TG_B + r * TQ, TQ)

        def qaug(c):
            r, h = divmod(c, HEADS_PER_TILE)
            rows = pl.ds(qrow(r), TQ)
            gq = gq_ref[rows, :]
            off = lax.broadcasted_iota(jnp.int32, (TQ, LANES), 1) - hp * AUG_STRIDE
            own = ((off >= N_SPLIT * h) & (off < N_SPLIT * (h + 1))) | \
                  ((off >= N_AUG + N_SPLIT * h) & (off < N_AUG + N_SPLIT * (h + 1)))
            return jnp.concatenate([_head_masked_queries(q_ref[rows, :])[h],
                                    jnp.where(own, gq, jnp.zeros_like(gq))], axis=1)

        _reset(u_ref, acc_ref)
        qa = [qaug(c) for c in range(n_chains)]

        def keys_block(j, n_steps):
            kaug, vt_blk = keys_at(pl.multiple_of(j * TKV, TKV), n_steps * TKV)
            _lagged(n_chains,
                    lambda c: _dot_nt(kaug, qa[c]),
                    lambda c, a: update(u_ref, acc_ref, c, a, _head_values(vt_blk, c % HEADS_PER_TILE)),
                    ISSUE_LAG)

        n_before = gi * KV_PER_G - j_first
        odd = n_before & 1

        def two_steps(p, carry):
            keys_block(j_first + odd + 2 * p, 2)
            return carry

        lax.fori_loop(0, lax.shift_right_logical(n_before, 1), two_steps, 0)

        def own_block(with_step):
            kaug, vt_blk = keys_at(pl.multiple_of(gi * TG_B, TG_B), TG_B)
            s_loc = lax.broadcasted_iota(jnp.int32, (TQ, TQ), 0)
            t_loc = lax.broadcasted_iota(jnp.int32, (TQ, TQ), 1)
            n_extra = n_chains if with_step else 0
            if with_step:
                kaug_e, vt_e = keys_at(pl.multiple_of(j_first * TKV, TKV), TKV)

            def issue(i):
                if i < n_extra:
                    return _dot_nt(kaug_e, qa[i])
                c = i - n_extra
                r = c // HEADS_PER_TILE
                a = _dot_nt(kaug[:(r + 1) * TQ], qa[c])
                diag = jnp.where(s_loc <= t_loc, a[r * TQ:], NEG)
                return diag if r == 0 else jnp.concatenate([a[:r * TQ], diag], axis=0)

            def consume(i, a):
                if i < n_extra:
                    update(u_ref, acc_ref, i, a, _head_values(vt_e, i % HEADS_PER_TILE))
                    return
                c = i - n_extra
                r, h = divmod(c, HEADS_PER_TILE)
                update(u_ref, acc_ref, c, a, _head_values(vt_blk[:, :(r + 1) * TQ], h))
                if h == HEADS_PER_TILE - 1:
                    _finish(acc_ref, z_ref, o_ref, r, qrow(r))

            _lagged(n_extra + n_chains, issue, consume, ISSUE_LAG)

        @pl.when(odd != 0)
        def _():
            own_block(True)

        @pl.when(odd == 0)
        def _():
            own_block(False)

    def run(update):
        def body(gi, carry):
            group(gi, update)
            return carry
        lax.fori_loop(0, n_groups, body, 0)

    @pl.when(plain != 0)
    def _():
        run(_plain_update)

    @pl.when(plain == 0)
    def _():
        run(_online_update)


def _fox(sched, q, gq, k, gk, vt, z, batch, seq):
    seq_rows, seq_cols, spec = _attn_specs(seq)
    aug_seq = spec((seq, LANES), lambda b, hp: (b, 0))
    return pl.pallas_call(
        _fox_kernel,
        grid_spec=pltpu.PrefetchScalarGridSpec(
            num_scalar_prefetch=1,
            grid=(batch, N_HEAD_PAIRS),
            in_specs=[seq_rows, aug_seq, seq_rows, aug_seq, seq_cols, seq_rows],
            out_specs=seq_rows,
            scratch_shapes=_state_scratch(Q_SUB_B)),
        out_shape=jax.ShapeDtypeStruct(q.shape, jnp.bfloat16),
        compiler_params=_params("arbitrary", "arbitrary"),
        name="fox",
    )(sched, q, gq, k, gk, vt, z)


def _in_weights(w_in, w_gate=None):
    w_in = w_in.astype(jnp.float32)
    e = MIX_WIDTH
    cols = [w_in[:, :e] * (LOG2E / math.sqrt(HEAD_DIM)), w_in[:, e:2 * e], w_in[:, 3 * e:]]
    if w_gate is not None:
        cols.append(jnp.pad(w_gate.astype(jnp.float32), ((0, 0), (0, LANES - N_HEADS))))
    return jnp.concatenate(cols, axis=1).astype(jnp.bfloat16), w_in[:, 2 * e:3 * e].T.astype(jnp.bfloat16)


def _reorder_heads(order, w_in, w_f, b_f, w_out):
    d = w_in.shape[0]
    w_in = w_in.reshape(d, 4, N_HEADS, HEAD_DIM)[:, :, order].reshape(d, 4 * MIX_WIDTH)
    w_out = w_out.reshape(N_HEADS, HEAD_DIM, -1)[order].reshape(MIX_WIDTH, -1)
    return w_in, w_f[:, order], b_f[order], w_out


def kernel(x, w_in_a, rel_bias_a, w_out_a, w_in_b, w_f_b, b_f_b, w_out_b, ln_g, ln_b):
    batch, seq, d = x.shape
    assert seq % (Q_SUB_A * TQ) == 0 and seq % TG_B == 0 and TG_B % TKV == 0 and TKV == TM_PROJ
    assert seq >= A_WINDOW and N_HEAD_PAIRS * AUG_STRIDE == LANES and 2 * N_AUG <= AUG_STRIDE
    xf = x.reshape(batch * seq, d).astype(jnp.float32)

    layers = []
    for i in range(DEPTH):
        j = i // N_MIXERS
        if i % N_MIXERS == 0:
            layers.append((_in_weights(w_in_a[j]), w_out_a[j].astype(jnp.bfloat16), None))
        else:
            order = jnp.argsort(b_f_b[j])
            w_in, w_f, b_f, w_out = _reorder_heads(order, w_in_b[j], w_f_b[j], b_f_b[j], w_out_b[j])
            layers.append((_in_weights(w_in, w_f), w_out.astype(jnp.bfloat16), b_f))

    outs = _proj(xf, nxt=layers[0][0])
    for i in range(DEPTH):
        _, w_out, b_f = layers[i]
        j = i // N_MIXERS
        if b_f is None:
            q, k, vt, z, nst = outs
            plain = _mixer_a_schedule(*_max_norms(nst, batch), rel_bias_a[j])
            g = _mixer_a(plain, q, k, vt, z, _mixer_a_bias(rel_bias_a[j]), batch, seq)
        else:
            q, k, vt, z, nst, lf = outs
            gk, gq, f_stats = _fox_prep(lf, b_f, batch, seq)
            g = _fox(_fox_schedule(*_max_norms(nst, batch), f_stats), q, gq, k, gk, vt, z, batch, seq)
        xf, *outs = _proj(xf, prev=(g, w_out, ln_g[i], ln_b[i]),
                          nxt=layers[i + 1][0] if i + 1 < DEPTH else None)
    return xf.reshape(batch, seq, d).astype(x.dtype)
```

```python
import functools
import math

import jax
import jax.numpy as jnp
from jax import lax
from jax.experimental import pallas as pl
from jax.experimental.pallas import tpu as pltpu

D_MODEL = 1024
N_HEADS = 16
HEAD_DIM = 64
MIX_WIDTH = N_HEADS * HEAD_DIM
CHUNK = 64
N_PAST_CHUNKS = 8
MAX_REL = 128
DEPTH = 4
N_MIXERS = 2
LN_EPS = 1e-5
DEEPNORM_ALPHA = (2.0 * DEPTH) ** 0.25
LOG2E = math.log2(math.e)
NEG = -1e30

LANES = 128
HEADS_PER_TILE = LANES // HEAD_DIM
N_HEAD_PAIRS = N_HEADS // HEADS_PER_TILE
BF16_ROWS = 16
N_SPLIT = 3
TQ = 256
Q_SUB_A = 16
Q_SUB_B = 4
TKV = 512
TM_PROJ = 512
ISSUE_LAG = 8
VMEM_LIMIT = 56 * 1024 * 1024
PRUNE_LOG2 = 152.0
NORM_SLACK = 1.0 + 2.0 ** -6
PLAIN_EXP_LOG2 = 60.0


def _params(*semantics):
    return pltpu.CompilerParams(dimension_semantics=semantics, vmem_limit_bytes=VMEM_LIMIT)


def _dot_nt(a, b):
    return lax.dot_general(a, b, (((1,), (1,)), ((), ())), preferred_element_type=jnp.float32)


def _split(x, n):
    pieces = []
    for _ in range(n):
        p = x.astype(jnp.bfloat16)
        pieces.append(p)
        x = x - p.astype(jnp.float32)
    return pieces


N_STATS = 8
LN_ROW_CHUNKS = 2


def _lagged(n, issue, consume, lag):
    pending = {}
    for c in range(n + lag):
        if c < n:
            pending[c] = issue(c)
        if c >= lag:
            consume(c - lag, pending.pop(c - lag))


def _out_proj_ln_rows(g_ref, x_ref, wo_ref, gam_ref, bet_ref, xo_ref):
    rows = TM_PROJ // LN_ROW_CHUNKS
    out = []

    def issue(c):
        return jnp.dot(g_ref[pl.ds(c * rows, rows), :], wo_ref[...], preferred_element_type=jnp.float32)

    def consume(c, y):
        sl = pl.ds(c * rows, rows)
        r = DEEPNORM_ALPHA * x_ref[sl, :] + y
        mu = jnp.mean(r, axis=-1, keepdims=True)
        d = r - mu
        var = jnp.mean(d * d, axis=-1, keepdims=True)
        xn = d * lax.rsqrt(var + LN_EPS) * gam_ref[...] + bet_ref[...]
        xo_ref[sl, :] = xn
        out.append(xn)

    _lagged(LN_ROW_CHUNKS, issue, consume, lag=1)
    return jnp.concatenate(out, axis=0)


def _in_proj_rows(x, w_ref, wvt_ref, hsel_ref, q_ref, k_ref, vt_ref, z_ref, nst_ref, gate_ref):
    xb = x.astype(jnp.bfloat16)
    sq_max = []
    for g, o_ref in enumerate((q_ref, k_ref, z_ref)):
        o = jnp.dot(xb, w_ref[:, g * MIX_WIDTH:(g + 1) * MIX_WIDTH], preferred_element_type=jnp.float32)
        if g == 2:
            o = o / (1.0 + jnp.exp(-o))
        o = o.astype(jnp.bfloat16)
        o_ref[...] = o
        if g < 2:
            of = o.astype(jnp.float32)
            sq = jnp.dot((of * of).astype(jnp.bfloat16), hsel_ref[...], preferred_element_type=jnp.float32)
            sq_max.append(jnp.max(sq, axis=0, keepdims=True))
    nst_ref[...] = jnp.concatenate(sq_max + [jnp.zeros((N_STATS - 2, LANES), jnp.float32)], axis=0)
    vt_ref[...] = _dot_nt(wvt_ref[...], xb).astype(jnp.bfloat16)
    if gate_ref is not None:
        gate_ref[...] = jnp.dot(xb, w_ref[:, 3 * MIX_WIDTH:], preferred_element_type=jnp.float32)


def _proj_kernel(*refs, has_out, has_in, has_gate):
    refs = list(refs)
    n_in = (5 if has_out else 1) + (3 if has_in else 0)
    ins, outs = refs[:n_in], refs[n_in:]
    if has_out:
        x = _out_proj_ln_rows(*ins[:5], outs.pop(0))
        ins = ins[5:]
    else:
        x = ins.pop(0)[...]
    if has_in:
        _in_proj_rows(x, *ins, *outs[:5], outs[5] if has_gate else None)


def _proj(xf, prev=None, nxt=None):
    m = xf.shape[0]
    row_d = pl.BlockSpec((TM_PROJ, D_MODEL), lambda i: (i, 0))
    row_e = pl.BlockSpec((TM_PROJ, MIX_WIDTH), lambda i: (i, 0))

    def whole(shape):
        return pl.BlockSpec(shape, lambda i: (0,) * len(shape))

    args, in_specs, out_shape, out_specs = [], [], [], []
    if prev is not None:
        g, w_out, gamma, beta = prev
        args += [g, xf, w_out, gamma.reshape(1, D_MODEL), beta.reshape(1, D_MODEL)]
        in_specs += [row_e, row_d, whole((MIX_WIDTH, D_MODEL)), whole((1, D_MODEL)), whole((1, D_MODEL))]
        out_shape.append(jax.ShapeDtypeStruct((m, D_MODEL), jnp.float32))
        out_specs.append(row_d)
    else:
        args.append(xf)
        in_specs.append(row_d)
    has_gate = False
    if nxt is not None:
        w, wvt = nxt
        has_gate = w.shape[1] > 3 * MIX_WIDTH
        hsel = (jnp.arange(MIX_WIDTH)[:, None] // HEAD_DIM == jnp.arange(LANES)[None, :]).astype(jnp.bfloat16)
        args += [w, wvt, hsel]
        in_specs += [whole(w.shape), whole(wvt.shape), whole(hsel.shape)]
        tok = jax.ShapeDtypeStruct((m, MIX_WIDTH), jnp.bfloat16)
        out_shape += [tok, tok, jax.ShapeDtypeStruct((MIX_WIDTH, m), jnp.bfloat16), tok,
                      jax.ShapeDtypeStruct((m // TM_PROJ, N_STATS, LANES), jnp.float32)]
        out_specs += [row_e, row_e, pl.BlockSpec((MIX_WIDTH, TM_PROJ), lambda i: (0, i)), row_e,
                      pl.BlockSpec((None, N_STATS, LANES), lambda i: (i, 0, 0))]
        if has_gate:
            out_shape.append(jax.ShapeDtypeStruct((m, LANES), jnp.float32))
            out_specs.append(pl.BlockSpec((TM_PROJ, LANES), lambda i: (i, 0)))
    name = (("out_" if prev is not None else "") + ("in_" if nxt is not None else "") + "proj"
            + ("_gate" if has_gate else ""))
    return pl.pallas_call(
        functools.partial(_proj_kernel, has_out=prev is not None, has_in=nxt is not None, has_gate=has_gate),
        grid=(m // TM_PROJ,),
        in_specs=in_specs,
        out_specs=out_specs,
        out_shape=out_shape,
        compiler_params=_params("arbitrary"),
        name=name,
    )(*args)


def _max_norms(nst, batch):
    nb = nst.shape[0] // batch
    n = jnp.sqrt(nst[:, :2, :N_HEADS]).reshape(batch, nb, 2, N_HEADS) * NORM_SLACK
    return n[:, :, 0], n[:, :, 1]


ACC_ROWS = HEAD_DIM + BF16_ROWS
U_ROWS = 8


def _head_masked_queries(q2):
    lane = lax.broadcasted_iota(jnp.int32, q2.shape, 1)
    zero = jnp.zeros_like(q2)
    return [jnp.where((lane >= HEAD_DIM * h) & (lane < HEAD_DIM * (h + 1)), q2, zero)
            for h in range(HEADS_PER_TILE)]


def _head_values(vt_blk, h):
    ones = jnp.ones((BF16_ROWS, vt_blk.shape[1]), vt_blk.dtype)
    return jnp.concatenate([vt_blk[HEAD_DIM * h:HEAD_DIM * (h + 1), :], ones], axis=0)


def _online_update(u_ref, acc_ref, c, a, vt_h):
    u = u_ref[c, 0:1, :]
    u_new = jnp.maximum(u, jnp.max(a, axis=0, keepdims=True))
    p = jnp.exp2(a - u_new).astype(jnp.bfloat16)
    acc_ref[c] = jnp.exp2(u - u_new) * acc_ref[c] + jnp.dot(vt_h, p, preferred_element_type=jnp.float32)
    u_ref[c, 0:1, :] = u_new


def _plain_update(u_ref, acc_ref, c, a, vt_h):
    del u_ref
    acc_ref[c] = acc_ref[c] + jnp.dot(vt_h, jnp.exp2(a).astype(jnp.bfloat16),
                                      preferred_element_type=jnp.float32)


def _reset(u_ref, acc_ref):
    u_ref[...] = jnp.full(u_ref.shape, NEG, jnp.float32)
    acc_ref[...] = jnp.zeros(acc_ref.shape, jnp.float32)


def _finish(acc_ref, z_ref, o_ref, r, row0):
    accs = [acc_ref[r * HEADS_PER_TILE + h] for h in range(HEADS_PER_TILE)]
    ot = jnp.concatenate([acc[:HEAD_DIM] / acc[HEAD_DIM:HEAD_DIM + 1] for acc in accs], axis=0)
    rows = pl.ds(row0, TQ)
    o_ref[rows, :] = (ot.T * z_ref[rows, :].astype(jnp.float32)).astype(o_ref.dtype)


def _attn_specs(seq, pair_major=False):
    def spec(block, index):
        if pair_major:
            return pl.BlockSpec(block, lambda hp, b, *_: index(b, hp))
        return pl.BlockSpec(block, lambda b, hp, *_: index(b, hp))

    seq_rows = spec((seq, LANES), lambda b, hp: (b, hp))
    seq_cols = spec((LANES, seq), lambda b, hp: (hp, b))
    return seq_rows, seq_cols, spec


def _state_scratch(q_sub):
    n = q_sub * HEADS_PER_TILE
    return [pltpu.VMEM((n, U_ROWS, TQ), jnp.float32), pltpu.VMEM((n, ACC_ROWS, TQ), jnp.float32)]


A_PAST = N_PAST_CHUNKS * CHUNK
A_WINDOW = A_PAST + TQ
A_BIAS_ROWS = 2 * A_PAST + TQ
A_BAND_LO = A_PAST - MAX_REL
A_BAND_HI = A_PAST + TQ + CHUNK


def _mixer_a_kernel(plain_ref, q_ref, k_ref, vt_ref, z_ref, bias_ref, o_ref, u_ref, acc_ref):
    plain = plain_ref[pl.program_id(1) * N_HEAD_PAIRS + pl.program_id(0)]
    n_groups = q_ref.shape[0] // (Q_SUB_A * TQ)

    def group(gi, update):
        _reset(u_ref, acc_ref)

        def qrow(r):
            return pl.multiple_of((gi * Q_SUB_A + r) * TQ, TQ)

        def window(r):
            start = qrow(r) - A_PAST
            row0 = pl.multiple_of(jnp.maximum(start, 0), TQ)
            return row0, pl.multiple_of(row0 - start, TQ)

        def issue(c):
            r, h = divmod(c, HEADS_PER_TILE)
            row0, brow = window(r)
            qm = _head_masked_queries(q_ref[pl.ds(qrow(r), TQ), :])[h]
            return _dot_nt(k_ref[pl.ds(row0, A_WINDOW), :], qm) + bias_ref[h, pl.ds(brow, A_WINDOW), :]

        def consume(c, a):
            r, h = divmod(c, HEADS_PER_TILE)
            row0, _ = window(r)
            update(u_ref, acc_ref, c, a, _head_values(vt_ref[:, pl.ds(row0, A_WINDOW)], h))
            if h == HEADS_PER_TILE - 1:
                _finish(acc_ref, z_ref, o_ref, r, qrow(r))

        _lagged(Q_SUB_A * HEADS_PER_TILE, issue, consume, ISSUE_LAG)

    def run(update):
        def body(gi, carry):
            group(gi, update)
            return carry
        lax.fori_loop(0, n_groups, body, 0)

    @pl.when(plain != 0)
    def _():
        run(_plain_update)

    @pl.when(plain == 0)
    def _():
        run(_online_update)


A_BAND = A_BAND_HI - A_BAND_LO
A_STRIP = 1024
A_STRIP_OFF = 512


def _bias_band_kernel(s_ref, o_ref):
    x = jnp.broadcast_to(s_ref[0], (A_BAND, A_STRIP))
    o_ref[0] = pltpu.roll(x, 0, 1, stride=1, stride_axis=0)[:, A_STRIP_OFF:A_STRIP_OFF + TQ]


def _mixer_a_bias(rel_bias):
    assert A_STRIP_OFF >= A_BAND - 1 and A_STRIP_OFF + TQ <= A_STRIP
    rb = rel_bias.astype(jnp.float32) * LOG2E
    c = jnp.arange(A_BIAS_ROWS)[:, None]
    r = jnp.arange(TQ)[None, :]
    strip_dist = jnp.arange(A_STRIP) - A_STRIP_OFF + (A_PAST - A_BAND_LO)
    strip = rb[:, jnp.clip(strip_dist, -MAX_REL, MAX_REL) + MAX_REL].reshape(N_HEADS, 1, A_STRIP)
    band = pl.pallas_call(
        _bias_band_kernel,
        grid=(N_HEADS,),
        in_specs=[pl.BlockSpec((1, 1, A_STRIP), lambda h: (h, 0, 0))],
        out_specs=pl.BlockSpec((1, A_BAND, TQ), lambda h: (h, 0, 0)),
        out_shape=jax.ShapeDtypeStruct((N_HEADS, A_BAND, TQ), jnp.float32),
        compiler_params=_params("arbitrary"),
        name="bias_band",
    )(strip)
    far = jnp.broadcast_to(rb[:, 2 * MAX_REL][:, None, None], (N_HEADS, A_BAND_LO, TQ))
    late = jnp.zeros((N_HEADS, A_BIAS_ROWS - A_BAND_HI, TQ), jnp.float32)
    table = jnp.concatenate([far, band, late], axis=1)
    dchunk = r // CHUNK - (c - A_PAST) // CHUNK
    visible = (dchunk >= 0) & (dchunk <= N_PAST_CHUNKS)
    return jnp.where(visible[None], table, NEG)


def _mixer_a_schedule(qn, kn, rel_bias):
    worst = jnp.max(qn, axis=1) * jnp.max(kn, axis=1) + jnp.max(jnp.abs(rel_bias), axis=-1)[None, :] * LOG2E
    ok = worst < PLAIN_EXP_LOG2
    return jnp.all(ok.reshape(-1, N_HEAD_PAIRS, HEADS_PER_TILE), axis=-1).reshape(-1).astype(jnp.int32)


def _mixer_a(plain, q, k, vt, z, bias, batch, seq):
    seq_rows, seq_cols, spec = _attn_specs(seq, pair_major=True)
    return pl.pallas_call(
        _mixer_a_kernel,
        grid_spec=pltpu.PrefetchScalarGridSpec(
            num_scalar_prefetch=1,
            grid=(N_HEAD_PAIRS, batch),
            in_specs=[seq_rows, seq_rows, seq_cols, seq_rows,
                      spec((HEADS_PER_TILE, A_BIAS_ROWS, TQ), lambda b, hp: (hp, 0, 0))],
            out_specs=seq_rows,
            scratch_shapes=_state_scratch(Q_SUB_A)),
        out_shape=jax.ShapeDtypeStruct(q.shape, jnp.bfloat16),
        compiler_params=_params("arbitrary", "arbitrary"),
        name="mixer_a",
    )(plain, q, k, vt, z, bias)


TG_B = Q_SUB_B * TQ
KV_PER_G = TG_B // TKV
N_AUG = HEADS_PER_TILE * N_SPLIT
AUG_STRIDE = LANES // N_HEAD_PAIRS


def _fox_prep_kernel(lf_ref, bf_ref, tri_ref, sel_ref, ones_ref, gk_ref, gq_ref, st_ref, carry_ref):
    @pl.when(pl.program_id(1) == 0)
    def _():
        carry_ref[...] = jnp.zeros_like(carry_ref)

    pre = lf_ref[...] + bf_ref[...]
    logf = (jnp.minimum(pre, 0.0) - jnp.log(1.0 + jnp.exp(-jnp.abs(pre)))) * LOG2E
    tri = tri_ref[...]
    csum = sum(jnp.dot(tri, piece, preferred_element_type=jnp.float32) for piece in _split(logf, N_SPLIT))
    f = csum + carry_ref[...]
    carry_ref[...] = f[TKV - 1:TKV, :]
    pieces = _split(-f, N_SPLIT)
    for side, o_ref in enumerate((gk_ref, gq_ref)):
        o_ref[...] = (sum(jnp.dot(piece, sel_ref[side, n], preferred_element_type=jnp.float32)
                          for n, piece in enumerate(pieces)) + ones_ref[side:side + 1, :]).astype(jnp.bfloat16)
    st_ref[...] = jnp.concatenate(
        [f[0:1, :], f[TKV - 1:TKV, :], jnp.zeros((N_STATS - 2, LANES), jnp.float32)], axis=0)


def _fox_prep(lf, b_f, batch, seq):
    nb = seq // TKV
    bf = jnp.pad(b_f.astype(jnp.float32), (0, LANES - N_HEADS)).reshape(1, LANES)
    tri = (jnp.arange(TKV)[:, None] >= jnp.arange(TKV)[None, :]).astype(jnp.bfloat16)
    src = jnp.arange(LANES)[:, None]
    dst = jnp.arange(LANES)[None, :]
    base = (src // HEADS_PER_TILE) * AUG_STRIDE + (src % HEADS_PER_TILE) * N_SPLIT
    sel = jnp.stack([
        jnp.stack([jnp.where((dst == base + side * N_AUG + n) & (src < N_HEADS), 1.0 - 2.0 * side, 0.0)
                   for n in range(N_SPLIT)]) for side in range(2)]).astype(jnp.bfloat16)
    lane = jnp.arange(LANES) % AUG_STRIDE
    ones = jnp.stack([(lane >= N_AUG) & (lane < 2 * N_AUG), lane < N_AUG]).astype(jnp.float32)
    tokspec = pl.BlockSpec((TKV, LANES), lambda b, i: (b * nb + i, 0))
    aug = jax.ShapeDtypeStruct((batch * seq, LANES), jnp.bfloat16)
    return pl.pallas_call(
        _fox_prep_kernel,
        grid=(batch, nb),
        in_specs=[tokspec,
                  pl.BlockSpec((1, LANES), lambda b, i: (0, 0)),
                  pl.BlockSpec((TKV, TKV), lambda b, i: (0, 0)),
                  pl.BlockSpec((2, N_SPLIT, LANES, LANES), lambda b, i: (0, 0, 0, 0)),
                  pl.BlockSpec((2, LANES), lambda b, i: (0, 0))],
        out_specs=[tokspec, tokspec, pl.BlockSpec((None, None, N_STATS, LANES), lambda b, i: (b, i, 0, 0))],
        out_shape=[aug, aug, jax.ShapeDtypeStruct((batch, nb, N_STATS, LANES), jnp.float32)],
        scratch_shapes=[pltpu.VMEM((1, LANES), jnp.float32)],
        compiler_params=_params("arbitrary", "arbitrary"),
        name="fox_prep",
    )(lf, bf, tri, sel, ones)


def _fox_schedule(qn, kn, f_stats):
    b, nb = f_stats.shape[0], f_stats.shape[1]
    ng = nb // KV_PER_G
    f_first, f_last = f_stats[:, :, 0, :N_HEADS], f_stats[:, :, 1, :N_HEADS]
    qn_g = jnp.max(qn.reshape(b, ng, KV_PER_G, N_HEADS), axis=2)
    kn_g = jnp.max(kn.reshape(b, ng, KV_PER_G, N_HEADS), axis=2)
    f_first_g = f_first.reshape(b, ng, KV_PER_G, N_HEADS)[:, :, 0]
    bound = (qn_g[:, :, None] * kn[:, None, :] + f_first_g[:, :, None] - f_last[:, None, :]
             + (qn_g * kn_g)[:, :, None])
    i_idx = jnp.arange(ng)[None, :, None, None]
    j_idx = jnp.arange(nb)[None, None, :, None]
    skip = (bound < -PRUNE_LOG2) & (j_idx < i_idx * KV_PER_G)
    first = jnp.min(jnp.where(skip, nb, j_idx), axis=2)
    first = jnp.min(first.reshape(b, ng, N_HEAD_PAIRS, HEADS_PER_TILE), axis=-1)
    first = jnp.transpose(first, (0, 2, 1)).reshape(-1)
    worst = jnp.max(qn, axis=1) * jnp.max(kn, axis=1)
    plain = jnp.all((worst < PLAIN_EXP_LOG2).reshape(b, N_HEAD_PAIRS, HEADS_PER_TILE), axis=-1).reshape(-1)
    return jnp.concatenate([first, plain.astype(first.dtype)]).astype(jnp.int32)


def _fox_kernel(sched_ref, q_ref, gq_ref, k_ref, gk_ref, vt_ref, z_ref, o_ref, u_ref, acc_ref):
    hp = pl.program_id(1)
    n_groups = q_ref.shape[0] // TG_B
    pair = pl.program_id(0) * N_HEAD_PAIRS + hp
    plain = sched_ref[pl.num_programs(0) * N_HEAD_PAIRS * n_groups + pair]
    n_chains = Q_SUB_B * HEADS_PER_TILE

    def keys_at(row0, n):
        rows = pl.ds(row0, n)
        return jnp.concatenate([k_ref[rows, :], gk_ref[rows, :]], axis=1), vt_ref[:, rows]

    def group(gi, update):
        j_first = sched_ref[pair * n_groups + gi]

        def qrow(r):
            return pl.multiple_of(gi * TG_B + r * TQ, TQ)

        def qaug(c):
            r, h = divmod(c, HEADS_PER_TILE)
            rows = pl.ds(qrow(r), TQ)
            gq = gq_ref[rows, :]
            off = lax.broadcasted_iota(jnp.int32, (TQ, LANES), 1) - hp * AUG_STRIDE
            own = ((off >= N_SPLIT * h) & (off < N_SPLIT * (h + 1))) | \
                  ((off >= N_AUG + N_SPLIT * h) & (off < N_AUG + N_SPLIT * (h + 1)))
            return jnp.concatenate([_head_masked_queries(q_ref[rows, :])[h],
                                    jnp.where(own, gq, jnp.zeros_like(gq))], axis=1)

        _reset(u_ref, acc_ref)
        qa = [qaug(c) for c in range(n_chains)]

        def keys_block(j, n_steps):
            kaug, vt_blk = keys_at(pl.multiple_of(j * TKV, TKV), n_steps * TKV)
            _lagged(n_chains,
                    lambda c: _dot_nt(kaug, qa[c]),
                    lambda c, a: update(u_ref, acc_ref, c, a, _head_values(vt_blk, c % HEADS_PER_TILE)),
                    ISSUE_LAG)

        n_before = gi * KV_PER_G - j_first
        n_joined = jnp.where((n_before & 1) != 0, 1, jnp.minimum(n_before, 2))

        def two_steps(p, carry):
            keys_block(j_first + n_joined + 2 * p, 2)
            return carry

        lax.fori_loop(0, lax.shift_right_logical(n_before - n_joined, 1), two_steps, 0)

        def own_block(joined):
            kaug, vt_blk = keys_at(pl.multiple_of(gi * TG_B, TG_B), TG_B)
            s_loc = lax.broadcasted_iota(jnp.int32, (TQ, TQ), 0)
            t_loc = lax.broadcasted_iota(jnp.int32, (TQ, TQ), 1)
            n_extra = n_chains if joined else 0
            if joined:
                kaug_e, vt_e = keys_at(pl.multiple_of(j_first * TKV, TKV), joined * TKV)

            def issue(i):
                if i < n_extra:
                    return _dot_nt(kaug_e, qa[i])
                c = i - n_extra
                r = c // HEADS_PER_TILE
                a = _dot_nt(kaug[:(r + 1) * TQ], qa[c])
                diag = jnp.where(s_loc <= t_loc, a[r * TQ:], NEG)
                return diag if r == 0 else jnp.concatenate([a[:r * TQ], diag], axis=0)

            def consume(i, a):
                if i < n_extra:
                    update(u_ref, acc_ref, i, a, _head_values(vt_e, i % HEADS_PER_TILE))
                    return
                c = i - n_extra
                r, h = divmod(c, HEADS_PER_TILE)
                update(u_ref, acc_ref, c, a, _head_values(vt_blk[:, :(r + 1) * TQ], h))
                if h == HEADS_PER_TILE - 1:
                    _finish(acc_ref, z_ref, o_ref, r, qrow(r))

            _lagged(n_extra + n_chains, issue, consume, ISSUE_LAG)

        for joined in range(3):
            pl.when(n_joined == joined)(functools.partial(own_block, joined))

    def run(update):
        def body(gi, carry):
            group(gi, update)
            return carry
        lax.fori_loop(0, n_groups, body, 0)

    @pl.when(plain != 0)
    def _():
        run(_plain_update)

    @pl.when(plain == 0)
    def _():
        run(_online_update)


def _fox(sched, q, gq, k, gk, vt, z, batch, seq):
    seq_rows, seq_cols, spec = _attn_specs(seq)
    aug_seq = spec((seq, LANES), lambda b, hp: (b, 0))
    return pl.pallas_call(
        _fox_kernel,
        grid_spec=pltpu.PrefetchScalarGridSpec(
            num_scalar_prefetch=1,
            grid=(batch, N_HEAD_PAIRS),
            in_specs=[seq_rows, aug_seq, seq_rows, aug_seq, seq_cols, seq_rows],
            out_specs=seq_rows,
            scratch_shapes=_state_scratch(Q_SUB_B)),
        out_shape=jax.ShapeDtypeStruct(q.shape, jnp.bfloat16),
        compiler_params=_params("arbitrary", "arbitrary"),
        name="fox",
    )(sched, q, gq, k, gk, vt, z)


def _in_weights(w_in, w_gate=None):
    w_in = w_in.astype(jnp.float32)
    e = MIX_WIDTH
    cols = [w_in[:, :e] * (LOG2E / math.sqrt(HEAD_DIM)), w_in[:, e:2 * e], w_in[:, 3 * e:]]
    if w_gate is not None:
        cols.append(jnp.pad(w_gate.astype(jnp.float32), ((0, 0), (0, LANES - N_HEADS))))
    return jnp.concatenate(cols, axis=1).astype(jnp.bfloat16), w_in[:, 2 * e:3 * e].T.astype(jnp.bfloat16)


def _reorder_heads(order, w_in, w_f, b_f, w_out):
    d = w_in.shape[0]
    w_in = w_in.reshape(d, 4, N_HEADS, HEAD_DIM)[:, :, order].reshape(d, 4 * MIX_WIDTH)
    w_out = w_out.reshape(N_HEADS, HEAD_DIM, -1)[order].reshape(MIX_WIDTH, -1)
    return w_in, w_f[:, order], b_f[order], w_out


def kernel(x, w_in_a, rel_bias_a, w_out_a, w_in_b, w_f_b, b_f_b, w_out_b, ln_g, ln_b):
    batch, seq, d = x.shape
    assert seq % (Q_SUB_A * TQ) == 0 and seq % TG_B == 0 and TG_B % TKV == 0 and TKV == TM_PROJ
    assert seq >= A_WINDOW and N_HEAD_PAIRS * AUG_STRIDE == LANES and 2 * N_AUG <= AUG_STRIDE
    xf = x.reshape(batch * seq, d).astype(jnp.float32)

    layers = []
    for i in range(DEPTH):
        j = i // N_MIXERS
        if i % N_MIXERS == 0:
            layers.append((_in_weights(w_in_a[j]), w_out_a[j].astype(jnp.bfloat16), None))
        else:
            order = jnp.argsort(b_f_b[j])
            w_in, w_f, b_f, w_out = _reorder_heads(order, w_in_b[j], w_f_b[j], b_f_b[j], w_out_b[j])
            layers.append((_in_weights(w_in, w_f), w_out.astype(jnp.bfloat16), b_f))

    outs = _proj(xf, nxt=layers[0][0])
    for i in range(DEPTH):
        _, w_out, b_f = layers[i]
        j = i // N_MIXERS
        if b_f is None:
            q, k, vt, z, nst = outs
            plain = _mixer_a_schedule(*_max_norms(nst, batch), rel_bias_a[j])
            g = _mixer_a(plain, q, k, vt, z, _mixer_a_bias(rel_bias_a[j]), batch, seq)
        else:
            q, k, vt, z, nst, lf = outs
            gk, gq, f_stats = _fox_prep(lf, b_f, batch, seq)
            g = _fox(_fox_schedule(*_max_norms(nst, batch), f_stats), q, gq, k, gk, vt, z, batch, seq)
        xf, *outs = _proj(xf, prev=(g, w_out, ln_g[i], ln_b[i]),
                          nxt=layers[i + 1][0] if i + 1 < DEPTH else None)
    return xf.reshape(batch, seq, d).astype(x.dtype)
```

```python
import functools
import math

import jax
import jax.numpy as jnp
from jax import lax
from jax.experimental import pallas as pl
from jax.experimental.pallas import tpu as pltpu

D_MODEL = 1024
N_HEADS = 16
HEAD_DIM = 64
MIX_WIDTH = N_HEADS * HEAD_DIM
CHUNK = 64
N_PAST_CHUNKS = 8
MAX_REL = 128
DEPTH = 4
N_MIXERS = 2
LN_EPS = 1e-5
DEEPNORM_ALPHA = (2.0 * DEPTH) ** 0.25
LOG2E = math.log2(math.e)
NEG = -1e30

LANES = 128
HEADS_PER_TILE = LANES // HEAD_DIM
N_HEAD_PAIRS = N_HEADS // HEADS_PER_TILE
BF16_ROWS = 16
N_SPLIT = 3
TQ = 256
Q_SUB_A = 16
Q_SUB_B = 4
TKV = 512
TM_PROJ = 512
ISSUE_LAG = 8
VMEM_LIMIT = 56 * 1024 * 1024
PRUNE_LOG2 = 152.0
NORM_SLACK = 1.0 + 2.0 ** -6
PLAIN_EXP_LOG2 = 60.0


def _params(*semantics):
    return pltpu.CompilerParams(dimension_semantics=semantics, vmem_limit_bytes=VMEM_LIMIT)


def _dot_nt(a, b):
    return lax.dot_general(a, b, (((1,), (1,)), ((), ())), preferred_element_type=jnp.float32)


def _split(x, n):
    pieces = []
    for _ in range(n):
        p = x.astype(jnp.bfloat16)
        pieces.append(p)
        x = x - p.astype(jnp.float32)
    return pieces


N_STATS = 8
LN_ROW_CHUNKS = 2


def _lagged(n, issue, consume, lag):
    pending = {}
    for c in range(n + lag):
        if c < n:
            pending[c] = issue(c)
        if c >= lag:
            consume(c - lag, pending.pop(c - lag))


def _out_proj_ln_rows(g_ref, x_ref, wo_ref, gam_ref, bet_ref, xo_ref):
    rows = TM_PROJ // LN_ROW_CHUNKS
    out = []

    def issue(c):
        return jnp.dot(g_ref[pl.ds(c * rows, rows), :], wo_ref[...], preferred_element_type=jnp.float32)

    def consume(c, y):
        sl = pl.ds(c * rows, rows)
        r = DEEPNORM_ALPHA * x_ref[sl, :] + y
        mu = jnp.mean(r, axis=-1, keepdims=True)
        d = r - mu
        var = jnp.mean(d * d, axis=-1, keepdims=True)
        xn = d * lax.rsqrt(var + LN_EPS) * gam_ref[...] + bet_ref[...]
        xo_ref[sl, :] = xn
        out.append(xn)

    _lagged(LN_ROW_CHUNKS, issue, consume, lag=1)
    return jnp.concatenate(out, axis=0)


def _in_proj_rows(x, w_ref, wvt_ref, hsel_ref, q_ref, k_ref, vt_ref, z_ref, nst_ref, gate_ref):
    xb = x.astype(jnp.bfloat16)
    sq_max = []
    for g, o_ref in enumerate((q_ref, k_ref, z_ref)):
        o = jnp.dot(xb, w_ref[:, g * MIX_WIDTH:(g + 1) * MIX_WIDTH], preferred_element_type=jnp.float32)
        if g == 2:
            o = o / (1.0 + jnp.exp(-o))
        o = o.astype(jnp.bfloat16)
        o_ref[...] = o
        if g < 2:
            of = o.astype(jnp.float32)
            sq = jnp.dot((of * of).astype(jnp.bfloat16), hsel_ref[...], preferred_element_type=jnp.float32)
            sq_max.append(jnp.max(sq, axis=0, keepdims=True))
    nst_ref[...] = jnp.concatenate(sq_max + [jnp.zeros((N_STATS - 2, LANES), jnp.float32)], axis=0)
    vt_ref[...] = _dot_nt(wvt_ref[...], xb).astype(jnp.bfloat16)
    if gate_ref is not None:
        gate_ref[...] = jnp.dot(xb, w_ref[:, 3 * MIX_WIDTH:], preferred_element_type=jnp.float32)


def _proj_kernel(*refs, has_out, has_in, has_gate):
    refs = list(refs)
    n_in = (5 if has_out else 1) + (3 if has_in else 0)
    ins, outs = refs[:n_in], refs[n_in:]
    if has_out:
        x = _out_proj_ln_rows(*ins[:5], outs.pop(0))
        ins = ins[5:]
    else:
        x = ins.pop(0)[...]
    if has_in:
        _in_proj_rows(x, *ins, *outs[:5], outs[5] if has_gate else None)


def _proj(xf, prev=None, nxt=None):
    m = xf.shape[0]
    row_d = pl.BlockSpec((TM_PROJ, D_MODEL), lambda i: (i, 0))
    row_e = pl.BlockSpec((TM_PROJ, MIX_WIDTH), lambda i: (i, 0))

    def whole(shape):
        return pl.BlockSpec(shape, lambda i: (0,) * len(shape))

    args, in_specs, out_shape, out_specs = [], [], [], []
    if prev is not None:
        g, w_out, gamma, beta = prev
        args += [g, xf, w_out, gamma.reshape(1, D_MODEL), beta.reshape(1, D_MODEL)]
        in_specs += [row_e, row_d, whole((MIX_WIDTH, D_MODEL)), whole((1, D_MODEL)), whole((1, D_MODEL))]
        out_shape.append(jax.ShapeDtypeStruct((m, D_MODEL), jnp.float32))
        out_specs.append(row_d)
    else:
        args.append(xf)
        in_specs.append(row_d)
    has_gate = False
    if nxt is not None:
        w, wvt = nxt
        has_gate = w.shape[1] > 3 * MIX_WIDTH
        hsel = (jnp.arange(MIX_WIDTH)[:, None] // HEAD_DIM == jnp.arange(LANES)[None, :]).astype(jnp.bfloat16)
        args += [w, wvt, hsel]
        in_specs += [whole(w.shape), whole(wvt.shape), whole(hsel.shape)]
        tok = jax.ShapeDtypeStruct((m, MIX_WIDTH), jnp.bfloat16)
        out_shape += [tok, tok, jax.ShapeDtypeStruct((MIX_WIDTH, m), jnp.bfloat16), tok,
                      jax.ShapeDtypeStruct((m // TM_PROJ, N_STATS, LANES), jnp.float32)]
        out_specs += [row_e, row_e, pl.BlockSpec((MIX_WIDTH, TM_PROJ), lambda i: (0, i)), row_e,
                      pl.BlockSpec((None, N_STATS, LANES), lambda i: (i, 0, 0))]
        if has_gate:
            out_shape.append(jax.ShapeDtypeStruct((m, LANES), jnp.float32))
            out_specs.append(pl.BlockSpec((TM_PROJ, LANES), lambda i: (i, 0)))
    name = (("out_" if prev is not None else "") + ("in_" if nxt is not None else "") + "proj"
            + ("_gate" if has_gate else ""))
    return pl.pallas_call(
        functools.partial(_proj_kernel, has_out=prev is not None, has_in=nxt is not None, has_gate=has_gate),
        grid=(m // TM_PROJ,),
        in_specs=in_specs,
        out_specs=out_specs,
        out_shape=out_shape,
        compiler_params=_params("arbitrary"),
        name=name,
    )(*args)


def _max_norms(nst, batch):
    nb = nst.shape[0] // batch
    n = jnp.sqrt(nst[:, :2, :N_HEADS]).reshape(batch, nb, 2, N_HEADS) * NORM_SLACK
    return n[:, :, 0], n[:, :, 1]


ACC_ROWS = HEAD_DIM + BF16_ROWS
U_ROWS = 8


def _head_masked_queries(q2):
    lane = lax.broadcasted_iota(jnp.int32, q2.shape, 1)
    zero = jnp.zeros_like(q2)
    return [jnp.where((lane >= HEAD_DIM * h) & (lane < HEAD_DIM * (h + 1)), q2, zero)
            for h in range(HEADS_PER_TILE)]


def _head_values(vt_blk, h):
    ones = jnp.ones((BF16_ROWS, vt_blk.shape[1]), vt_blk.dtype)
    return jnp.concatenate([vt_blk[HEAD_DIM * h:HEAD_DIM * (h + 1), :], ones], axis=0)


def _online_update(u_ref, acc_ref, c, a, vt_h):
    u = u_ref[c, 0:1, :]
    u_new = jnp.maximum(u, jnp.max(a, axis=0, keepdims=True))
    p = jnp.exp2(a - u_new).astype(jnp.bfloat16)
    acc_ref[c] = jnp.exp2(u - u_new) * acc_ref[c] + jnp.dot(vt_h, p, preferred_element_type=jnp.float32)
    u_ref[c, 0:1, :] = u_new


def _plain_update(u_ref, acc_ref, c, a, vt_h):
    del u_ref
    acc_ref[c] = acc_ref[c] + jnp.dot(vt_h, jnp.exp2(a).astype(jnp.bfloat16),
                                      preferred_element_type=jnp.float32)


def _reset(u_ref, acc_ref):
    u_ref[...] = jnp.full(u_ref.shape, NEG, jnp.float32)
    acc_ref[...] = jnp.zeros(acc_ref.shape, jnp.float32)


def _finish(u_ref, acc_ref, z_ref, o_ref, r, row0, reinit):
    chains = [r * HEADS_PER_TILE + h for h in range(HEADS_PER_TILE)]
    accs = [acc_ref[c] for c in chains]
    ot = jnp.concatenate([acc[:HEAD_DIM] / acc[HEAD_DIM:HEAD_DIM + 1] for acc in accs], axis=0)
    rows = pl.ds(row0, TQ)
    o_ref[rows, :] = (ot.T * z_ref[rows, :].astype(jnp.float32)).astype(o_ref.dtype)
    for c in chains if reinit else ():
        acc_ref[c] = jnp.zeros(acc_ref.shape[1:], jnp.float32)
        u_ref[c] = jnp.full(u_ref.shape[1:], NEG, jnp.float32)


def _attn_specs(seq, pair_major=False):
    def spec(block, index):
        if pair_major:
            return pl.BlockSpec(block, lambda hp, b, *_: index(b, hp))
        return pl.BlockSpec(block, lambda b, hp, *_: index(b, hp))

    seq_rows = spec((seq, LANES), lambda b, hp: (b, hp))
    seq_cols = spec((LANES, seq), lambda b, hp: (hp, b))
    return seq_rows, seq_cols, spec


def _state_scratch(q_sub):
    n = q_sub * HEADS_PER_TILE
    return [pltpu.VMEM((n, U_ROWS, TQ), jnp.float32), pltpu.VMEM((n, ACC_ROWS, TQ), jnp.float32)]


A_PAST = N_PAST_CHUNKS * CHUNK
A_WINDOW = A_PAST + TQ
A_BIAS_ROWS = 2 * A_PAST + TQ
A_BAND_LO = A_PAST - MAX_REL
A_BAND_HI = A_PAST + TQ + CHUNK


def _mixer_a_kernel(plain_ref, q_ref, k_ref, vt_ref, z_ref, bias_ref, o_ref, u_ref, acc_ref):
    plain = plain_ref[pl.program_id(1) * N_HEAD_PAIRS + pl.program_id(0)]
    n_groups = q_ref.shape[0] // (Q_SUB_A * TQ)

    def group(gi, update):
        _reset(u_ref, acc_ref)

        def qrow(r):
            return pl.multiple_of((gi * Q_SUB_A + r) * TQ, TQ)

        def window(r):
            start = qrow(r) - A_PAST
            row0 = pl.multiple_of(jnp.maximum(start, 0), TQ)
            return row0, pl.multiple_of(row0 - start, TQ)

        def issue(c):
            r, h = divmod(c, HEADS_PER_TILE)
            row0, brow = window(r)
            qm = _head_masked_queries(q_ref[pl.ds(qrow(r), TQ), :])[h]
            return _dot_nt(k_ref[pl.ds(row0, A_WINDOW), :], qm) + bias_ref[h, pl.ds(brow, A_WINDOW), :]

        def consume(c, a):
            r, h = divmod(c, HEADS_PER_TILE)
            row0, _ = window(r)
            update(u_ref, acc_ref, c, a, _head_values(vt_ref[:, pl.ds(row0, A_WINDOW)], h))
            if h == HEADS_PER_TILE - 1:
                _finish(u_ref, acc_ref, z_ref, o_ref, r, qrow(r), reinit=False)

        _lagged(Q_SUB_A * HEADS_PER_TILE, issue, consume, ISSUE_LAG)

    def run(update):
        def body(gi, carry):
            group(gi, update)
            return carry
        lax.fori_loop(0, n_groups, body, 0)

    @pl.when(plain != 0)
    def _():
        run(_plain_update)

    @pl.when(plain == 0)
    def _():
        run(_online_update)


A_BAND = A_BAND_HI - A_BAND_LO
A_STRIP = 1024
A_STRIP_OFF = 512


def _bias_band_kernel(s_ref, o_ref):
    x = jnp.broadcast_to(s_ref[0], (A_BAND, A_STRIP))
    o_ref[0] = pltpu.roll(x, 0, 1, stride=1, stride_axis=0)[:, A_STRIP_OFF:A_STRIP_OFF + TQ]


def _mixer_a_bias(rel_bias):
    assert A_STRIP_OFF >= A_BAND - 1 and A_STRIP_OFF + TQ <= A_STRIP
    rb = rel_bias.astype(jnp.float32) * LOG2E
    c = jnp.arange(A_BIAS_ROWS)[:, None]
    r = jnp.arange(TQ)[None, :]
    strip_dist = jnp.arange(A_STRIP) - A_STRIP_OFF + (A_PAST - A_BAND_LO)
    strip = rb[:, jnp.clip(strip_dist, -MAX_REL, MAX_REL) + MAX_REL].reshape(N_HEADS, 1, A_STRIP)
    band = pl.pallas_call(
        _bias_band_kernel,
        grid=(N_HEADS,),
        in_specs=[pl.BlockSpec((1, 1, A_STRIP), lambda h: (h, 0, 0))],
        out_specs=pl.BlockSpec((1, A_BAND, TQ), lambda h: (h, 0, 0)),
        out_shape=jax.ShapeDtypeStruct((N_HEADS, A_BAND, TQ), jnp.float32),
        compiler_params=_params("arbitrary"),
        name="bias_band",
    )(strip)
    far = jnp.broadcast_to(rb[:, 2 * MAX_REL][:, None, None], (N_HEADS, A_BAND_LO, TQ))
    late = jnp.zeros((N_HEADS, A_BIAS_ROWS - A_BAND_HI, TQ), jnp.float32)
    table = jnp.concatenate([far, band, late], axis=1)
    dchunk = r // CHUNK - (c - A_PAST) // CHUNK
    visible = (dchunk >= 0) & (dchunk <= N_PAST_CHUNKS)
    return jnp.where(visible[None], table, NEG)


def _mixer_a_schedule(qn, kn, rel_bias):
    worst = jnp.max(qn, axis=1) * jnp.max(kn, axis=1) + jnp.max(jnp.abs(rel_bias), axis=-1)[None, :] * LOG2E
    ok = worst < PLAIN_EXP_LOG2
    return jnp.all(ok.reshape(-1, N_HEAD_PAIRS, HEADS_PER_TILE), axis=-1).reshape(-1).astype(jnp.int32)


def _mixer_a(plain, q, k, vt, z, bias, batch, seq):
    seq_rows, seq_cols, spec = _attn_specs(seq, pair_major=True)
    return pl.pallas_call(
        _mixer_a_kernel,
        grid_spec=pltpu.PrefetchScalarGridSpec(
            num_scalar_prefetch=1,
            grid=(N_HEAD_PAIRS, batch),
            in_specs=[seq_rows, seq_rows, seq_cols, seq_rows,
                      spec((HEADS_PER_TILE, A_BIAS_ROWS, TQ), lambda b, hp: (hp, 0, 0))],
            out_specs=seq_rows,
            scratch_shapes=_state_scratch(Q_SUB_A)),
        out_shape=jax.ShapeDtypeStruct(q.shape, jnp.bfloat16),
        compiler_params=_params("arbitrary", "arbitrary"),
        name="mixer_a",
    )(plain, q, k, vt, z, bias)


TG_B = Q_SUB_B * TQ
KV_PER_G = TG_B // TKV
MAX_JOINED = 3
N_AUG = HEADS_PER_TILE * N_SPLIT
AUG_STRIDE = LANES // N_HEAD_PAIRS


def _fox_prep_kernel(lf_ref, bf_ref, tri_ref, sel_ref, ones_ref, gk_ref, gq_ref, st_ref, carry_ref):
    @pl.when(pl.program_id(1) == 0)
    def _():
        carry_ref[...] = jnp.zeros_like(carry_ref)

    pre = lf_ref[...] + bf_ref[...]
    logf = (jnp.minimum(pre, 0.0) - jnp.log(1.0 + jnp.exp(-jnp.abs(pre)))) * LOG2E
    tri = tri_ref[...]
    csum = sum(jnp.dot(tri, piece, preferred_element_type=jnp.float32) for piece in _split(logf, N_SPLIT))
    f = csum + carry_ref[...]
    carry_ref[...] = f[TKV - 1:TKV, :]
    pieces = _split(-f, N_SPLIT)
    for side, o_ref in enumerate((gk_ref, gq_ref)):
        o_ref[...] = (sum(jnp.dot(piece, sel_ref[side, n], preferred_element_type=jnp.float32)
                          for n, piece in enumerate(pieces)) + ones_ref[side:side + 1, :]).astype(jnp.bfloat16)
    st_ref[...] = jnp.concatenate(
        [f[0:1, :], f[TKV - 1:TKV, :], jnp.zeros((N_STATS - 2, LANES), jnp.float32)], axis=0)


def _fox_prep(lf, b_f, batch, seq):
    nb = seq // TKV
    bf = jnp.pad(b_f.astype(jnp.float32), (0, LANES - N_HEADS)).reshape(1, LANES)
    tri = (jnp.arange(TKV)[:, None] >= jnp.arange(TKV)[None, :]).astype(jnp.bfloat16)
    src = jnp.arange(LANES)[:, None]
    dst = jnp.arange(LANES)[None, :]
    base = (src // HEADS_PER_TILE) * AUG_STRIDE + (src % HEADS_PER_TILE) * N_SPLIT
    sel = jnp.stack([
        jnp.stack([jnp.where((dst == base + side * N_AUG + n) & (src < N_HEADS), 1.0 - 2.0 * side, 0.0)
                   for n in range(N_SPLIT)]) for side in range(2)]).astype(jnp.bfloat16)
    lane = jnp.arange(LANES) % AUG_STRIDE
    ones = jnp.stack([(lane >= N_AUG) & (lane < 2 * N_AUG), lane < N_AUG]).astype(jnp.float32)
    tokspec = pl.BlockSpec((TKV, LANES), lambda b, i: (b * nb + i, 0))
    aug = jax.ShapeDtypeStruct((batch * seq, LANES), jnp.bfloat16)
    return pl.pallas_call(
        _fox_prep_kernel,
        grid=(batch, nb),
        in_specs=[tokspec,
                  pl.BlockSpec((1, LANES), lambda b, i: (0, 0)),
                  pl.BlockSpec((TKV, TKV), lambda b, i: (0, 0)),
                  pl.BlockSpec((2, N_SPLIT, LANES, LANES), lambda b, i: (0, 0, 0, 0)),
                  pl.BlockSpec((2, LANES), lambda b, i: (0, 0))],
        out_specs=[tokspec, tokspec, pl.BlockSpec((None, None, N_STATS, LANES), lambda b, i: (b, i, 0, 0))],
        out_shape=[aug, aug, jax.ShapeDtypeStruct((batch, nb, N_STATS, LANES), jnp.float32)],
        scratch_shapes=[pltpu.VMEM((1, LANES), jnp.float32)],
        compiler_params=_params("arbitrary", "arbitrary"),
        name="fox_prep",
    )(lf, bf, tri, sel, ones)


def _fox_schedule(qn, kn, f_stats):
    b, nb = f_stats.shape[0], f_stats.shape[1]
    ng = nb // KV_PER_G
    f_first, f_last = f_stats[:, :, 0, :N_HEADS], f_stats[:, :, 1, :N_HEADS]
    qn_g = jnp.max(qn.reshape(b, ng, KV_PER_G, N_HEADS), axis=2)
    kn_g = jnp.max(kn.reshape(b, ng, KV_PER_G, N_HEADS), axis=2)
    f_first_g = f_first.reshape(b, ng, KV_PER_G, N_HEADS)[:, :, 0]
    bound = (qn_g[:, :, None] * kn[:, None, :] + f_first_g[:, :, None] - f_last[:, None, :]
             + (qn_g * kn_g)[:, :, None])
    i_idx = jnp.arange(ng)[None, :, None, None]
    j_idx = jnp.arange(nb)[None, None, :, None]
    skip = (bound < -PRUNE_LOG2) & (j_idx < i_idx * KV_PER_G)
    first = jnp.min(jnp.where(skip, nb, j_idx), axis=2)
    first = jnp.min(first.reshape(b, ng, N_HEAD_PAIRS, HEADS_PER_TILE), axis=-1)
    first = jnp.transpose(first, (0, 2, 1)).reshape(-1)
    worst = jnp.max(qn, axis=1) * jnp.max(kn, axis=1)
    plain = jnp.all((worst < PLAIN_EXP_LOG2).reshape(b, N_HEAD_PAIRS, HEADS_PER_TILE), axis=-1).reshape(-1)
    return jnp.concatenate([first, plain.astype(first.dtype)]).astype(jnp.int32)


def _fox_kernel(sched_ref, q_ref, gq_ref, k_ref, gk_ref, vt_ref, z_ref, o_ref, u_ref, acc_ref, qa_ref):
    hp = pl.program_id(1)
    n_groups = q_ref.shape[0] // TG_B
    pair = pl.program_id(0) * N_HEAD_PAIRS + hp
    plain = sched_ref[pl.num_programs(0) * N_HEAD_PAIRS * n_groups + pair]
    n_chains = Q_SUB_B * HEADS_PER_TILE

    def keys_at(row0, n):
        rows = pl.ds(row0, n)
        return jnp.concatenate([k_ref[rows, :], gk_ref[rows, :]], axis=1), vt_ref[:, rows]

    def stage_queries(gi):
        gi = jnp.asarray(gi, jnp.int32)
        slot = gi & 1
        gi = jnp.minimum(gi, n_groups - 1)
        for c in range(n_chains):
            r, h = divmod(c, HEADS_PER_TILE)
            rows = pl.ds(pl.multiple_of(gi * TG_B + r * TQ, TQ), TQ)
            gq = gq_ref[rows, :]
            off = lax.broadcasted_iota(jnp.int32, (TQ, LANES), 1) - hp * AUG_STRIDE
            own = ((off >= N_SPLIT * h) & (off < N_SPLIT * (h + 1))) | \
                  ((off >= N_AUG + N_SPLIT * h) & (off < N_AUG + N_SPLIT * (h + 1)))
            qa_ref[slot, c] = jnp.concatenate([_head_masked_queries(q_ref[rows, :])[h],
                                         jnp.where(own, gq, jnp.zeros_like(gq))], axis=1)

    _reset(u_ref, acc_ref)
    stage_queries(0)

    def group(gi, update):
        j_first = sched_ref[pair * n_groups + gi]

        def qrow(r):
            return pl.multiple_of(gi * TG_B + r * TQ, TQ)


        def keys_block(j, n_steps):
            kaug, vt_blk = keys_at(pl.multiple_of(j * TKV, TKV), n_steps * TKV)
            _lagged(n_chains,
                    lambda c: _dot_nt(kaug, qa_ref[gi & 1, c]),
                    lambda c, a: update(u_ref, acc_ref, c, a, _head_values(vt_blk, c % HEADS_PER_TILE)),
                    ISSUE_LAG)

        n_before = gi * KV_PER_G - j_first
        n_joined = jnp.where((n_before & 1) != 0, jnp.minimum(n_before, MAX_JOINED), jnp.minimum(n_before, 2))

        def two_steps(p, carry):
            keys_block(j_first + n_joined + 2 * p, 2)
            return carry

        lax.fori_loop(0, lax.shift_right_logical(n_before - n_joined, 1), two_steps, 0)

        def own_block(joined):
            kaug, vt_blk = keys_at(pl.multiple_of(gi * TG_B, TG_B), TG_B)
            s_loc = lax.broadcasted_iota(jnp.int32, (TQ, TQ), 0)
            t_loc = lax.broadcasted_iota(jnp.int32, (TQ, TQ), 1)
            n_extra = n_chains if joined else 0
            if joined:
                kaug_e, vt_e = keys_at(pl.multiple_of(j_first * TKV, TKV), joined * TKV)

            def issue(i):
                if i < n_extra:
                    return _dot_nt(kaug_e, qa_ref[gi & 1, i])
                c = i - n_extra
                r = c // HEADS_PER_TILE
                a = _dot_nt(kaug[:(r + 1) * TQ], qa_ref[gi & 1, c])
                diag = jnp.where(s_loc <= t_loc, a[r * TQ:], NEG)
                return diag if r == 0 else jnp.concatenate([a[:r * TQ], diag], axis=0)

            def consume(i, a):
                if i < n_extra:
                    update(u_ref, acc_ref, i, a, _head_values(vt_e, i % HEADS_PER_TILE))
                    return
                c = i - n_extra
                r, h = divmod(c, HEADS_PER_TILE)
                update(u_ref, acc_ref, c, a, _head_values(vt_blk[:, :(r + 1) * TQ], h))
                if h == HEADS_PER_TILE - 1:
                    _finish(u_ref, acc_ref, z_ref, o_ref, r, qrow(r), reinit=True)

            _lagged(n_extra + n_chains, issue, consume, ISSUE_LAG)
            stage_queries(gi + 1)

        for joined in range(MAX_JOINED + 1):
            pl.when(n_joined == joined)(functools.partial(own_block, joined))

    def run(update):
        def body(gi, carry):
            group(gi, update)
            return carry
        lax.fori_loop(0, n_groups, body, 0)

    @pl.when(plain != 0)
    def _():
        run(_plain_update)

    @pl.when(plain == 0)
    def _():
        run(_online_update)


def _fox(sched, q, gq, k, gk, vt, z, batch, seq):
    seq_rows, seq_cols, spec = _attn_specs(seq)
    aug_seq = spec((seq, LANES), lambda b, hp: (b, 0))
    return pl.pallas_call(
        _fox_kernel,
        grid_spec=pltpu.PrefetchScalarGridSpec(
            num_scalar_prefetch=1,
            grid=(batch, N_HEAD_PAIRS),
            in_specs=[seq_rows, aug_seq, seq_rows, aug_seq, seq_cols, seq_rows],
            out_specs=seq_rows,
            scratch_shapes=_state_scratch(Q_SUB_B)
            + [pltpu.VMEM((2, Q_SUB_B * HEADS_PER_TILE, TQ, 2 * LANES), jnp.bfloat16)]),
        out_shape=jax.ShapeDtypeStruct(q.shape, jnp.bfloat16),
        compiler_params=_params("arbitrary", "arbitrary"),
        name="fox",
    )(sched, q, gq, k, gk, vt, z)


def _in_weights(w_in, w_gate=None):
    w_in = w_in.astype(jnp.float32)
    e = MIX_WIDTH
    cols = [w_in[:, :e] * (LOG2E / math.sqrt(HEAD_DIM)), w_in[:, e:2 * e], w_in[:, 3 * e:]]
    if w_gate is not None:
        cols.append(jnp.pad(w_gate.astype(jnp.float32), ((0, 0), (0, LANES - N_HEADS))))
    return jnp.concatenate(cols, axis=1).astype(jnp.bfloat16), w_in[:, 2 * e:3 * e].T.astype(jnp.bfloat16)


def _reorder_heads(order, w_in, w_f, b_f, w_out):
    d = w_in.shape[0]
    w_in = w_in.reshape(d, 4, N_HEADS, HEAD_DIM)[:, :, order].reshape(d, 4 * MIX_WIDTH)
    w_out = w_out.reshape(N_HEADS, HEAD_DIM, -1)[order].reshape(MIX_WIDTH, -1)
    return w_in, w_f[:, order], b_f[order], w_out


def kernel(x, w_in_a, rel_bias_a, w_out_a, w_in_b, w_f_b, b_f_b, w_out_b, ln_g, ln_b):
    batch, seq, d = x.shape
    assert seq % (Q_SUB_A * TQ) == 0 and seq % TG_B == 0 and TG_B % TKV == 0 and TKV == TM_PROJ
    assert seq >= A_WINDOW and N_HEAD_PAIRS * AUG_STRIDE == LANES and 2 * N_AUG <= AUG_STRIDE
    xf = x.reshape(batch * seq, d).astype(jnp.float32)

    layers = []
    for i in range(DEPTH):
        j = i // N_MIXERS
        if i % N_MIXERS == 0:
            layers.append((_in_weights(w_in_a[j]), w_out_a[j].astype(jnp.bfloat16), None))
        else:
            order = jnp.argsort(b_f_b[j])
            w_in, w_f, b_f, w_out = _reorder_heads(order, w_in_b[j], w_f_b[j], b_f_b[j], w_out_b[j])
            layers.append((_in_weights(w_in, w_f), w_out.astype(jnp.bfloat16), b_f))

    outs = _proj(xf, nxt=layers[0][0])
    for i in range(DEPTH):
        _, w_out, b_f = layers[i]
        j = i // N_MIXERS
        if b_f is None:
            q, k, vt, z, nst = outs
            plain = _mixer_a_schedule(*_max_norms(nst, batch), rel_bias_a[j])
            g = _mixer_a(plain, q, k, vt, z, _mixer_a_bias(rel_bias_a[j]), batch, seq)
        else:
            q, k, vt, z, nst, lf = outs
            gk, gq, f_stats = _fox_prep(lf, b_f, batch, seq)
            g = _fox(_fox_schedule(*_max_norms(nst, batch), f_stats), q, gq, k, gk, vt, z, batch, seq)
        xf, *outs = _proj(xf, prev=(g, w_out, ln_g[i], ln_b[i]),
                          nxt=layers[i + 1][0] if i + 1 < DEPTH else None)
    return xf.reshape(batch, seq, d).astype(x.dtype)
```

```python
import functools
import math

import jax
import jax.numpy as jnp
from jax import lax
from jax.experimental import pallas as pl
from jax.experimental.pallas import tpu as pltpu

D_MODEL = 1024
N_HEADS = 16
HEAD_DIM = 64
MIX_WIDTH = N_HEADS * HEAD_DIM
CHUNK = 64
N_PAST_CHUNKS = 8
MAX_REL = 128
DEPTH = 4
N_MIXERS = 2
LN_EPS = 1e-5
DEEPNORM_ALPHA = (2.0 * DEPTH) ** 0.25
LOG2E = math.log2(math.e)
NEG = -1e30

LANES = 128
HEADS_PER_TILE = LANES // HEAD_DIM
N_HEAD_PAIRS = N_HEADS // HEADS_PER_TILE
BF16_ROWS = 16
N_SPLIT = 3
TQ = 256
Q_SUB_A = 16
Q_SUB_B = 4
TKV = 512
TM_PROJ = 512
ISSUE_LAG = 8
VMEM_LIMIT = 56 * 1024 * 1024
PRUNE_LOG2 = 152.0
NORM_SLACK = 1.0 + 2.0 ** -6
PLAIN_EXP_LOG2 = 60.0


def _params(*semantics):
    return pltpu.CompilerParams(dimension_semantics=semantics, vmem_limit_bytes=VMEM_LIMIT)


def _dot_nt(a, b):
    return lax.dot_general(a, b, (((1,), (1,)), ((), ())), preferred_element_type=jnp.float32)


def _split(x, n):
    pieces = []
    for _ in range(n):
        p = x.astype(jnp.bfloat16)
        pieces.append(p)
        x = x - p.astype(jnp.float32)
    return pieces


N_STATS = 8
LN_ROW_CHUNKS = 2


def _lagged(n, issue, consume, lag):
    pending = {}
    for c in range(n + lag):
        if c < n:
            pending[c] = issue(c)
        if c >= lag:
            consume(c - lag, pending.pop(c - lag))


def _out_proj_ln_rows(g_ref, x_ref, wo_ref, gam_ref, bet_ref, xo_ref):
    rows = TM_PROJ // LN_ROW_CHUNKS
    out = []

    def issue(c):
        return jnp.dot(g_ref[pl.ds(c * rows, rows), :], wo_ref[...], preferred_element_type=jnp.float32)

    def consume(c, y):
        sl = pl.ds(c * rows, rows)
        r = DEEPNORM_ALPHA * x_ref[sl, :] + y
        mu = jnp.mean(r, axis=-1, keepdims=True)
        d = r - mu
        var = jnp.mean(d * d, axis=-1, keepdims=True)
        xn = d * lax.rsqrt(var + LN_EPS) * gam_ref[...] + bet_ref[...]
        xo_ref[sl, :] = xn
        out.append(xn)

    _lagged(LN_ROW_CHUNKS, issue, consume, lag=1)
    return jnp.concatenate(out, axis=0)


def _in_proj_rows(x, w_ref, wvt_ref, hsel_ref, q_ref, k_ref, vt_ref, z_ref, nst_ref, gate_ref):
    xb = x.astype(jnp.bfloat16)
    sq_max = []
    for g, o_ref in enumerate((q_ref, k_ref, z_ref)):
        o = jnp.dot(xb, w_ref[:, g * MIX_WIDTH:(g + 1) * MIX_WIDTH], preferred_element_type=jnp.float32)
        if g == 2:
            o = o / (1.0 + jnp.exp(-o))
        o = o.astype(jnp.bfloat16)
        o_ref[...] = o
        if g < 2:
            of = o.astype(jnp.float32)
            sq = jnp.dot((of * of).astype(jnp.bfloat16), hsel_ref[...], preferred_element_type=jnp.float32)
            sq_max.append(jnp.max(sq, axis=0, keepdims=True))
    nst_ref[...] = jnp.concatenate(sq_max + [jnp.zeros((N_STATS - 2, LANES), jnp.float32)], axis=0)
    vt_ref[...] = _dot_nt(wvt_ref[...], xb).astype(jnp.bfloat16)
    if gate_ref is not None:
        gate_ref[...] = jnp.dot(xb, w_ref[:, 3 * MIX_WIDTH:], preferred_element_type=jnp.float32)


def _proj_kernel(*refs, has_out, has_in, has_gate):
    refs = list(refs)
    n_in = (5 if has_out else 1) + (3 if has_in else 0)
    ins, outs = refs[:n_in], refs[n_in:]
    if has_out:
        x = _out_proj_ln_rows(*ins[:5], outs.pop(0))
        ins = ins[5:]
    else:
        x = ins.pop(0)[...]
    if has_in:
        _in_proj_rows(x, *ins, *outs[:5], outs[5] if has_gate else None)


def _proj(xf, prev=None, nxt=None):
    m = xf.shape[0]
    row_d = pl.BlockSpec((TM_PROJ, D_MODEL), lambda i: (i, 0))
    row_e = pl.BlockSpec((TM_PROJ, MIX_WIDTH), lambda i: (i, 0))

    def whole(shape):
        return pl.BlockSpec(shape, lambda i: (0,) * len(shape))

    args, in_specs, out_shape, out_specs = [], [], [], []
    if prev is not None:
        g, w_out, gamma, beta = prev
        args += [g, xf, w_out, gamma.reshape(1, D_MODEL), beta.reshape(1, D_MODEL)]
        in_specs += [row_e, row_d, whole((MIX_WIDTH, D_MODEL)), whole((1, D_MODEL)), whole((1, D_MODEL))]
        out_shape.append(jax.ShapeDtypeStruct((m, D_MODEL), jnp.float32))
        out_specs.append(row_d)
    else:
        args.append(xf)
        in_specs.append(row_d)
    has_gate = False
    if nxt is not None:
        w, wvt = nxt
        has_gate = w.shape[1] > 3 * MIX_WIDTH
        hsel = (jnp.arange(MIX_WIDTH)[:, None] // HEAD_DIM == jnp.arange(LANES)[None, :]).astype(jnp.bfloat16)
        args += [w, wvt, hsel]
        in_specs += [whole(w.shape), whole(wvt.shape), whole(hsel.shape)]
        tok = jax.ShapeDtypeStruct((m, MIX_WIDTH), jnp.bfloat16)
        out_shape += [tok, tok, jax.ShapeDtypeStruct((MIX_WIDTH, m), jnp.bfloat16), tok,
                      jax.ShapeDtypeStruct((m // TM_PROJ, N_STATS, LANES), jnp.float32)]
        out_specs += [row_e, row_e, pl.BlockSpec((MIX_WIDTH, TM_PROJ), lambda i: (0, i)), row_e,
                      pl.BlockSpec((None, N_STATS, LANES), lambda i: (i, 0, 0))]
        if has_gate:
            out_shape.append(jax.ShapeDtypeStruct((m, LANES), jnp.float32))
            out_specs.append(pl.BlockSpec((TM_PROJ, LANES), lambda i: (i, 0)))
    name = (("out_" if prev is not None else "") + ("in_" if nxt is not None else "") + "proj"
            + ("_gate" if has_gate else ""))
    return pl.pallas_call(
        functools.partial(_proj_kernel, has_out=prev is not None, has_in=nxt is not None, has_gate=has_gate),
        grid=(m // TM_PROJ,),
        in_specs=in_specs,
        out_specs=out_specs,
        out_shape=out_shape,
        compiler_params=_params("arbitrary"),
        name=name,
    )(*args)


def _max_norms(nst, batch):
    nb = nst.shape[0] // batch
    n = jnp.sqrt(nst[:, :2, :N_HEADS]).reshape(batch, nb, 2, N_HEADS) * NORM_SLACK
    return n[:, :, 0], n[:, :, 1]


ACC_ROWS = HEAD_DIM + BF16_ROWS
U_ROWS = 8


def _head_masked_queries(q2):
    lane = lax.broadcasted_iota(jnp.int32, q2.shape, 1)
    zero = jnp.zeros_like(q2)
    return [jnp.where((lane >= HEAD_DIM * h) & (lane < HEAD_DIM * (h + 1)), q2, zero)
            for h in range(HEADS_PER_TILE)]


def _head_values(vt_blk, h):
    ones = jnp.ones((BF16_ROWS, vt_blk.shape[1]), vt_blk.dtype)
    return jnp.concatenate([vt_blk[HEAD_DIM * h:HEAD_DIM * (h + 1), :], ones], axis=0)


def _online_update(u_ref, acc_ref, c, a, vt_h):
    u = u_ref[c, 0:1, :]
    u_new = jnp.maximum(u, jnp.max(a, axis=0, keepdims=True))
    p = jnp.exp2(a - u_new).astype(jnp.bfloat16)
    acc_ref[c] = jnp.exp2(u - u_new) * acc_ref[c] + jnp.dot(vt_h, p, preferred_element_type=jnp.float32)
    u_ref[c, 0:1, :] = u_new


def _plain_update(u_ref, acc_ref, c, a, vt_h):
    del u_ref
    acc_ref[c] = acc_ref[c] + jnp.dot(vt_h, jnp.exp2(a).astype(jnp.bfloat16),
                                      preferred_element_type=jnp.float32)


def _reset(u_ref, acc_ref):
    u_ref[...] = jnp.full(u_ref.shape, NEG, jnp.float32)
    acc_ref[...] = jnp.zeros(acc_ref.shape, jnp.float32)


def _finish(acc_ref, z_ref, o_ref, r, row0):
    accs = [acc_ref[r * HEADS_PER_TILE + h] for h in range(HEADS_PER_TILE)]
    ot = jnp.concatenate([acc[:HEAD_DIM] / acc[HEAD_DIM:HEAD_DIM + 1] for acc in accs], axis=0)
    rows = pl.ds(row0, TQ)
    o_ref[rows, :] = (ot.T * z_ref[rows, :].astype(jnp.float32)).astype(o_ref.dtype)


def _attn_specs(seq, pair_major=False):
    def spec(block, index):
        if pair_major:
            return pl.BlockSpec(block, lambda hp, b, *_: index(b, hp))
        return pl.BlockSpec(block, lambda b, hp, *_: index(b, hp))

    seq_rows = spec((seq, LANES), lambda b, hp: (b, hp))
    seq_cols = spec((LANES, seq), lambda b, hp: (hp, b))
    return seq_rows, seq_cols, spec


def _state_scratch(q_sub):
    n = q_sub * HEADS_PER_TILE
    return [pltpu.VMEM((n, U_ROWS, TQ), jnp.float32), pltpu.VMEM((n, ACC_ROWS, TQ), jnp.float32)]


A_PAST = N_PAST_CHUNKS * CHUNK
A_WINDOW = A_PAST + TQ
A_BIAS_ROWS = 2 * A_PAST + TQ
A_BAND_LO = A_PAST - MAX_REL
A_BAND_HI = A_PAST + TQ + CHUNK


def _mixer_a_kernel(plain_ref, q_ref, k_ref, vt_ref, z_ref, bias_ref, o_ref, u_ref, acc_ref):
    plain = plain_ref[pl.program_id(1) * N_HEAD_PAIRS + pl.program_id(0)]
    n_groups = q_ref.shape[0] // (Q_SUB_A * TQ)

    def group(gi, update):
        _reset(u_ref, acc_ref)

        def qrow(r):
            return pl.multiple_of((gi * Q_SUB_A + r) * TQ, TQ)

        def window(r):
            start = qrow(r) - A_PAST
            row0 = pl.multiple_of(jnp.maximum(start, 0), TQ)
            return row0, pl.multiple_of(row0 - start, TQ)

        def issue(c):
            r, h = divmod(c, HEADS_PER_TILE)
            row0, brow = window(r)
            qm = _head_masked_queries(q_ref[pl.ds(qrow(r), TQ), :])[h]
            return _dot_nt(k_ref[pl.ds(row0, A_WINDOW), :], qm) + bias_ref[h, pl.ds(brow, A_WINDOW), :]

        def consume(c, a):
            r, h = divmod(c, HEADS_PER_TILE)
            row0, _ = window(r)
            update(u_ref, acc_ref, c, a, _head_values(vt_ref[:, pl.ds(row0, A_WINDOW)], h))
            if h == HEADS_PER_TILE - 1:
                _finish(acc_ref, z_ref, o_ref, r, qrow(r))

        _lagged(Q_SUB_A * HEADS_PER_TILE, issue, consume, ISSUE_LAG)

    def run(update):
        def body(gi, carry):
            group(gi, update)
            return carry
        lax.fori_loop(0, n_groups, body, 0)

    @pl.when(plain != 0)
    def _():
        run(_plain_update)

    @pl.when(plain == 0)
    def _():
        run(_online_update)


A_BAND = A_BAND_HI - A_BAND_LO
A_STRIP = 1024
A_STRIP_OFF = 512


def _bias_band_kernel(s_ref, o_ref):
    x = jnp.broadcast_to(s_ref[0], (A_BAND, A_STRIP))
    o_ref[0] = pltpu.roll(x, 0, 1, stride=1, stride_axis=0)[:, A_STRIP_OFF:A_STRIP_OFF + TQ]


def _mixer_a_bias(rel_bias):
    assert A_STRIP_OFF >= A_BAND - 1 and A_STRIP_OFF + TQ <= A_STRIP
    rb = rel_bias.astype(jnp.float32) * LOG2E
    c = jnp.arange(A_BIAS_ROWS)[:, None]
    r = jnp.arange(TQ)[None, :]
    strip_dist = jnp.arange(A_STRIP) - A_STRIP_OFF + (A_PAST - A_BAND_LO)
    strip = rb[:, jnp.clip(strip_dist, -MAX_REL, MAX_REL) + MAX_REL].reshape(N_HEADS, 1, A_STRIP)
    band = pl.pallas_call(
        _bias_band_kernel,
        grid=(N_HEADS,),
        in_specs=[pl.BlockSpec((1, 1, A_STRIP), lambda h: (h, 0, 0))],
        out_specs=pl.BlockSpec((1, A_BAND, TQ), lambda h: (h, 0, 0)),
        out_shape=jax.ShapeDtypeStruct((N_HEADS, A_BAND, TQ), jnp.float32),
        compiler_params=_params("arbitrary"),
        name="bias_band",
    )(strip)
    far = jnp.broadcast_to(rb[:, 2 * MAX_REL][:, None, None], (N_HEADS, A_BAND_LO, TQ))
    late = jnp.zeros((N_HEADS, A_BIAS_ROWS - A_BAND_HI, TQ), jnp.float32)
    table = jnp.concatenate([far, band, late], axis=1)
    dchunk = r // CHUNK - (c - A_PAST) // CHUNK
    visible = (dchunk >= 0) & (dchunk <= N_PAST_CHUNKS)
    return jnp.where(visible[None], table, NEG)


def _mixer_a_schedule(qn, kn, rel_bias):
    worst = jnp.max(qn, axis=1) * jnp.max(kn, axis=1) + jnp.max(jnp.abs(rel_bias), axis=-1)[None, :] * LOG2E
    ok = worst < PLAIN_EXP_LOG2
    return jnp.all(ok.reshape(-1, N_HEAD_PAIRS, HEADS_PER_TILE), axis=-1).reshape(-1).astype(jnp.int32)


def _mixer_a(plain, q, k, vt, z, bias, batch, seq):
    seq_rows, seq_cols, spec = _attn_specs(seq, pair_major=True)
    return pl.pallas_call(
        _mixer_a_kernel,
        grid_spec=pltpu.PrefetchScalarGridSpec(
            num_scalar_prefetch=1,
            grid=(N_HEAD_PAIRS, batch),
            in_specs=[seq_rows, seq_rows, seq_cols, seq_rows,
                      spec((HEADS_PER_TILE, A_BIAS_ROWS, TQ), lambda b, hp: (hp, 0, 0))],
            out_specs=seq_rows,
            scratch_shapes=_state_scratch(Q_SUB_A)),
        out_shape=jax.ShapeDtypeStruct(q.shape, jnp.bfloat16),
        compiler_params=_params("arbitrary", "arbitrary"),
        name="mixer_a",
    )(plain, q, k, vt, z, bias)


TG_B = Q_SUB_B * TQ
MAX_JOINED = 3
KV_PER_G = TG_B // TKV
N_AUG = HEADS_PER_TILE * N_SPLIT
AUG_STRIDE = LANES // N_HEAD_PAIRS


def _fox_prep_kernel(lf_ref, bf_ref, tri_ref, sel_ref, ones_ref, gk_ref, gq_ref, st_ref, carry_ref):
    @pl.when(pl.program_id(1) == 0)
    def _():
        carry_ref[...] = jnp.zeros_like(carry_ref)

    pre = lf_ref[...] + bf_ref[...]
    logf = (jnp.minimum(pre, 0.0) - jnp.log(1.0 + jnp.exp(-jnp.abs(pre)))) * LOG2E
    tri = tri_ref[...]
    csum = sum(jnp.dot(tri, piece, preferred_element_type=jnp.float32) for piece in _split(logf, N_SPLIT))
    f = csum + carry_ref[...]
    carry_ref[...] = f[TKV - 1:TKV, :]
    pieces = _split(-f, N_SPLIT)
    for side, o_ref in enumerate((gk_ref, gq_ref)):
        o_ref[...] = (sum(jnp.dot(piece, sel_ref[side, n], preferred_element_type=jnp.float32)
                          for n, piece in enumerate(pieces)) + ones_ref[side:side + 1, :]).astype(jnp.bfloat16)
    st_ref[...] = jnp.concatenate(
        [f[0:1, :], f[TKV - 1:TKV, :], jnp.zeros((N_STATS - 2, LANES), jnp.float32)], axis=0)


def _fox_prep(lf, b_f, batch, seq):
    nb = seq // TKV
    bf = jnp.pad(b_f.astype(jnp.float32), (0, LANES - N_HEADS)).reshape(1, LANES)
    tri = (jnp.arange(TKV)[:, None] >= jnp.arange(TKV)[None, :]).astype(jnp.bfloat16)
    src = jnp.arange(LANES)[:, None]
    dst = jnp.arange(LANES)[None, :]
    base = (src // HEADS_PER_TILE) * AUG_STRIDE + (src % HEADS_PER_TILE) * N_SPLIT
    sel = jnp.stack([
        jnp.stack([jnp.where((dst == base + side * N_AUG + n) & (src < N_HEADS), 1.0 - 2.0 * side, 0.0)
                   for n in range(N_SPLIT)]) for side in range(2)]).astype(jnp.bfloat16)
    lane = jnp.arange(LANES) % AUG_STRIDE
    ones = jnp.stack([(lane >= N_AUG) & (lane < 2 * N_AUG), lane < N_AUG]).astype(jnp.float32)
    tokspec = pl.BlockSpec((TKV, LANES), lambda b, i: (b * nb + i, 0))
    aug = jax.ShapeDtypeStruct((batch * seq, LANES), jnp.bfloat16)
    return pl.pallas_call(
        _fox_prep_kernel,
        grid=(batch, nb),
        in_specs=[tokspec,
                  pl.BlockSpec((1, LANES), lambda b, i: (0, 0)),
                  pl.BlockSpec((TKV, TKV), lambda b, i: (0, 0)),
                  pl.BlockSpec((2, N_SPLIT, LANES, LANES), lambda b, i: (0, 0, 0, 0)),
                  pl.BlockSpec((2, LANES), lambda b, i: (0, 0))],
        out_specs=[tokspec, tokspec, pl.BlockSpec((None, None, N_STATS, LANES), lambda b, i: (b, i, 0, 0))],
        out_shape=[aug, aug, jax.ShapeDtypeStruct((batch, nb, N_STATS, LANES), jnp.float32)],
        scratch_shapes=[pltpu.VMEM((1, LANES), jnp.float32)],
        compiler_params=_params("arbitrary", "arbitrary"),
        name="fox_prep",
    )(lf, bf, tri, sel, ones)


def _fox_schedule(qn, kn, f_stats):
    b, nb = f_stats.shape[0], f_stats.shape[1]
    ng = nb // KV_PER_G
    f_first, f_last = f_stats[:, :, 0, :N_HEADS], f_stats[:, :, 1, :N_HEADS]
    qn_g = jnp.max(qn.reshape(b, ng, KV_PER_G, N_HEADS), axis=2)
    kn_g = jnp.max(kn.reshape(b, ng, KV_PER_G, N_HEADS), axis=2)
    f_first_g = f_first.reshape(b, ng, KV_PER_G, N_HEADS)[:, :, 0]
    bound = (qn_g[:, :, None] * kn[:, None, :] + f_first_g[:, :, None] - f_last[:, None, :]
             + (qn_g * kn_g)[:, :, None])
    i_idx = jnp.arange(ng)[None, :, None, None]
    j_idx = jnp.arange(nb)[None, None, :, None]
    skip = (bound < -PRUNE_LOG2) & (j_idx < i_idx * KV_PER_G)
    first = jnp.min(jnp.where(skip, nb, j_idx), axis=2)
    first = jnp.min(first.reshape(b, ng, N_HEAD_PAIRS, HEADS_PER_TILE), axis=-1)
    first = jnp.transpose(first, (0, 2, 1)).reshape(-1)
    worst = jnp.max(qn, axis=1) * jnp.max(kn, axis=1)
    plain = jnp.all((worst < PLAIN_EXP_LOG2).reshape(b, N_HEAD_PAIRS, HEADS_PER_TILE), axis=-1).reshape(-1)
    return jnp.concatenate([first, plain.astype(first.dtype)]).astype(jnp.int32)


def _fox_kernel(sched_ref, q_ref, gq_ref, k_ref, gk_ref, vt_ref, z_ref, o_ref, u_ref, acc_ref):
    hp = pl.program_id(1)
    n_groups = q_ref.shape[0] // TG_B
    pair = pl.program_id(0) * N_HEAD_PAIRS + hp
    plain = sched_ref[pl.num_programs(0) * N_HEAD_PAIRS * n_groups + pair]
    n_chains = Q_SUB_B * HEADS_PER_TILE

    def keys_at(row0, n):
        rows = pl.ds(row0, n)
        return jnp.concatenate([k_ref[rows, :], gk_ref[rows, :]], axis=1), vt_ref[:, rows]

    def group(gi, update):
        j_first = sched_ref[pair * n_groups + gi]

        def qrow(r):
            return pl.multiple_of(gi * TG_B + r * TQ, TQ)

        def qaug(c):
            r, h = divmod(c, HEADS_PER_TILE)
            rows = pl.ds(qrow(r), TQ)
            gq = gq_ref[rows, :]
            off = lax.broadcasted_iota(jnp.int32, (TQ, LANES), 1) - hp * AUG_STRIDE
            own = ((off >= N_SPLIT * h) & (off < N_SPLIT * (h + 1))) | \
                  ((off >= N_AUG + N_SPLIT * h) & (off < N_AUG + N_SPLIT * (h + 1)))
            return jnp.concatenate([_head_masked_queries(q_ref[rows, :])[h],
                                    jnp.where(own, gq, jnp.zeros_like(gq))], axis=1)

        _reset(u_ref, acc_ref)
        qa = [qaug(c) for c in range(n_chains)]

        def keys_block(j, n_steps):
            kaug, vt_blk = keys_at(pl.multiple_of(j * TKV, TKV), n_steps * TKV)
            _lagged(n_chains,
                    lambda c: _dot_nt(kaug, qa[c]),
                    lambda c, a: update(u_ref, acc_ref, c, a, _head_values(vt_blk, c % HEADS_PER_TILE)),
                    ISSUE_LAG)

        n_before = gi * KV_PER_G - j_first
        n_joined = jnp.where((n_before & 1) != 0, jnp.minimum(n_before, MAX_JOINED), jnp.minimum(n_before, 2))

        def two_steps(p, carry):
            keys_block(j_first + n_joined + 2 * p, 2)
            return carry

        lax.fori_loop(0, lax.shift_right_logical(n_before - n_joined, 1), two_steps, 0)

        def own_block(joined):
            kaug, vt_blk = keys_at(pl.multiple_of(gi * TG_B, TG_B), TG_B)
            s_loc = lax.broadcasted_iota(jnp.int32, (TQ, TQ), 0)
            t_loc = lax.broadcasted_iota(jnp.int32, (TQ, TQ), 1)
            n_extra = n_chains if joined else 0
            if joined:
                kaug_e, vt_e = keys_at(pl.multiple_of(j_first * TKV, TKV), joined * TKV)

            def issue(i):
                if i < n_extra:
                    return _dot_nt(kaug_e, qa[i])
                c = i - n_extra
                r = c // HEADS_PER_TILE
                a = _dot_nt(kaug[:(r + 1) * TQ], qa[c])
                diag = jnp.where(s_loc <= t_loc, a[r * TQ:], NEG)
                return diag if r == 0 else jnp.concatenate([a[:r * TQ], diag], axis=0)

            def consume(i, a):
                if i < n_extra:
                    update(u_ref, acc_ref, i, a, _head_values(vt_e, i % HEADS_PER_TILE))
                    return
                c = i - n_extra
                r, h = divmod(c, HEADS_PER_TILE)
                update(u_ref, acc_ref, c, a, _head_values(vt_blk[:, :(r + 1) * TQ], h))
                if h == HEADS_PER_TILE - 1:
                    _finish(acc_ref, z_ref, o_ref, r, qrow(r))

            _lagged(n_extra + n_chains, issue, consume, ISSUE_LAG)

        for joined in range(MAX_JOINED + 1):
            pl.when(n_joined == joined)(functools.partial(own_block, joined))

    def run(update):
        def body(gi, carry):
            group(gi, update)
            return carry
        lax.fori_loop(0, n_groups, body, 0)

    @pl.when(plain != 0)
    def _():
        run(_plain_update)

    @pl.when(plain == 0)
    def _():
        run(_online_update)


def _fox(sched, q, gq, k, gk, vt, z, batch, seq):
    seq_rows, seq_cols, spec = _attn_specs(seq)
    aug_seq = spec((seq, LANES), lambda b, hp: (b, 0))
    return pl.pallas_call(
        _fox_kernel,
        grid_spec=pltpu.PrefetchScalarGridSpec(
            num_scalar_prefetch=1,
            grid=(batch, N_HEAD_PAIRS),
            in_specs=[seq_rows, aug_seq, seq_rows, aug_seq, seq_cols, seq_rows],
            out_specs=seq_rows,
            scratch_shapes=_state_scratch(Q_SUB_B)),
        out_shape=jax.ShapeDtypeStruct(q.shape, jnp.bfloat16),
        compiler_params=_params("arbitrary", "arbitrary"),
        name="fox",
    )(sched, q, gq, k, gk, vt, z)


def _in_weights(w_in, w_gate=None):
    w_in = w_in.astype(jnp.float32)
    e = MIX_WIDTH
    cols = [w_in[:, :e] * (LOG2E / math.sqrt(HEAD_DIM)), w_in[:, e:2 * e], w_in[:, 3 * e:]]
    if w_gate is not None:
        cols.append(jnp.pad(w_gate.astype(jnp.float32), ((0, 0), (0, LANES - N_HEADS))))
    return jnp.concatenate(cols, axis=1).astype(jnp.bfloat16), w_in[:, 2 * e:3 * e].T.astype(jnp.bfloat16)


def _reorder_heads(order, w_in, w_f, b_f, w_out):
    d = w_in.shape[0]
    w_in = w_in.reshape(d, 4, N_HEADS, HEAD_DIM)[:, :, order].reshape(d, 4 * MIX_WIDTH)
    w_out = w_out.reshape(N_HEADS, HEAD_DIM, -1)[order].reshape(MIX_WIDTH, -1)
    return w_in, w_f[:, order], b_f[order], w_out


def kernel(x, w_in_a, rel_bias_a, w_out_a, w_in_b, w_f_b, b_f_b, w_out_b, ln_g, ln_b):
    batch, seq, d = x.shape
    assert seq % (Q_SUB_A * TQ) == 0 and seq % TG_B == 0 and TG_B % TKV == 0 and TKV == TM_PROJ
    assert seq >= A_WINDOW and N_HEAD_PAIRS * AUG_STRIDE == LANES and 2 * N_AUG <= AUG_STRIDE
    xf = x.reshape(batch * seq, d).astype(jnp.float32)

    layers = []
    for i in range(DEPTH):
        j = i // N_MIXERS
        if i % N_MIXERS == 0:
            layers.append((_in_weights(w_in_a[j]), w_out_a[j].astype(jnp.bfloat16), None))
        else:
            order = jnp.argsort(b_f_b[j])
            w_in, w_f, b_f, w_out = _reorder_heads(order, w_in_b[j], w_f_b[j], b_f_b[j], w_out_b[j])
            layers.append((_in_weights(w_in, w_f), w_out.astype(jnp.bfloat16), b_f))

    outs = _proj(xf, nxt=layers[0][0])
    for i in range(DEPTH):
        _, w_out, b_f = layers[i]
        j = i // N_MIXERS
        if b_f is None:
            q, k, vt, z, nst = outs
            plain = _mixer_a_schedule(*_max_norms(nst, batch), rel_bias_a[j])
            g = _mixer_a(plain, q, k, vt, z, _mixer_a_bias(rel_bias_a[j]), batch, seq)
        else:
            q, k, vt, z, nst, lf = outs
            gk, gq, f_stats = _fox_prep(lf, b_f, batch, seq)
            g = _fox(_fox_schedule(*_max_norms(nst, batch), f_stats), q, gq, k, gk, vt, z, batch, seq)
        xf, *outs = _proj(xf, prev=(g, w_out, ln_g[i], ln_b[i]),
                          nxt=layers[i + 1][0] if i + 1 < DEPTH else None)
    return xf.reshape(batch, seq, d).astype(x.dtype)
```

```python
import functools
import math

import jax
import jax.numpy as jnp
from jax import lax
from jax.experimental import pallas as pl
from jax.experimental.pallas import tpu as pltpu

D_MODEL = 1024
N_HEADS = 16
HEAD_DIM = 64
MIX_WIDTH = N_HEADS * HEAD_DIM
CHUNK = 64
N_PAST_CHUNKS = 8
MAX_REL = 128
DEPTH = 4
N_MIXERS = 2
LN_EPS = 1e-5
DEEPNORM_ALPHA = (2.0 * DEPTH) ** 0.25
LOG2E = math.log2(math.e)
NEG = -1e30

LANES = 128
HEADS_PER_TILE = LANES // HEAD_DIM
N_HEAD_PAIRS = N_HEADS // HEADS_PER_TILE
BF16_ROWS = 16
N_SPLIT = 3
TQ = 256
Q_SUB_A = 16
Q_SUB_B = 4
TKV = 512
TM_PROJ = 512
ISSUE_LAG = 8
VMEM_LIMIT = 56 * 1024 * 1024
PRUNE_LOG2 = 152.0
NORM_SLACK = 1.0 + 2.0 ** -6
PLAIN_EXP_LOG2 = 60.0


def _params(*semantics):
    return pltpu.CompilerParams(dimension_semantics=semantics, vmem_limit_bytes=VMEM_LIMIT)


def _dot_nt(a, b):
    return lax.dot_general(a, b, (((1,), (1,)), ((), ())), preferred_element_type=jnp.float32)


def _split(x, n):
    pieces = []
    for _ in range(n):
        p = x.astype(jnp.bfloat16)
        pieces.append(p)
        x = x - p.astype(jnp.float32)
    return pieces


N_STATS = 8
LN_ROW_CHUNKS = 2


def _lagged(n, issue, consume, lag):
    pending = {}
    for c in range(n + lag):
        if c < n:
            pending[c] = issue(c)
        if c >= lag:
            consume(c - lag, pending.pop(c - lag))


def _out_proj_ln_rows(g_ref, x_ref, wo_ref, gam_ref, bet_ref, xo_ref):
    rows = TM_PROJ // LN_ROW_CHUNKS
    out = []

    def issue(c):
        return jnp.dot(g_ref[pl.ds(c * rows, rows), :], wo_ref[...], preferred_element_type=jnp.float32)

    def consume(c, y):
        sl = pl.ds(c * rows, rows)
        r = DEEPNORM_ALPHA * x_ref[sl, :] + y
        mu = jnp.mean(r, axis=-1, keepdims=True)
        d = r - mu
        var = jnp.mean(d * d, axis=-1, keepdims=True)
        xn = d * lax.rsqrt(var + LN_EPS) * gam_ref[...] + bet_ref[...]
        xo_ref[sl, :] = xn
        out.append(xn)

    _lagged(LN_ROW_CHUNKS, issue, consume, lag=1)
    return jnp.concatenate(out, axis=0)


def _in_proj_rows(x, w_ref, wvt_ref, hsel_ref, q_ref, k_ref, vt_ref, z_ref, nst_ref, gate_ref, with_norms):
    xb = x.astype(jnp.bfloat16)
    sq_max = []
    for g, o_ref in enumerate((q_ref, k_ref, z_ref)):
        o = jnp.dot(xb, w_ref[:, g * MIX_WIDTH:(g + 1) * MIX_WIDTH], preferred_element_type=jnp.float32)
        if g == 2:
            o = o / (1.0 + jnp.exp(-o))
        o = o.astype(jnp.bfloat16)
        o_ref[...] = o
        if g < 2 and not with_norms:
            sq_max.append(jnp.zeros((1, LANES), jnp.float32))
        elif g < 2:
            of = o.astype(jnp.float32)
            sq = jnp.dot((of * of).astype(jnp.bfloat16), hsel_ref[...], preferred_element_type=jnp.float32)
            sq_max.append(jnp.max(sq, axis=0, keepdims=True))
    vt = _dot_nt(wvt_ref[...], xb).astype(jnp.bfloat16)
    vt_ref[...] = vt
    v_abs = jnp.max(jnp.abs(vt.astype(jnp.float32)), axis=0, keepdims=True)
    v_max = functools.reduce(jnp.maximum, [v_abs[:, n * LANES:(n + 1) * LANES] for n in range(TM_PROJ // LANES)])
    nst_ref[...] = jnp.concatenate(sq_max + [v_max, jnp.zeros((N_STATS - 3, LANES), jnp.float32)], axis=0)
    if gate_ref is not None:
        gate_ref[...] = jnp.dot(xb, w_ref[:, 3 * MIX_WIDTH:], preferred_element_type=jnp.float32)


def _proj_kernel(*refs, has_out, has_in, has_gate, with_norms):
    refs = list(refs)
    n_in = (5 if has_out else 1) + (3 if has_in else 0)
    ins, outs = refs[:n_in], refs[n_in:]
    if has_out:
        x = _out_proj_ln_rows(*ins[:5], outs.pop(0))
        ins = ins[5:]
    else:
        x = ins.pop(0)[...]
    if has_in:
        _in_proj_rows(x, *ins, *outs[:5], outs[5] if has_gate else None, with_norms)


def _proj(xf, prev=None, nxt=None, with_norms=True):
    m = xf.shape[0]
    row_d = pl.BlockSpec((TM_PROJ, D_MODEL), lambda i: (i, 0))
    row_e = pl.BlockSpec((TM_PROJ, MIX_WIDTH), lambda i: (i, 0))

    def whole(shape):
        return pl.BlockSpec(shape, lambda i: (0,) * len(shape))

    args, in_specs, out_shape, out_specs = [], [], [], []
    if prev is not None:
        g, w_out, gamma, beta = prev
        args += [g, xf, w_out, gamma.reshape(1, D_MODEL), beta.reshape(1, D_MODEL)]
        in_specs += [row_e, row_d, whole((MIX_WIDTH, D_MODEL)), whole((1, D_MODEL)), whole((1, D_MODEL))]
        out_shape.append(jax.ShapeDtypeStruct((m, D_MODEL), jnp.float32))
        out_specs.append(row_d)
    else:
        args.append(xf)
        in_specs.append(row_d)
    has_gate = False
    if nxt is not None:
        w, wvt = nxt
        has_gate = w.shape[1] > 3 * MIX_WIDTH
        hsel = (jnp.arange(MIX_WIDTH)[:, None] // HEAD_DIM == jnp.arange(LANES)[None, :]).astype(jnp.bfloat16)
        args += [w, wvt, hsel]
        in_specs += [whole(w.shape), whole(wvt.shape), whole(hsel.shape)]
        tok = jax.ShapeDtypeStruct((m, MIX_WIDTH), jnp.bfloat16)
        out_shape += [tok, tok, jax.ShapeDtypeStruct((MIX_WIDTH, m), jnp.bfloat16), tok,
                      jax.ShapeDtypeStruct((m // TM_PROJ, N_STATS, LANES), jnp.float32)]
        out_specs += [row_e, row_e, pl.BlockSpec((MIX_WIDTH, TM_PROJ), lambda i: (0, i)), row_e,
                      pl.BlockSpec((None, N_STATS, LANES), lambda i: (i, 0, 0))]
        if has_gate:
            out_shape.append(jax.ShapeDtypeStruct((m, LANES), jnp.float32))
            out_specs.append(pl.BlockSpec((TM_PROJ, LANES), lambda i: (i, 0)))
    name = (("out_" if prev is not None else "") + ("in_" if nxt is not None else "") + "proj"
            + ("_gate" if has_gate else "") + ("" if with_norms or nxt is None else "_nostats"))
    return pl.pallas_call(
        functools.partial(_proj_kernel, has_out=prev is not None, has_in=nxt is not None, has_gate=has_gate,
                          with_norms=with_norms),
        grid=(m // TM_PROJ,),
        in_specs=in_specs,
        out_specs=out_specs,
        out_shape=out_shape,
        compiler_params=_params("arbitrary"),
        name=name,
    )(*args)


def _max_norms(nst, batch):
    nb = nst.shape[0] // batch
    n = jnp.sqrt(nst[:, :2, :N_HEADS]).reshape(batch, nb, 2, N_HEADS) * NORM_SLACK
    return n[:, :, 0], n[:, :, 1]


ACC_ROWS = HEAD_DIM + BF16_ROWS
U_ROWS = 8


def _head_masked_queries(q2):
    lane = lax.broadcasted_iota(jnp.int32, q2.shape, 1)
    zero = jnp.zeros_like(q2)
    return [jnp.where((lane >= HEAD_DIM * h) & (lane < HEAD_DIM * (h + 1)), q2, zero)
            for h in range(HEADS_PER_TILE)]


def _head_values(vt_blk, h):
    ones = jnp.ones((BF16_ROWS, vt_blk.shape[1]), vt_blk.dtype)
    return jnp.concatenate([vt_blk[HEAD_DIM * h:HEAD_DIM * (h + 1), :], ones], axis=0)


def _online_update(u_ref, acc_ref, c, a, vt_h):
    u = u_ref[c, 0:1, :]
    u_new = jnp.maximum(u, jnp.max(a, axis=0, keepdims=True))
    p = jnp.exp2(a - u_new).astype(jnp.bfloat16)
    acc_ref[c] = jnp.exp2(u - u_new) * acc_ref[c] + jnp.dot(vt_h, p, preferred_element_type=jnp.float32)
    u_ref[c, 0:1, :] = u_new


def _plain_update(u_ref, acc_ref, c, a, vt_h):
    del u_ref
    acc_ref[c] = acc_ref[c] + jnp.dot(vt_h, jnp.exp2(a).astype(jnp.bfloat16),
                                      preferred_element_type=jnp.float32)


def _reset(u_ref, acc_ref):
    u_ref[...] = jnp.full(u_ref.shape, NEG, jnp.float32)
    acc_ref[...] = jnp.zeros(acc_ref.shape, jnp.float32)


def _finish(acc_ref, z_ref, o_ref, r, row0):
    accs = [acc_ref[r * HEADS_PER_TILE + h] for h in range(HEADS_PER_TILE)]
    ot = jnp.concatenate([acc[:HEAD_DIM] / acc[HEAD_DIM:HEAD_DIM + 1] for acc in accs], axis=0)
    rows = pl.ds(row0, TQ)
    o_ref[rows, :] = (ot.T * z_ref[rows, :].astype(jnp.float32)).astype(o_ref.dtype)


def _attn_specs(seq, pair_major=False):
    def spec(block, index):
        if pair_major:
            return pl.BlockSpec(block, lambda hp, b, *_: index(b, hp))
        return pl.BlockSpec(block, lambda b, hp, *_: index(b, hp))

    seq_rows = spec((seq, LANES), lambda b, hp: (b, hp))
    seq_cols = spec((LANES, seq), lambda b, hp: (hp, b))
    return seq_rows, seq_cols, spec


def _state_scratch(q_sub):
    n = q_sub * HEADS_PER_TILE
    return [pltpu.VMEM((n, U_ROWS, TQ), jnp.float32), pltpu.VMEM((n, ACC_ROWS, TQ), jnp.float32)]


A_PAST = N_PAST_CHUNKS * CHUNK
A_WINDOW = A_PAST + TQ
A_BIAS_ROWS = 2 * A_PAST + TQ
A_BAND_LO = A_PAST - MAX_REL
A_BAND_HI = A_PAST + TQ + CHUNK


def _mixer_a_kernel(vok_ref, q_ref, k_ref, vt_ref, z_ref, bias_ref, o_ref, u_ref, acc_ref):
    n_groups = q_ref.shape[0] // (Q_SUB_A * TQ)
    v_ok = vok_ref[pl.program_id(1)]

    def group(gi, update):
        _reset(u_ref, acc_ref)

        def qrow(r):
            return pl.multiple_of((gi * Q_SUB_A + r) * TQ, TQ)

        def window(r):
            start = qrow(r) - A_PAST
            row0 = pl.multiple_of(jnp.maximum(start, 0), TQ)
            return row0, pl.multiple_of(row0 - start, TQ)

        def issue(c):
            r, h = divmod(c, HEADS_PER_TILE)
            row0, brow = window(r)
            qm = _head_masked_queries(q_ref[pl.ds(qrow(r), TQ), :])[h]
            return _dot_nt(k_ref[pl.ds(row0, A_WINDOW), :], qm) + bias_ref[h, pl.ds(brow, A_WINDOW), :]

        def consume(c, a):
            r, h = divmod(c, HEADS_PER_TILE)
            row0, _ = window(r)
            update(u_ref, acc_ref, c, a, _head_values(vt_ref[:, pl.ds(row0, A_WINDOW)], h))
            if h == HEADS_PER_TILE - 1:
                _finish(acc_ref, z_ref, o_ref, r, qrow(r))

        _lagged(Q_SUB_A * HEADS_PER_TILE, issue, consume, ISSUE_LAG)

    def body(gi, carry):
        group(gi, _plain_update)
        sums = acc_ref[:, HEAD_DIM:HEAD_DIM + 1, :]
        in_range = (sums >= 2.0 ** -PLAIN_EXP_LOG2) & (sums <= 2.0 ** PLAIN_EXP_LOG2)
        n_bad = jnp.sum(jnp.where(in_range, 0.0, 1.0))

        @pl.when(jnp.logical_or(n_bad > 0.0, v_ok == 0))
        def _():
            group(gi, _online_update)

        return carry

    lax.fori_loop(0, n_groups, body, 0)


A_BAND = A_BAND_HI - A_BAND_LO
A_STRIP = 1024
A_STRIP_OFF = 512


def _bias_band_kernel(s_ref, o_ref):
    x = jnp.broadcast_to(s_ref[0], (A_BAND, A_STRIP))
    o_ref[0] = pltpu.roll(x, 0, 1, stride=1, stride_axis=0)[:, A_STRIP_OFF:A_STRIP_OFF + TQ]


def _mixer_a_bias(rel_bias):
    assert A_STRIP_OFF >= A_BAND - 1 and A_STRIP_OFF + TQ <= A_STRIP
    rb = rel_bias.astype(jnp.float32) * LOG2E
    c = jnp.arange(A_BIAS_ROWS)[:, None]
    r = jnp.arange(TQ)[None, :]
    strip_dist = jnp.arange(A_STRIP) - A_STRIP_OFF + (A_PAST - A_BAND_LO)
    strip = rb[:, jnp.clip(strip_dist, -MAX_REL, MAX_REL) + MAX_REL].reshape(N_HEADS, 1, A_STRIP)
    band = pl.pallas_call(
        _bias_band_kernel,
        grid=(N_HEADS,),
        in_specs=[pl.BlockSpec((1, 1, A_STRIP), lambda h: (h, 0, 0))],
        out_specs=pl.BlockSpec((1, A_BAND, TQ), lambda h: (h, 0, 0)),
        out_shape=jax.ShapeDtypeStruct((N_HEADS, A_BAND, TQ), jnp.float32),
        compiler_params=_params("arbitrary"),
        name="bias_band",
    )(strip)
    far = jnp.broadcast_to(rb[:, 2 * MAX_REL][:, None, None], (N_HEADS, A_BAND_LO, TQ))
    late = jnp.zeros((N_HEADS, A_BIAS_ROWS - A_BAND_HI, TQ), jnp.float32)
    table = jnp.concatenate([far, band, late], axis=1)
    dchunk = r // CHUNK - (c - A_PAST) // CHUNK
    visible = (dchunk >= 0) & (dchunk <= N_PAST_CHUNKS)
    return jnp.where(visible[None], table, NEG)


def _mixer_a(nst, q, k, vt, z, bias, batch, seq):
    seq_rows, seq_cols, spec = _attn_specs(seq, pair_major=True)
    v_ok = (jnp.max(nst[:, 2, :].reshape(batch, -1), axis=1) <= 2.0 ** PLAIN_EXP_LOG2).astype(jnp.int32)
    return pl.pallas_call(
        _mixer_a_kernel,
        grid_spec=pltpu.PrefetchScalarGridSpec(
            num_scalar_prefetch=1,
            grid=(N_HEAD_PAIRS, batch),
            in_specs=[seq_rows, seq_rows, seq_cols, seq_rows,
                      spec((HEADS_PER_TILE, A_BIAS_ROWS, TQ), lambda b, hp: (hp, 0, 0))],
            out_specs=seq_rows,
            scratch_shapes=_state_scratch(Q_SUB_A)),
        out_shape=jax.ShapeDtypeStruct(q.shape, jnp.bfloat16),
        compiler_params=_params("arbitrary", "arbitrary"),
        name="mixer_a",
    )(v_ok, q, k, vt, z, bias)


TG_B = Q_SUB_B * TQ
MAX_JOINED = 3
KV_PER_G = TG_B // TKV
N_AUG = HEADS_PER_TILE * N_SPLIT
AUG_STRIDE = LANES // N_HEAD_PAIRS


def _fox_prep_kernel(lf_ref, bf_ref, tri_ref, sel_ref, ones_ref, gk_ref, gq_ref, st_ref, carry_ref):
    @pl.when(pl.program_id(1) == 0)
    def _():
        carry_ref[...] = jnp.zeros_like(carry_ref)

    pre = lf_ref[...] + bf_ref[...]
    logf = (jnp.minimum(pre, 0.0) - jnp.log(1.0 + jnp.exp(-jnp.abs(pre)))) * LOG2E
    tri = tri_ref[...]
    csum = sum(jnp.dot(tri, piece, preferred_element_type=jnp.float32) for piece in _split(logf, N_SPLIT))
    f = csum + carry_ref[...]
    carry_ref[...] = f[TKV - 1:TKV, :]
    pieces = _split(-f, N_SPLIT)
    for side, o_ref in enumerate((gk_ref, gq_ref)):
        o_ref[...] = (sum(jnp.dot(piece, sel_ref[side, n], preferred_element_type=jnp.float32)
                          for n, piece in enumerate(pieces)) + ones_ref[side:side + 1, :]).astype(jnp.bfloat16)
    st_ref[...] = jnp.concatenate(
        [f[0:1, :], f[TKV - 1:TKV, :], jnp.zeros((N_STATS - 2, LANES), jnp.float32)], axis=0)


def _fox_prep(lf, b_f, batch, seq):
    nb = seq // TKV
    bf = jnp.pad(b_f.astype(jnp.float32), (0, LANES - N_HEADS)).reshape(1, LANES)
    tri = (jnp.arange(TKV)[:, None] >= jnp.arange(TKV)[None, :]).astype(jnp.bfloat16)
    src = jnp.arange(LANES)[:, None]
    dst = jnp.arange(LANES)[None, :]
    base = (src // HEADS_PER_TILE) * AUG_STRIDE + (src % HEADS_PER_TILE) * N_SPLIT
    sel = jnp.stack([
        jnp.stack([jnp.where((dst == base + side * N_AUG + n) & (src < N_HEADS), 1.0 - 2.0 * side, 0.0)
                   for n in range(N_SPLIT)]) for side in range(2)]).astype(jnp.bfloat16)
    lane = jnp.arange(LANES) % AUG_STRIDE
    ones = jnp.stack([(lane >= N_AUG) & (lane < 2 * N_AUG), lane < N_AUG]).astype(jnp.float32)
    tokspec = pl.BlockSpec((TKV, LANES), lambda b, i: (b * nb + i, 0))
    aug = jax.ShapeDtypeStruct((batch * seq, LANES), jnp.bfloat16)
    return pl.pallas_call(
        _fox_prep_kernel,
        grid=(batch, nb),
        in_specs=[tokspec,
                  pl.BlockSpec((1, LANES), lambda b, i: (0, 0)),
                  pl.BlockSpec((TKV, TKV), lambda b, i: (0, 0)),
                  pl.BlockSpec((2, N_SPLIT, LANES, LANES), lambda b, i: (0, 0, 0, 0)),
                  pl.BlockSpec((2, LANES), lambda b, i: (0, 0))],
        out_specs=[tokspec, tokspec, pl.BlockSpec((None, None, N_STATS, LANES), lambda b, i: (b, i, 0, 0))],
        out_shape=[aug, aug, jax.ShapeDtypeStruct((batch, nb, N_STATS, LANES), jnp.float32)],
        scratch_shapes=[pltpu.VMEM((1, LANES), jnp.float32)],
        compiler_params=_params("arbitrary", "arbitrary"),
        name="fox_prep",
    )(lf, bf, tri, sel, ones)


def _fox_schedule(qn, kn, f_stats):
    b, nb = f_stats.shape[0], f_stats.shape[1]
    ng = nb // KV_PER_G
    f_first, f_last = f_stats[:, :, 0, :N_HEADS], f_stats[:, :, 1, :N_HEADS]
    qn_g = jnp.max(qn.reshape(b, ng, KV_PER_G, N_HEADS), axis=2)
    kn_g = jnp.max(kn.reshape(b, ng, KV_PER_G, N_HEADS), axis=2)
    f_first_g = f_first.reshape(b, ng, KV_PER_G, N_HEADS)[:, :, 0]
    bound = (qn_g[:, :, None] * kn[:, None, :] + f_first_g[:, :, None] - f_last[:, None, :]
             + (qn_g * kn_g)[:, :, None])
    i_idx = jnp.arange(ng)[None, :, None, None]
    j_idx = jnp.arange(nb)[None, None, :, None]
    skip = (bound < -PRUNE_LOG2) & (j_idx < i_idx * KV_PER_G)
    first = jnp.min(jnp.where(skip, nb, j_idx), axis=2)
    first = jnp.min(first.reshape(b, ng, N_HEAD_PAIRS, HEADS_PER_TILE), axis=-1)
    first = jnp.transpose(first, (0, 2, 1)).reshape(-1)
    worst = jnp.max(qn, axis=1) * jnp.max(kn, axis=1)
    plain = jnp.all((worst < PLAIN_EXP_LOG2).reshape(b, N_HEAD_PAIRS, HEADS_PER_TILE), axis=-1).reshape(-1)
    return jnp.concatenate([first, plain.astype(first.dtype)]).astype(jnp.int32)


def _fox_kernel(sched_ref, q_ref, gq_ref, k_ref, gk_ref, vt_ref, z_ref, o_ref, u_ref, acc_ref):
    hp = pl.program_id(1)
    n_groups = q_ref.shape[0] // TG_B
    pair = pl.program_id(0) * N_HEAD_PAIRS + hp
    plain = sched_ref[pl.num_programs(0) * N_HEAD_PAIRS * n_groups + pair]
    n_chains = Q_SUB_B * HEADS_PER_TILE

    def keys_at(row0, n):
        rows = pl.ds(row0, n)
        return jnp.concatenate([k_ref[rows, :], gk_ref[rows, :]], axis=1), vt_ref[:, rows]

    def group(gi, update):
        j_first = sched_ref[pair * n_groups + gi]

        def qrow(r):
            return pl.multiple_of(gi * TG_B + r * TQ, TQ)

        def qaug(c):
            r, h = divmod(c, HEADS_PER_TILE)
            rows = pl.ds(qrow(r), TQ)
            gq = gq_ref[rows, :]
            off = lax.broadcasted_iota(jnp.int32, (TQ, LANES), 1) - hp * AUG_STRIDE
            own = ((off >= N_SPLIT * h) & (off < N_SPLIT * (h + 1))) | \
                  ((off >= N_AUG + N_SPLIT * h) & (off < N_AUG + N_SPLIT * (h + 1)))
            return jnp.concatenate([_head_masked_queries(q_ref[rows, :])[h],
                                    jnp.where(own, gq, jnp.zeros_like(gq))], axis=1)

        _reset(u_ref, acc_ref)
        qa = [qaug(c) for c in range(n_chains)]

        def keys_block(j, n_steps):
            kaug, vt_blk = keys_at(pl.multiple_of(j * TKV, TKV), n_steps * TKV)
            _lagged(n_chains,
                    lambda c: _dot_nt(kaug, qa[c]),
                    lambda c, a: update(u_ref, acc_ref, c, a, _head_values(vt_blk, c % HEADS_PER_TILE)),
                    ISSUE_LAG)

        n_before = gi * KV_PER_G - j_first
        n_joined = jnp.where((n_before & 1) != 0, jnp.minimum(n_before, MAX_JOINED), jnp.minimum(n_before, 2))

        def two_steps(p, carry):
            keys_block(j_first + n_joined + 2 * p, 2)
            return carry

        lax.fori_loop(0, lax.shift_right_logical(n_before - n_joined, 1), two_steps, 0)

        def own_block(joined):
            kaug, vt_blk = keys_at(pl.multiple_of(gi * TG_B, TG_B), TG_B)
            s_loc = lax.broadcasted_iota(jnp.int32, (TQ, TQ), 0)
            t_loc = lax.broadcasted_iota(jnp.int32, (TQ, TQ), 1)
            n_extra = n_chains if joined else 0
            if joined:
                kaug_e, vt_e = keys_at(pl.multiple_of(j_first * TKV, TKV), joined * TKV)

            def issue(i):
                if i < n_extra:
                    return _dot_nt(kaug_e, qa[i])
                c = i - n_extra
                r = c // HEADS_PER_TILE
                a = _dot_nt(kaug[:(r + 1) * TQ], qa[c])
                diag = jnp.where(s_loc <= t_loc, a[r * TQ:], NEG)
                return diag if r == 0 else jnp.concatenate([a[:r * TQ], diag], axis=0)

            def consume(i, a):
                if i < n_extra:
                    update(u_ref, acc_ref, i, a, _head_values(vt_e, i % HEADS_PER_TILE))
                    return
                c = i - n_extra
                r, h = divmod(c, HEADS_PER_TILE)
                update(u_ref, acc_ref, c, a, _head_values(vt_blk[:, :(r + 1) * TQ], h))
                if h == HEADS_PER_TILE - 1:
                    _finish(acc_ref, z_ref, o_ref, r, qrow(r))

            _lagged(n_extra + n_chains, issue, consume, ISSUE_LAG)

        for joined in range(MAX_JOINED + 1):
            pl.when(n_joined == joined)(functools.partial(own_block, joined))

    def run(update):
        def body(gi, carry):
            group(gi, update)
            return carry
        lax.fori_loop(0, n_groups, body, 0)

    @pl.when(plain != 0)
    def _():
        run(_plain_update)

    @pl.when(plain == 0)
    def _():
        run(_online_update)


def _fox(sched, q, gq, k, gk, vt, z, batch, seq):
    seq_rows, seq_cols, spec = _attn_specs(seq)
    aug_seq = spec((seq, LANES), lambda b, hp: (b, 0))
    return pl.pallas_call(
        _fox_kernel,
        grid_spec=pltpu.PrefetchScalarGridSpec(
            num_scalar_prefetch=1,
            grid=(batch, N_HEAD_PAIRS),
            in_specs=[seq_rows, aug_seq, seq_rows, aug_seq, seq_cols, seq_rows],
            out_specs=seq_rows,
            scratch_shapes=_state_scratch(Q_SUB_B)),
        out_shape=jax.ShapeDtypeStruct(q.shape, jnp.bfloat16),
        compiler_params=_params("arbitrary", "arbitrary"),
        name="fox",
    )(sched, q, gq, k, gk, vt, z)


def _in_weights(w_in, w_gate=None):
    w_in = w_in.astype(jnp.float32)
    e = MIX_WIDTH
    cols = [w_in[:, :e] * (LOG2E / math.sqrt(HEAD_DIM)), w_in[:, e:2 * e], w_in[:, 3 * e:]]
    if w_gate is not None:
        cols.append(jnp.pad(w_gate.astype(jnp.float32), ((0, 0), (0, LANES - N_HEADS))))
    return jnp.concatenate(cols, axis=1).astype(jnp.bfloat16), w_in[:, 2 * e:3 * e].T.astype(jnp.bfloat16)


def _reorder_heads(order, w_in, w_f, b_f, w_out):
    d = w_in.shape[0]
    w_in = w_in.reshape(d, 4, N_HEADS, HEAD_DIM)[:, :, order].reshape(d, 4 * MIX_WIDTH)
    w_out = w_out.reshape(N_HEADS, HEAD_DIM, -1)[order].reshape(MIX_WIDTH, -1)
    return w_in, w_f[:, order], b_f[order], w_out


def kernel(x, w_in_a, rel_bias_a, w_out_a, w_in_b, w_f_b, b_f_b, w_out_b, ln_g, ln_b):
    batch, seq, d = x.shape
    assert seq % (Q_SUB_A * TQ) == 0 and seq % TG_B == 0 and TG_B % TKV == 0 and TKV == TM_PROJ
    assert seq >= A_WINDOW and N_HEAD_PAIRS * AUG_STRIDE == LANES and 2 * N_AUG <= AUG_STRIDE
    xf = x.reshape(batch * seq, d).astype(jnp.float32)

    layers = []
    for i in range(DEPTH):
        j = i // N_MIXERS
        if i % N_MIXERS == 0:
            layers.append((_in_weights(w_in_a[j]), w_out_a[j].astype(jnp.bfloat16), None))
        else:
            order = jnp.argsort(b_f_b[j])
            w_in, w_f, b_f, w_out = _reorder_heads(order, w_in_b[j], w_f_b[j], b_f_b[j], w_out_b[j])
            layers.append((_in_weights(w_in, w_f), w_out.astype(jnp.bfloat16), b_f))

    outs = _proj(xf, nxt=layers[0][0], with_norms=layers[0][2] is not None)
    for i in range(DEPTH):
        _, w_out, b_f = layers[i]
        j = i // N_MIXERS
        if b_f is None:
            q, k, vt, z, nst = outs
            g = _mixer_a(nst, q, k, vt, z, _mixer_a_bias(rel_bias_a[j]), batch, seq)
        else:
            q, k, vt, z, nst, lf = outs
            gk, gq, f_stats = _fox_prep(lf, b_f, batch, seq)
            g = _fox(_fox_schedule(*_max_norms(nst, batch), f_stats), q, gq, k, gk, vt, z, batch, seq)
        nxt = layers[i + 1] if i + 1 < DEPTH else None
        xf, *outs = _proj(xf, prev=(g, w_out, ln_g[i], ln_b[i]), nxt=None if nxt is None else nxt[0],
                          with_norms=nxt is not None and nxt[2] is not None)
    return xf.reshape(batch, seq, d).astype(x.dtype)
```
